```python
import jax, jax.numpy as jnp
from jax import lax
import numpy as np

D_MODEL = 2048
BATCH = 1
SEQ = 8192
DEPTH = 2

N_META = 16
EPS = 1e-6
D_FF = -(-(8 * D_MODEL) // (3 * 256)) * 256
N_A_LAYERS = (DEPTH + 1) // 2
N_C_LAYERS = DEPTH // 2
D_A = D_MODEL // 2
HGRN_HEAD = 128
HGRN_HEADS = D_A // HGRN_HEAD
CHUNK = 64
D_B = D_MODEL - D_A
SCONV_W = 3
D_C = D_MODEL // 2
RG_BLOCKS = 8
RG_BLOCK = D_C // RG_BLOCKS
RG_CONV_W = 4
RG_C = 8.0
ATT_HEADS = 8
ATT_HEAD_DIM = 128
D_D = ATT_HEADS * ATT_HEAD_DIM
KV_RANK = 256
IDX_HEADS = 16
IDX_DIM = 64
TOPK_MAX = 256
Q_BLOCK = 128

AB_SIZES = [D_A, D_A, D_A, D_A, D_B, D_B, D_B]
CD_SIZES = [D_C, D_C, D_D, KV_RANK, IDX_HEADS * IDX_DIM, IDX_DIM, IDX_HEADS]

kernel_name = 'hybrid_hgrn2_sconv_rglru_dsa_block'


def rms_norm(x, g):
    xf = x.astype(jnp.float32)
    y = xf * lax.rsqrt(jnp.mean(xf * xf, axis=-1, keepdims=True) + EPS)
    return (y * g.astype(jnp.float32)).astype(x.dtype)


def split_cols(a, sizes):
    idx = [int(v) for v in np.cumsum(sizes)[:-1]]
    return jnp.split(a, idx, axis=-1)


def causal_depthwise_conv(u, w):
    K = w.shape[0]
    return lax.conv_general_dilated(
        u, w[:, None, :].astype(u.dtype), window_strides=(1,), padding=[(K - 1, 0)],
        dimension_numbers=('NWC', 'WIO', 'NWC'), feature_group_count=u.shape[-1])


def hgrn2_chunked(q, f_logit, v, lb):
    B, T, H, dk = q.shape
    dv = v.shape[-1]
    f32 = jnp.float32
    f = lb + (1.0 - lb) * jax.nn.sigmoid(f_logit.astype(f32))
    log_f = jnp.log(f)
    k = 1.0 - f
    pad = (-N_META) % CHUNK
    padf = lambda a: jnp.pad(a.astype(f32), ((0, 0), (pad, 0), (0, 0), (0, 0)))
    q, k, v, log_f = padf(q), padf(k), padf(v), padf(log_f)
    n = (T + pad) // CHUNK
    chunks = lambda a: a.reshape(B, n, CHUNK, H, a.shape[-1]).transpose(1, 0, 3, 2, 4)
    tri = jnp.tril(jnp.ones((CHUNK, CHUNK), bool))[:, :, None]

    def step(S, xs):
        qc, kc, vc, lfc = xs
        b = jnp.cumsum(lfc, axis=2)
        inter = jnp.einsum('bhtk,bhkv->bhtv', qc * jnp.exp(b), S)
        diff = b[:, :, :, None, :] - b[:, :, None, :, :]
        decay = jnp.where(tri, jnp.exp(jnp.where(tri, diff, 0.0)), 0.0)
        att = jnp.einsum('bhtsk,bhsk->bhts', qc[:, :, :, None, :] * decay, kc)
        intra = jnp.einsum('bhts,bhsv->bhtv', att, vc)
        b_last = b[:, :, -1:, :]
        S_new = jnp.exp(b_last[:, :, 0, :])[..., None] * S + jnp.einsum(
            'bhsk,bhsv->bhkv', kc * jnp.exp(b_last - b), vc)
        return S_new, inter + intra

    S0 = jnp.zeros((B, H, dk, dv), f32)
    _, o = lax.scan(step, S0, (chunks(q), chunks(k), chunks(v), chunks(log_f)))
    o = o.transpose(1, 0, 3, 2, 4).reshape(B, n * CHUNK, H, dv)
    return o[:, pad:]


def rg_lru(u, w_a, b_a, w_i, b_i, lam):
    B, T, _ = u.shape
    f32 = jnp.float32
    ub = u.reshape(B, T, RG_BLOCKS, RG_BLOCK)
    r = jax.nn.sigmoid(jnp.einsum('btnc,ncd->btnd', ub, w_a.astype(f32)).reshape(B, T, D_C) + b_a.astype(f32))
    ig = jax.nn.sigmoid(jnp.einsum('btnc,ncd->btnd', ub, w_i.astype(f32)).reshape(B, T, D_C) + b_i.astype(f32))
    log_a = -RG_C * r * jax.nn.softplus(-lam.astype(f32))
    a = jnp.exp(log_a)
    xin = jnp.sqrt(-jnp.expm1(2.0 * log_a)) * (ig * u)

    def comb(left, right):
        a1, b1 = left
        a2, b2 = right
        return a1 * a2, a2 * b1 + b2

    _, h = lax.associative_scan(comb, (a, xin), axis=1)
    return h


def dsa_attention(q, ckv, iq, ik, iw, w_uk, w_uv):
    B, T = q.shape[0], q.shape[1]
    f32 = jnp.float32
    m = N_META
    L = T - m
    k_sel = min(TOPK_MAX, L // 4)
    n_blk = L // Q_BLOCK
    q_lat = jnp.einsum('bthd,rhd->bthr', q, w_uk) * (ATT_HEAD_DIM ** -0.5)
    c_meta, c_real = ckv[:, :m], ckv[:, m:]
    ik_real = ik[:, m:].astype(f32)
    tri = jnp.tril(jnp.ones((m, m), bool))
    s_mm = jnp.einsum('bqhr,bsr->bqhs', q_lat[:, :m], c_meta).astype(f32)
    p_mm = jax.nn.softmax(jnp.where(tri[None, :, None, :], s_mm, -jnp.inf), axis=-1).astype(ckv.dtype)
    o_meta = jnp.einsum('bqhs,bsr->bqhr', p_mm, c_meta)
    s_pos = jnp.arange(L)
    bidx = jnp.arange(B)[:, None, None]

    def to_blocks(a):
        return a[:, m:].reshape((B, n_blk, Q_BLOCK) + a.shape[2:]).swapaxes(0, 1)

    def block(args):
        ql, iqb, iwb, start = args
        t_pos = start + jnp.arange(Q_BLOCK)
        rel = jax.nn.relu(jnp.einsum('bqhd,bsd->bqhs', iqb.astype(f32), ik_real) * (IDX_DIM ** -0.5))
        score = jnp.einsum('bqhs,bqh->bqs', rel, iwb.astype(f32) * (IDX_HEADS ** -0.5))
        score = jnp.where((s_pos[None, :] <= t_pos[:, None])[None], score, -jnp.inf)
        _, idx = lax.top_k(score, k_sel)
        valid = idx <= t_pos[None, :, None]
        c_sel = c_real[bidx, idx]
        s_meta = jnp.einsum('bqhr,bmr->bqhm', ql, c_meta).astype(f32)
        s_sel = jnp.einsum('bqhr,bqkr->bqhk', ql, c_sel).astype(f32)
        s_all = jnp.concatenate([s_meta, jnp.where(valid[:, :, None, :], s_sel, -jnp.inf)], axis=-1)
        p = jax.nn.softmax(s_all, axis=-1).astype(ql.dtype)
        return (jnp.einsum('bqhm,bmr->bqhr', p[..., :m], c_meta)
                + jnp.einsum('bqhk,bqkr->bqhr', p[..., m:], c_sel))

    starts = jnp.arange(n_blk, dtype=jnp.int32) * Q_BLOCK
    o_blk = lax.map(block, (to_blocks(q_lat), to_blocks(iq), to_blocks(iw), starts))
    o_real = o_blk.swapaxes(0, 1).reshape(B, L, ATT_HEADS, KV_RANK)
    o_lat = jnp.concatenate([o_meta, o_real], axis=1)
    return jnp.einsum('bthr,rhd->bthd', o_lat, w_uv)


def mixer_ab(h, w_in, w_out, lb, out_norm_g, sconv_w):
    B, T, _ = h.shape
    q, f, i, g, sx, sb, sc = split_cols(h @ w_in, AB_SIZES)
    hd = lambda a: a.reshape(B, T, HGRN_HEADS, HGRN_HEAD)
    o = hgrn2_chunked(hd(q), hd(f), hd(i), lb.reshape(HGRN_HEADS, HGRN_HEAD))
    o = rms_norm(o, out_norm_g).reshape(B, T, D_A).astype(h.dtype) * jax.nn.silu(g)
    yb = sb * causal_depthwise_conv(sc * sx, sconv_w)
    return jnp.concatenate([o, yb], axis=-1) @ w_out


def mixer_cd(h, w_in, w_out, conv_w, conv_b, w_a, b_a, w_i, b_i, lam, kv_norm_g, w_uk, w_uv):
    B, T, _ = h.shape
    rx, ry, q, ckv, iq, ik, iw = split_cols(h @ w_in, CD_SIZES)
    u = causal_depthwise_conv(rx, conv_w) + conv_b
    hc = rg_lru(u.astype(jnp.float32), w_a, b_a, w_i, b_i, lam).astype(h.dtype) * jax.nn.gelu(ry)
    att = dsa_attention(q.reshape(B, T, ATT_HEADS, ATT_HEAD_DIM), rms_norm(ckv, kv_norm_g),
                        iq.reshape(B, T, IDX_HEADS, IDX_DIM), ik, iw, w_uk, w_uv)
    return jnp.concatenate([hc, att.reshape(B, T, D_D)], axis=-1) @ w_out


def swiglu(h, w1, w3, w2):
    return (jax.nn.silu(h @ w1) * (h @ w3)) @ w2


def setup_inputs(seed: int = 0) -> dict:
    key = jax.random.key(seed)
    ks = iter(jax.random.split(key, 40))
    f32 = jnp.float32
    nrm = lambda shape, scale: jax.random.normal(next(ks), shape, f32) * scale
    gain = lambda shape: 1.0 + 0.05 * jax.random.normal(next(ks), shape, f32)
    u = jax.random.uniform(next(ks), (N_C_LAYERS, D_C), f32, minval=0.9, maxval=0.999)
    s = u ** (1.0 / RG_C)
    return {
        'x': nrm((BATCH, SEQ, D_MODEL), 1.0),
        'meta_tokens': nrm((N_META, D_MODEL), 1.0),
        'ln_mix_pre': gain((DEPTH, D_MODEL)),
        'ln_mix_post': gain((DEPTH, D_MODEL)),
        'ln_ffn_pre': gain((DEPTH, D_MODEL)),
        'ln_ffn_post': gain((DEPTH, D_MODEL)),
        'ffn_w1': nrm((DEPTH, D_MODEL, D_FF), D_MODEL ** -0.5),
        'ffn_w3': nrm((DEPTH, D_MODEL, D_FF), D_MODEL ** -0.5),
        'ffn_w2': nrm((DEPTH, D_FF, D_MODEL), D_FF ** -0.5),
        'ab_w_in': nrm((N_A_LAYERS, D_MODEL, sum(AB_SIZES)), D_MODEL ** -0.5),
        'ab_w_out': nrm((N_A_LAYERS, D_A + D_B, D_MODEL), (D_A + D_B) ** -0.5),
        'hgrn_lb_logits': nrm((N_A_LAYERS + 1, D_A), 0.5),
        'hgrn_out_norm': gain((N_A_LAYERS, HGRN_HEAD)),
        'sconv_w': nrm((N_A_LAYERS, SCONV_W, D_B), SCONV_W ** -0.5),
        'cd_w_in': nrm((N_C_LAYERS, D_MODEL, sum(CD_SIZES)), D_MODEL ** -0.5),
        'cd_w_out': nrm((N_C_LAYERS, D_C + D_D, D_MODEL), (D_C + D_D) ** -0.5),
        'rg_conv_w': nrm((N_C_LAYERS, RG_CONV_W, D_C), RG_CONV_W ** -0.5),
        'rg_conv_b': nrm((N_C_LAYERS, D_C), 0.01),
        'rg_w_a': nrm((N_C_LAYERS, RG_BLOCKS, RG_BLOCK, RG_BLOCK), RG_BLOCK ** -0.5),
        'rg_b_a': nrm((N_C_LAYERS, D_C), 0.01),
        'rg_w_i': nrm((N_C_LAYERS, RG_BLOCKS, RG_BLOCK, RG_BLOCK), RG_BLOCK ** -0.5),
        'rg_b_i': nrm((N_C_LAYERS, D_C), 0.01),
        'rg_lambda': jnp.log(s) - jnp.log1p(-s),
        'mla_kv_norm': gain((N_C_LAYERS, KV_RANK)),
        'mla_w_uk': nrm((N_C_LAYERS, KV_RANK, ATT_HEADS, ATT_HEAD_DIM), KV_RANK ** -0.5),
        'mla_w_uv': nrm((N_C_LAYERS, KV_RANK, ATT_HEADS, ATT_HEAD_DIM), KV_RANK ** -0.5),
    }


def reference(x, meta_tokens, ln_mix_pre, ln_mix_post, ln_ffn_pre, ln_ffn_post, ffn_w1, ffn_w3, ffn_w2,
              ab_w_in, ab_w_out, hgrn_lb_logits, hgrn_out_norm, sconv_w,
              cd_w_in, cd_w_out, rg_conv_w, rg_conv_b, rg_w_a, rg_b_a, rg_w_i, rg_b_i, rg_lambda,
              mla_kv_norm, mla_w_uk, mla_w_uv):
    B = x.shape[0]
    meta = jnp.broadcast_to(meta_tokens.astype(x.dtype)[None], (B, N_META, x.shape[-1]))
    h = jnp.concatenate([meta, x], axis=1)
    lb_all = jnp.cumsum(jax.nn.softmax(hgrn_lb_logits.astype(jnp.float32), axis=0), axis=0)
    for l in range(DEPTH):
        j = l // 2
        hn = rms_norm(h, ln_mix_pre[l])
        if l % 2 == 0:
            mix = mixer_ab(hn, ab_w_in[j], ab_w_out[j], lb_all[j], hgrn_out_norm[j], sconv_w[j])
        else:
            mix = mixer_cd(hn, cd_w_in[j], cd_w_out[j], rg_conv_w[j], rg_conv_b[j], rg_w_a[j], rg_b_a[j],
                           rg_w_i[j], rg_b_i[j], rg_lambda[j], mla_kv_norm[j], mla_w_uk[j], mla_w_uv[j])
        h = h + rms_norm(mix, ln_mix_post[l])
        hn = rms_norm(h, ln_ffn_pre[l])
        h = h + rms_norm(swiglu(hn, ffn_w1[l], ffn_w3[l], ffn_w2[l]), ln_ffn_post[l])
    return h[:, N_META:]
```

```python
import functools

import jax
import jax.numpy as jnp
from jax import lax
from jax.experimental import pallas as pl
from jax.experimental.pallas import tpu as pltpu

F32 = jnp.float32
BF16 = jnp.bfloat16

EPS = 1e-6
N_META = 16
FRONT = 128
PAD_ROWS = FRONT - N_META
HGRN_HEAD = 128
HGRN_CHUNK = 128
HGRN_SUB = 16
RG_BLOCK = 128
RG_C = 8.0
ATT_HEADS = 8
ATT_HEAD_DIM = 128
KV_RANK = 256
IDX_HEADS = 16
IDX_DIM = 64
TOPK_MAX = 256
Q_BLOCK = 128
KEY_CHUNK = 512
LOG2_E = 1.4426950408889634
BISECT_STEPS = 4
BISECT_CAP = 64
NEG_BIG = -1e30
VMEM_LIMIT = 56 * 1024 * 1024


def _pick_tile(n, target, mult):
    best = None
    for t in range(mult, min(n, target) + 1, mult):
        if n % t == 0:
            best = t
    assert best is not None, (n, target, mult)
    return best


def _params(sem):
    return pltpu.CompilerParams(dimension_semantics=sem, vmem_limit_bytes=VMEM_LIMIT)


def _rms(x, g):
    return x * lax.rsqrt(jnp.mean(x * x, axis=-1, keepdims=True) + EPS) * g


def _norm_mm_kernel(x_ref, g_ref, w_ref, o_ref, xn_ref):
    @pl.when(pl.program_id(1) == 0)
    def _():
        xn_ref[...] = _rms(x_ref[...], g_ref[...]).astype(BF16)

    o_ref[...] = jnp.dot(xn_ref[...], w_ref[...], preferred_element_type=F32)


def _norm_mm(x, g, w, tn=512):
    m, d = x.shape
    n = w.shape[1]
    tm = _pick_tile(m, 640, 128)
    return pl.pallas_call(
        _norm_mm_kernel,
        grid=(m // tm, n // tn),
        in_specs=[pl.BlockSpec((tm, d), lambda i, j: (i, 0)),
                  pl.BlockSpec((1, d), lambda i, j: (0, 0)),
                  pl.BlockSpec((d, tn), lambda i, j: (0, j))],
        out_specs=pl.BlockSpec((tm, tn), lambda i, j: (i, j)),
        out_shape=jax.ShapeDtypeStruct((m, n), F32),
        scratch_shapes=[pltpu.VMEM((tm, d), BF16)],
        compiler_params=_params(("parallel", "arbitrary")),
        name="norm_proj",
    )(x, g.reshape(1, d), w)


def _norm_mm_t_kernel(x_ref, g_ref, wt_ref, o_ref, xn_ref):
    @pl.when(pl.program_id(1) == 0)
    def _():
        xn_ref[...] = _rms(x_ref[...], g_ref[...]).astype(BF16)

    o_ref[...] = lax.dot_general(wt_ref[...], xn_ref[...], (((1,), (1,)), ((), ())),
                                 preferred_element_type=F32)


def _norm_mm_t(x, g, wt, tc=512):
    m, d = x.shape
    n = wt.shape[0]
    tm = _pick_tile(m, 640, 128)
    return pl.pallas_call(
        _norm_mm_t_kernel,
        grid=(m // tm, n // tc),
        in_specs=[pl.BlockSpec((tm, d), lambda i, j: (i, 0)),
                  pl.BlockSpec((1, d), lambda i, j: (0, 0)),
                  pl.BlockSpec((tc, d), lambda i, j: (j, 0))],
        out_specs=pl.BlockSpec((tc, tm), lambda i, j: (j, i)),
        out_shape=jax.ShapeDtypeStruct((n, m), F32),
        scratch_shapes=[pltpu.VMEM((tm, d), BF16)],
        compiler_params=_params(("parallel", "arbitrary")),
        name="norm_proj_t",
    )(x, g.reshape(1, d), wt)


def _out_proj_kernel(a_ref, b_ref, wa_ref, wb_ref, h_ref, g_ref, o_ref):
    y = jnp.dot(a_ref[...], wa_ref[...], preferred_element_type=F32)
    y = y + jnp.dot(b_ref[...], wb_ref[...], preferred_element_type=F32)
    o_ref[...] = h_ref[...] + _rms(y, g_ref[...])


def _out_proj(a, b, wa, wb, h, g):
    m, d = h.shape
    ka, kb = a.shape[1], b.shape[1]
    tm = _pick_tile(m, 640, 128)
    return pl.pallas_call(
        _out_proj_kernel,
        grid=(m // tm,),
        in_specs=[pl.BlockSpec((tm, ka), lambda i: (i, 0)),
                  pl.BlockSpec((tm, kb), lambda i: (i, 0)),
                  pl.BlockSpec((ka, d), lambda i: (0, 0)),
                  pl.BlockSpec((kb, d), lambda i: (0, 0)),
                  pl.BlockSpec((tm, d), lambda i: (i, 0)),
                  pl.BlockSpec((1, d), lambda i: (0, 0))],
        out_specs=pl.BlockSpec((tm, d), lambda i: (i, 0)),
        out_shape=jax.ShapeDtypeStruct((m, d), F32),
        compiler_params=_params(("parallel",)),
        name="out_proj",
    )(a, b, wa, wb, h, g.reshape(1, d))


def _ffn_kernel(h_ref, gpre_ref, w1_ref, w3_ref, w2_ref, gpost_ref, o_ref, xn_ref, acc_ref):
    j = pl.program_id(1)

    @pl.when(j == 0)
    def _():
        xn_ref[...] = _rms(h_ref[...], gpre_ref[...]).astype(BF16)
        acc_ref[...] = jnp.zeros_like(acc_ref)

    xn = xn_ref[...]
    a = jnp.dot(xn, w1_ref[...], preferred_element_type=F32)
    b = jnp.dot(xn, w3_ref[...], preferred_element_type=F32)
    u = (a * jax.nn.sigmoid(a) * b).astype(BF16)
    acc_ref[...] += jnp.dot(u, w2_ref[...], preferred_element_type=F32)

    @pl.when(j == pl.num_programs(1) - 1)
    def _():
        o_ref[...] = h_ref[...] + _rms(acc_ref[...], gpost_ref[...])


def _ffn(h, gpre, w1, w3, w2, gpost, tf=512):
    m, d = h.shape
    f = w1.shape[1]
    tm = _pick_tile(m, 640, 128)
    return pl.pallas_call(
        _ffn_kernel,
        grid=(m // tm, f // tf),
        in_specs=[pl.BlockSpec((tm, d), lambda i, j: (i, 0)),
                  pl.BlockSpec((1, d), lambda i, j: (0, 0)),
                  pl.BlockSpec((d, tf), lambda i, j: (0, j)),
                  pl.BlockSpec((d, tf), lambda i, j: (0, j)),
                  pl.BlockSpec((tf, d), lambda i, j: (j, 0)),
                  pl.BlockSpec((1, d), lambda i, j: (0, 0))],
        out_specs=pl.BlockSpec((tm, d), lambda i, j: (i, 0)),
        out_shape=jax.ShapeDtypeStruct((m, d), F32),
        scratch_shapes=[pltpu.VMEM((tm, d), BF16), pltpu.VMEM((tm, d), F32)],
        compiler_params=_params(("parallel", "arbitrary")),
        name="ffn",
    )(h, gpre.reshape(1, d), w1, w3, w2, gpost.reshape(1, d))


def _hgrn_kernel(q_ref, f_ref, v_ref, gate_ref, lbl_ref, gn_ref, o_ref,
                 st_ref, kpad_ref, bpad_ref, vpad_ref, *, n_chunks, lb_row):
    c_rows, sub = HGRN_CHUNK, HGRN_SUB

    @pl.when(pl.program_id(1) == 0)
    def _():
        st_ref[...] = jnp.zeros_like(st_ref)
        zpad = jnp.zeros((sub, HGRN_HEAD), F32)
        kpad_ref[0:sub, :] = zpad
        bpad_ref[0:sub, :] = zpad
        vpad_ref[0:sub, :] = zpad

    logits = lbl_ref[...]
    ex = jnp.exp(logits - jnp.max(logits, axis=0, keepdims=True))
    lb = jnp.sum(ex[0:lb_row + 1, :], axis=0, keepdims=True) / jnp.sum(ex, axis=0, keepdims=True)

    r_i = lax.broadcasted_iota(jnp.int32, (c_rows, c_rows), 0)
    c_i = lax.broadcasted_iota(jnp.int32, (c_rows, c_rows), 1)
    tri = (r_i >= c_i).astype(F32)
    row_id = lax.broadcasted_iota(jnp.int32, (c_rows, 1), 0)
    rowmod = row_id % sub

    for c in range(n_chunks):
        rows = slice(c * c_rows, (c + 1) * c_rows)
        q = q_ref[rows, :]
        v = v_ref[rows, :]
        f = lb + (1.0 - lb) * jax.nn.sigmoid(f_ref[rows, :])
        k = 1.0 - f
        b = jnp.dot(tri, jnp.log(f), precision=lax.Precision.HIGHEST, preferred_element_type=F32)
        b_last = b[c_rows - 1:c_rows, :]

        st = st_ref[...]
        inter = lax.dot_general((q * jnp.exp(b)).astype(BF16), st.astype(BF16),
                                (((1,), (1,)), ((), ())), preferred_element_type=F32)
        kt = (k * jnp.exp(b_last - b)).astype(BF16)
        st_ref[...] = st * jnp.exp(b_last) + lax.dot_general(
            v.astype(BF16), kt, (((0,), (0,)), ((), ())), preferred_element_type=F32)

        att = [jnp.zeros((sub, c_rows), F32)]
        for i in range(1, c_rows // sub):
            lo = i * sub
            edge = b[lo - 1:lo, :]
            qi = (q[lo:lo + sub, :] * jnp.exp(b[lo:lo + sub, :] - edge)).astype(BF16)
            earlier = row_id < lo
            kj = jnp.where(earlier, k * jnp.exp(jnp.where(earlier, edge - b, 0.0)), 0.0).astype(BF16)
            att.append(lax.dot_general(qi, kj, (((1,), (1,)), ((), ())), preferred_element_type=F32))
        att = jnp.concatenate(att, axis=0).astype(BF16)
        o = inter + jnp.dot(att, v.astype(BF16), preferred_element_type=F32)

        kpad_ref[sub:sub + c_rows, :] = k
        bpad_ref[sub:sub + c_rows, :] = b
        vpad_ref[sub:sub + c_rows, :] = v
        for d in range(sub):
            kd = kpad_ref[sub - d:sub - d + c_rows, :]
            bd = bpad_ref[sub - d:sub - d + c_rows, :]
            vd = vpad_ref[sub - d:sub - d + c_rows, :]
            valid = rowmod >= d
            e = jnp.exp(jnp.where(valid, b - bd, 0.0))
            a = jnp.sum(q * kd * e, axis=-1, keepdims=True)
            o = o + jnp.where(valid, a, 0.0) * vd

        gate = gate_ref[rows, :]
        o_ref[rows, :] = (_rms(o, gn_ref[...]) * (gate * jax.nn.sigmoid(gate))).astype(BF16)


def _hgrn(p, lb_logits, gn, lb_row):
    m = p.shape[0]
    d_a = lb_logits.shape[1]
    heads = d_a // HGRN_HEAD
    tb = HGRN_CHUNK
    n_l = lb_logits.shape[0]
    col = lambda off: (lambda h, t: (t, off + h))
    kern = functools.partial(_hgrn_kernel, n_chunks=tb // HGRN_CHUNK, lb_row=lb_row)
    return pl.pallas_call(
        kern,
        grid=(heads, m // tb),
        in_specs=[pl.BlockSpec((tb, HGRN_HEAD), col(0)),
                  pl.BlockSpec((tb, HGRN_HEAD), col(heads)),
                  pl.BlockSpec((tb, HGRN_HEAD), col(2 * heads)),
                  pl.BlockSpec((tb, HGRN_HEAD), col(3 * heads)),
                  pl.BlockSpec((n_l, HGRN_HEAD), lambda h, t: (0, h)),
                  pl.BlockSpec((1, HGRN_HEAD), lambda h, t: (0, 0))],
        out_specs=pl.BlockSpec((tb, HGRN_HEAD), lambda h, t: (t, h)),
        out_shape=jax.ShapeDtypeStruct((m, d_a), BF16),
        scratch_shapes=[pltpu.VMEM((HGRN_HEAD, HGRN_HEAD), F32),
                        pltpu.VMEM((HGRN_SUB + HGRN_CHUNK, HGRN_HEAD), F32),
                        pltpu.VMEM((HGRN_SUB + HGRN_CHUNK, HGRN_HEAD), F32),
                        pltpu.VMEM((HGRN_SUB + HGRN_CHUNK, HGRN_HEAD), F32)],
        compiler_params=_params(("parallel", "arbitrary")),
        name="hgrn2",
    )(p, p, p, p, lb_logits, gn.reshape(1, HGRN_HEAD))


def _sconv_kernel(sx_ref, sb_ref, sc_ref, sxp_ref, scp_ref, w_ref, o_ref, ext_ref, *, taps):
    tm = sx_ref.shape[0]
    prev = sxp_ref[...] * scp_ref[...]
    ext_ref[0:8, :] = jnp.where(pl.program_id(0) > 0, prev, 0.0)
    ext_ref[8:8 + tm, :] = sx_ref[...] * sc_ref[...]
    y = jnp.zeros(sx_ref.shape, F32)
    for j in range(taps):
        s = 8 - (taps - 1) + j
        y = y + w_ref[j:j + 1, :] * ext_ref[s:s + tm, :]
    o_ref[...] = (sb_ref[...] * y).astype(BF16)


def _sconv(p, w, col0, width):
    m = p.shape[0]
    taps = w.shape[0]
    tm = _pick_tile(m, 640, 128)
    cb = col0 // width
    prev = lambda off: (lambda i: (jnp.maximum(i * (tm // 8) - 1, 0), off))
    return pl.pallas_call(
        functools.partial(_sconv_kernel, taps=taps),
        grid=(m // tm,),
        in_specs=[pl.BlockSpec((tm, width), lambda i: (i, cb)),
                  pl.BlockSpec((tm, width), lambda i: (i, cb + 1)),
                  pl.BlockSpec((tm, width), lambda i: (i, cb + 2)),
                  pl.BlockSpec((8, width), prev(cb)),
                  pl.BlockSpec((8, width), prev(cb + 2)),
                  pl.BlockSpec((taps, width), lambda i: (0, 0))],
        out_specs=pl.BlockSpec((tm, width), lambda i: (i, 0)),
        out_shape=jax.ShapeDtypeStruct((m, width), BF16),
        scratch_shapes=[pltpu.VMEM((tm + 8, width), F32)],
        compiler_params=_params(("parallel",)),
        name="sconv",
    )(p, p, p, p, p, w)


def _rglru_kernel(rx_ref, ry_ref, rxp_ref, cw_ref, cb_ref, wa_ref, ba_ref, wi_ref, bi_ref, lam_ref,
                  o_ref, ext_ref, a_ref, x_ref, hs_ref, h_ref, *, taps):
    tm, width = rx_ref.shape
    i = pl.program_id(0)

    @pl.when(i == 0)
    def _():
        h_ref[...] = jnp.zeros_like(h_ref)

    ext_ref[0:8, :] = jnp.where(i > 0, rxp_ref[...], 0.0)
    ext_ref[8:8 + tm, :] = rx_ref[...]
    u = jnp.zeros((tm, width), F32) + cb_ref[...]
    for j in range(taps):
        s = 8 - (taps - 1) + j
        u = u + cw_ref[j:j + 1, :] * ext_ref[s:s + tm, :]

    u_b = u.astype(BF16)
    r_parts, i_parts = [], []
    for n in range(width // RG_BLOCK):
        blk = slice(n * RG_BLOCK, (n + 1) * RG_BLOCK)
        r_parts.append(jnp.dot(u_b[:, blk], wa_ref[n], preferred_element_type=F32))
        i_parts.append(jnp.dot(u_b[:, blk], wi_ref[n], preferred_element_type=F32))
    r = jax.nn.sigmoid(jnp.concatenate(r_parts, axis=1) + ba_ref[...])
    ig = jax.nn.sigmoid(jnp.concatenate(i_parts, axis=1) + bi_ref[...])

    neg_lam = -lam_ref[...]
    softplus = jnp.maximum(neg_lam, 0.0) + jnp.log1p(jnp.exp(-jnp.abs(neg_lam)))
    log_a = -RG_C * r * softplus
    row = i * tm + lax.broadcasted_iota(jnp.int32, (tm, 1), 0)
    a = jnp.exp(log_a)
    xin = jnp.sqrt(1.0 - a * a) * (ig * u)
    a_ref[...] = a
    x_ref[...] = jnp.where(row >= PAD_ROWS, xin, 0.0)

    def group(gidx, h):
        base = pl.multiple_of(gidx * 8, 8)
        a8 = a_ref[pl.ds(base, 8), :]
        x8 = x_ref[pl.ds(base, 8), :]
        for rr in range(8):
            h = a8[rr:rr + 1, :] * h + x8[rr:rr + 1, :]
            hs_ref[pl.ds(base + rr, 1), :] = h
        return h

    h_ref[...] = lax.fori_loop(0, tm // 8, group, h_ref[...])
    o_ref[...] = (hs_ref[...] * jax.nn.gelu(ry_ref[...])).astype(BF16)


def _rglru(p, cw, cb, wa, ba, wi, bi, lam, width):
    m = p.shape[0]
    taps = cw.shape[0]
    tm = _pick_tile(m, 640, 128)
    nb = width // RG_BLOCK
    row = lambda v: v.reshape(1, width)
    full2 = lambda shape: pl.BlockSpec(shape, lambda i: (0, 0))
    full3 = lambda shape: pl.BlockSpec(shape, lambda i: (0, 0, 0))
    return pl.pallas_call(
        functools.partial(_rglru_kernel, taps=taps),
        grid=(m // tm,),
        in_specs=[pl.BlockSpec((tm, width), lambda i: (i, 0)),
                  pl.BlockSpec((tm, width), lambda i: (i, 1)),
                  pl.BlockSpec((8, width), lambda i: (jnp.maximum(i * (tm // 8) - 1, 0), 0)),
                  full2((taps, width)), full2((1, width)),
                  full3((nb, RG_BLOCK, RG_BLOCK)), full2((1, width)),
                  full3((nb, RG_BLOCK, RG_BLOCK)), full2((1, width)),
                  full2((1, width))],
        out_specs=pl.BlockSpec((tm, width), lambda i: (i, 0)),
        out_shape=jax.ShapeDtypeStruct((m, width), BF16),
        scratch_shapes=[pltpu.VMEM((tm + 8, width), F32),
                        pltpu.VMEM((tm, width), F32),
                        pltpu.VMEM((tm, width), F32),
                        pltpu.VMEM((tm, width), F32),
                        pltpu.VMEM((1, width), F32)],
        compiler_params=_params(("arbitrary",)),
        name="rglru",
    )(p, p, p, cw, row(cb), wa, row(ba), wi, row(bi), row(lam))


def _latent_kernel(c_ref, ct_ref, g_ref, gt_ref, o_ref, ot_ref):
    o_ref[...] = _rms(c_ref[...], g_ref[...]).astype(BF16)
    ct = ct_ref[...]
    ot_ref[...] = (ct * lax.rsqrt(jnp.mean(ct * ct, axis=0, keepdims=True) + EPS) * gt_ref[...]).astype(BF16)


def _latent(p, pt, g, col0, row0):
    m = p.shape[0]
    r = g.shape[0]
    tm = _pick_tile(m, 640, 128)
    return pl.pallas_call(
        _latent_kernel,
        grid=(m // tm,),
        in_specs=[pl.BlockSpec((tm, r), lambda i: (i, col0 // r)),
                  pl.BlockSpec((r, tm), lambda i: (row0 // r, i)),
                  pl.BlockSpec((1, r), lambda i: (0, 0)),
                  pl.BlockSpec((r, 1), lambda i: (0, 0))],
        out_specs=[pl.BlockSpec((tm, r), lambda i: (i, 0)),
                   pl.BlockSpec((r, tm), lambda i: (0, i))],
        out_shape=[jax.ShapeDtypeStruct((m, r), BF16), jax.ShapeDtypeStruct((r, m), BF16)],
        compiler_params=_params(("parallel",)),
        name="latent_norm",
    )(p, pt, g.reshape(1, r), g.reshape(r, 1))


def _dsa_kernel(qt_ref, iqt_ref, iwt_ref, c_ref, ct_ref, ik_ref, wuk_ref, wuvt_ref, o_ref,
                sc_ref, iqp_ref, qlt_ref, acc_ref, m_ref, l_ref, *, k_sel):
    i = pl.program_id(0)
    tq = Q_BLOCK
    n_heads = ATT_HEADS
    sub_blocks = KEY_CHUNK // tq
    n_chunks = (i * tq + KEY_CHUNK - 1) // KEY_CHUNK

    scale = ATT_HEAD_DIM ** -0.5 * LOG2_E
    for h in range(n_heads):
        qh = qt_ref[h * ATT_HEAD_DIM:(h + 1) * ATT_HEAD_DIM, :].astype(BF16)
        ql = jnp.dot(wuk_ref[h], qh, preferred_element_type=F32) * scale
        qlt_ref[:, h * tq:(h + 1) * tq] = ql.astype(BF16)

    iw = iwt_ref[...] * ((IDX_DIM ** -0.5) * (IDX_HEADS ** -0.5))
    k_loc = lax.broadcasted_iota(jnp.int32, (tq, tq), 0)
    q_loc = lax.broadcasted_iota(jnp.int32, (tq, tq), 1)

    for hp in range(IDX_HEADS // 2):
        pair = jnp.concatenate(
            [iqt_ref[(2 * hp) * IDX_DIM:(2 * hp + 1) * IDX_DIM, :],
             iqt_ref[(2 * hp + 1) * IDX_DIM:(2 * hp + 2) * IDX_DIM, :]], axis=1).astype(BF16)
        iqp_ref[hp, 0:IDX_DIM, :] = pair
        iqp_ref[hp, IDX_DIM:2 * IDX_DIM, :] = jnp.zeros_like(pair)

    def score_chunk(j, carry):
        for u in range(sub_blocks):
            kb = j * sub_blocks + 1 + u
            r0 = pl.multiple_of(kb * tq, tq)
            ikb = ik_ref[pl.ds(r0, tq), :]
            s = jnp.zeros((tq, tq), F32)
            for hp in range(IDX_HEADS // 2):
                x = jnp.dot(ikb, iqp_ref[hp], preferred_element_type=F32)
                s = s + jnp.maximum(x[:, 0:tq], 0.0) * iw[2 * hp:2 * hp + 1, :]
                s = s + jnp.maximum(x[:, tq:2 * tq], 0.0) * iw[2 * hp + 1:2 * hp + 2, :]
            visible = jnp.logical_or(kb < i, jnp.logical_and(kb == i, k_loc <= q_loc))
            sc_ref[pl.ds(r0, tq), :] = jnp.where(visible, s, -jnp.inf)
        return carry

    lax.fori_loop(0, n_chunks, score_chunk, 0)

    groups = KEY_CHUNK // 64

    def chunk_scores(j):
        r0 = pl.multiple_of(FRONT + j * KEY_CHUNK, FRONT)
        return sc_ref[pl.ds(r0, KEY_CHUNK), :].reshape(groups, 8, 8, tq)

    def count_ge(t):
        def body(j, cnt):
            kk = chunk_scores(j)
            for g in range(groups):
                cnt = jnp.where(kk[g] >= t, cnt + 1, cnt)
            return cnt
        cnt = lax.fori_loop(0, n_chunks, body, jnp.zeros((8, 8, tq), jnp.int32))
        return jnp.sum(jnp.sum(cnt, axis=0), axis=0, keepdims=True)

    def min_max(j, carry):
        mn, mx = carry
        kk = chunk_scores(j)
        for g in range(groups):
            mx = jnp.maximum(mx, kk[g])
            mn = jnp.minimum(mn, jnp.where(kk[g] == -jnp.inf, jnp.inf, kk[g]))
        return mn, mx

    mn, mx = lax.fori_loop(0, n_chunks, min_max,
                           (jnp.full((8, 8, tq), jnp.inf, F32), jnp.full((8, 8, tq), -jnp.inf, F32)))
    row_min = jnp.min(jnp.min(mn, axis=0), axis=0, keepdims=True)
    row_max = jnp.max(jnp.max(mx, axis=0), axis=0, keepdims=True)

    lane = lax.broadcasted_iota(jnp.int32, (1, tq), 1)
    n_visible = (i - 1) * tq + lane + 1
    k_row = jnp.minimum(k_sel, n_visible)

    def probe(mid, movable, lo, hi, cnt_lo):
        c = count_ge(mid)
        up = jnp.logical_and(c >= k_row, movable)
        down = jnp.logical_and(jnp.logical_not(up), movable)
        return jnp.where(up, mid, lo), jnp.where(down, mid, hi), jnp.where(up, c, cnt_lo)

    def midpoint(lo, hi, cnt_lo):
        mid = lo + 0.5 * (hi - lo)
        movable = jnp.logical_and(cnt_lo != k_row, jnp.logical_and(mid > lo, mid < hi))
        return mid, movable

    def any_lane(flag):
        return jnp.max(flag.astype(jnp.int32))

    lo, hi, cnt_lo = probe(row_max, n_visible > k_row, row_min, row_max, n_visible)

    def bisect_cond(state):
        return jnp.logical_and(state[0] > 0, state[1] < BISECT_CAP)

    def bisect_body(state):
        _, it, lo, hi, cnt_lo = state
        for _ in range(BISECT_STEPS):
            mid, movable = midpoint(lo, hi, cnt_lo)
            lo, hi, cnt_lo = probe(mid, movable, lo, hi, cnt_lo)
        return any_lane(midpoint(lo, hi, cnt_lo)[1]), it + 1, lo, hi, cnt_lo

    state = (any_lane(midpoint(lo, hi, cnt_lo)[1]), jnp.int32(0), lo, hi, cnt_lo)
    thr = lax.while_loop(bisect_cond, bisect_body, state)[2]

    qlt = qlt_ref[...]

    def attend(c_blk, ct_blk, allowed):
        s = jnp.dot(c_blk, qlt, preferred_element_type=F32)
        bias = jnp.where(allowed, 0.0, NEG_BIG)
        s = s + jnp.concatenate([bias] * n_heads, axis=1)
        m_old = m_ref[...]
        m_new = jnp.maximum(m_old, jnp.max(s, axis=0, keepdims=True))
        alpha = jnp.exp2(m_old - m_new)
        p = jnp.exp2(s - m_new)
        l_ref[...] = alpha * l_ref[...] + jnp.sum(p, axis=0, keepdims=True)
        acc_ref[...] = alpha * acc_ref[...] + jnp.dot(ct_blk, p.astype(BF16), preferred_element_type=F32)
        m_ref[...] = m_new

    m_ref[...] = jnp.full(m_ref.shape, NEG_BIG, F32)
    l_ref[...] = jnp.zeros_like(l_ref)
    acc_ref[...] = jnp.zeros_like(acc_ref)

    q_row = i * tq + q_loc
    allowed0 = jnp.logical_and(jnp.logical_or(k_loc >= PAD_ROWS, k_loc == q_row), k_loc <= q_row)
    attend(c_ref[0:FRONT, :], ct_ref[:, 0:FRONT], allowed0)

    def attend_chunk(j, carry):
        r0 = pl.multiple_of(FRONT + j * KEY_CHUNK, FRONT)
        allowed = sc_ref[pl.ds(r0, KEY_CHUNK), :] >= thr
        attend(c_ref[pl.ds(r0, KEY_CHUNK), :], ct_ref[:, pl.ds(r0, KEY_CHUNK)], allowed)
        return carry

    lax.fori_loop(0, n_chunks, attend_chunk, 0)

    o_lat = (acc_ref[...] / l_ref[...]).astype(BF16)
    for h in range(n_heads):
        oh = jnp.dot(wuvt_ref[h], o_lat[:, h * tq:(h + 1) * tq], preferred_element_type=F32)
        o_ref[:, h * ATT_HEAD_DIM:(h + 1) * ATT_HEAD_DIM] = oh.T.astype(BF16)


def _dsa(pt, c, ct, ik, wuk, wuvt, k_sel, q_row0, iq_row0, iw_row0):
    m = c.shape[0]
    tq = Q_BLOCK
    d_q = ATT_HEADS * ATT_HEAD_DIM
    d_iq = IDX_HEADS * IDX_DIM
    full2 = lambda shape: pl.BlockSpec(shape, lambda i: (0, 0))
    full3 = lambda shape: pl.BlockSpec(shape, lambda i: (0, 0, 0))
    return pl.pallas_call(
        functools.partial(_dsa_kernel, k_sel=k_sel),
        grid=(m // tq,),
        in_specs=[pl.BlockSpec((d_q, tq), lambda i: (q_row0 // d_q, i)),
                  pl.BlockSpec((d_iq, tq), lambda i: (iq_row0 // d_iq, i)),
                  pl.BlockSpec((IDX_HEADS, tq), lambda i: (iw_row0 // IDX_HEADS, i)),
                  full2(c.shape), full2(ct.shape), full2(ik.shape),
                  full3(wuk.shape), full3(wuvt.shape)],
        out_specs=pl.BlockSpec((tq, d_q), lambda i: (i, 0)),
        out_shape=jax.ShapeDtypeStruct((m, d_q), BF16),
        scratch_shapes=[pltpu.VMEM((m, tq), F32),
                        pltpu.VMEM((IDX_HEADS // 2, 2 * IDX_DIM, 2 * tq), BF16),
                        pltpu.VMEM((KV_RANK, ATT_HEADS * tq), BF16),
                        pltpu.VMEM((KV_RANK, ATT_HEADS * tq), F32),
                        pltpu.VMEM((1, ATT_HEADS * tq), F32),
                        pltpu.VMEM((1, ATT_HEADS * tq), F32)],
        compiler_params=_params(("arbitrary",)),
        name="dsa",
    )(pt, pt, pt, c, ct, ik, wuk, wuvt)


def _pad_cols(w, n):
    return jnp.pad(w, ((0, 0), (0, n - w.shape[1])))


def kernel(x, meta_tokens, ln_mix_pre, ln_mix_post, ln_ffn_pre, ln_ffn_post, ffn_w1, ffn_w3, ffn_w2,
           ab_w_in, ab_w_out, hgrn_lb_logits, hgrn_out_norm, sconv_w,
           cd_w_in, cd_w_out, rg_conv_w, rg_conv_b, rg_w_a, rg_b_a, rg_w_i, rg_b_i, rg_lambda,
           mla_kv_norm, mla_w_uk, mla_w_uv):
    assert x.shape[0] == 1
    seq, d = x.shape[1], x.shape[2]
    assert seq % KEY_CHUNK == 0
    d_a = hgrn_lb_logits.shape[1]
    d_b = sconv_w.shape[2]
    d_c = rg_lambda.shape[1]
    d_d = ATT_HEADS * ATT_HEAD_DIM
    d_iq = IDX_HEADS * IDX_DIM
    k_sel = min(TOPK_MAX, seq // 4)

    h = jnp.concatenate([jnp.zeros((PAD_ROWS, d), F32), meta_tokens.astype(F32), x[0]], axis=0)

    p0 = _norm_mm(h, ln_mix_pre[0], ab_w_in[0].astype(BF16))
    og = _hgrn(p0, hgrn_lb_logits, hgrn_out_norm[0], lb_row=0)
    yb = _sconv(p0, sconv_w[0], col0=4 * d_a, width=d_b)
    w_out = ab_w_out[0].astype(BF16)
    h = _out_proj(og, yb, w_out[:d_a], w_out[d_a:], h, ln_mix_post[0])
    h = _ffn(h, ln_ffn_pre[0], ffn_w1[0].astype(BF16), ffn_w3[0].astype(BF16),
             ffn_w2[0].astype(BF16), ln_ffn_post[0])

    w_in = cd_w_in[0]
    o_rx, o_ry, o_q, o_c = 0, d_c, 2 * d_c, 2 * d_c + d_d
    o_iq = o_c + KV_RANK
    o_ik = o_iq + d_iq
    o_iw = o_ik + IDX_DIM
    w_rows = jnp.concatenate([w_in[:, o_rx:o_q], w_in[:, o_c:o_iq], w_in[:, o_ik:o_iw]], axis=1)
    w_rows = _pad_cols(w_rows, -(-w_rows.shape[1] // 512) * 512).astype(BF16)
    w_cols = jnp.concatenate([w_in[:, o_q:o_c], w_in[:, o_iq:o_ik], w_in[:, o_c:o_iq], w_in[:, o_iw:]], axis=1)
    w_cols = _pad_cols(w_cols, -(-w_cols.shape[1] // 512) * 512).astype(BF16).T
    p1 = _norm_mm(h, ln_mix_pre[1], w_rows)
    p1t = _norm_mm_t(h, ln_mix_pre[1], w_cols)

    hc = _rglru(p1, rg_conv_w[0], rg_conv_b[0], rg_w_a[0].astype(BF16), rg_b_a[0],
                rg_w_i[0].astype(BF16), rg_b_i[0], rg_lambda[0], width=d_c)
    c, ct = _latent(p1, p1t, mla_kv_norm[0], col0=2 * d_c, row0=d_d + d_iq)
    ik = p1[:, 2 * d_c + KV_RANK:2 * d_c + KV_RANK + 2 * IDX_DIM].astype(BF16)
    wuk = jnp.transpose(mla_w_uk[0], (1, 0, 2)).astype(BF16)
    wuvt = jnp.transpose(mla_w_uv[0], (1, 2, 0)).astype(BF16)
    att = _dsa(p1t, c, ct, ik, wuk, wuvt, k_sel,
               q_row0=0, iq_row0=d_d, iw_row0=d_d + d_iq + KV_RANK)
    w_out = cd_w_out[0].astype(BF16)
    h = _out_proj(hc, att, w_out[:d_c], w_out[d_c:], h, ln_mix_post[1])
    h = _ffn(h, ln_ffn_pre[1], ffn_w1[1].astype(BF16), ffn_w3[1].astype(BF16),
             ffn_w2[1].astype(BF16), ln_ffn_post[1])
    return h[FRONT:][None]
```

```python
import functools

import jax
import jax.numpy as jnp
from jax import lax
from jax.experimental import pallas as pl
from jax.experimental.pallas import tpu as pltpu

F32 = jnp.float32
BF16 = jnp.bfloat16

EPS = 1e-6
N_META = 16
FRONT = 128
PAD_ROWS = FRONT - N_META
HGRN_HEAD = 128
HGRN_CHUNK = 128
HGRN_SUB = 16
RG_BLOCK = 128
RG_C = 8.0
ATT_HEADS = 8
ATT_HEAD_DIM = 128
KV_RANK = 256
IDX_HEADS = 16
IDX_DIM = 64
TOPK_MAX = 256
Q_BLOCK = 128
KEY_CHUNK = 512
ONES_ROWS = 16
LOG2_E = 1.4426950408889634
BISECT_STEPS = 4
BISECT_CAP = 64
NEG_BIG = -1e30
VMEM_LIMIT = 56 * 1024 * 1024


def _pick_tile(n, target, mult):
    best = None
    for t in range(mult, min(n, target) + 1, mult):
        if n % t == 0:
            best = t
    assert best is not None, (n, target, mult)
    return best


def _params(sem):
    return pltpu.CompilerParams(dimension_semantics=sem, vmem_limit_bytes=VMEM_LIMIT)


def _rms(x, g):
    return x * lax.rsqrt(jnp.mean(x * x, axis=-1, keepdims=True) + EPS) * g


def _norm_mm_kernel(x_ref, g_ref, w_ref, o_ref, xn_ref):
    @pl.when(pl.program_id(1) == 0)
    def _():
        xn_ref[...] = _rms(x_ref[...], g_ref[...]).astype(BF16)

    o_ref[...] = jnp.dot(xn_ref[...], w_ref[...], preferred_element_type=F32)


def _norm_mm(x, g, w, tn=512):
    m, d = x.shape
    n = w.shape[1]
    tm = _pick_tile(m, 640, 128)
    return pl.pallas_call(
        _norm_mm_kernel,
        grid=(m // tm, n // tn),
        in_specs=[pl.BlockSpec((tm, d), lambda i, j: (i, 0)),
                  pl.BlockSpec((1, d), lambda i, j: (0, 0)),
                  pl.BlockSpec((d, tn), lambda i, j: (0, j))],
        out_specs=pl.BlockSpec((tm, tn), lambda i, j: (i, j)),
        out_shape=jax.ShapeDtypeStruct((m, n), F32),
        scratch_shapes=[pltpu.VMEM((tm, d), BF16)],
        compiler_params=_params(("parallel", "arbitrary")),
        name="norm_proj",
    )(x, g.reshape(1, d), w)


def _norm_mm_t_kernel(x_ref, g_ref, wt_ref, o_ref, xn_ref):
    @pl.when(pl.program_id(1) == 0)
    def _():
        xn_ref[...] = _rms(x_ref[...], g_ref[...]).astype(BF16)

    o_ref[...] = lax.dot_general(wt_ref[...], xn_ref[...], (((1,), (1,)), ((), ())),
                                 preferred_element_type=F32)


def _norm_mm_t(x, g, wt, tc=512):
    m, d = x.shape
    n = wt.shape[0]
    tm = _pick_tile(m, 640, 128)
    return pl.pallas_call(
        _norm_mm_t_kernel,
        grid=(m // tm, n // tc),
        in_specs=[pl.BlockSpec((tm, d), lambda i, j: (i, 0)),
                  pl.BlockSpec((1, d), lambda i, j: (0, 0)),
                  pl.BlockSpec((tc, d), lambda i, j: (j, 0))],
        out_specs=pl.BlockSpec((tc, tm), lambda i, j: (j, i)),
        out_shape=jax.ShapeDtypeStruct((n, m), F32),
        scratch_shapes=[pltpu.VMEM((tm, d), BF16)],
        compiler_params=_params(("parallel", "arbitrary")),
        name="norm_proj_t",
    )(x, g.reshape(1, d), wt)


def _out_proj_kernel(a_ref, b_ref, wa_ref, wb_ref, h_ref, g_ref, o_ref):
    y = jnp.dot(a_ref[...], wa_ref[...], preferred_element_type=F32)
    y = y + jnp.dot(b_ref[...], wb_ref[...], preferred_element_type=F32)
    o_ref[...] = h_ref[...] + _rms(y, g_ref[...])


def _out_proj(a, b, wa, wb, h, g):
    m, d = h.shape
    ka, kb = a.shape[1], b.shape[1]
    tm = _pick_tile(m, 640, 128)
    return pl.pallas_call(
        _out_proj_kernel,
        grid=(m // tm,),
        in_specs=[pl.BlockSpec((tm, ka), lambda i: (i, 0)),
                  pl.BlockSpec((tm, kb), lambda i: (i, 0)),
                  pl.BlockSpec((ka, d), lambda i: (0, 0)),
                  pl.BlockSpec((kb, d), lambda i: (0, 0)),
                  pl.BlockSpec((tm, d), lambda i: (i, 0)),
                  pl.BlockSpec((1, d), lambda i: (0, 0))],
        out_specs=pl.BlockSpec((tm, d), lambda i: (i, 0)),
        out_shape=jax.ShapeDtypeStruct((m, d), F32),
        compiler_params=_params(("parallel",)),
        name="out_proj",
    )(a, b, wa, wb, h, g.reshape(1, d))


def _ffn_kernel(h_ref, gpre_ref, w1_ref, w3_ref, w2_ref, gpost_ref, o_ref, xn_ref, acc_ref):
    j = pl.program_id(1)

    @pl.when(j == 0)
    def _():
        xn_ref[...] = _rms(h_ref[...], gpre_ref[...]).astype(BF16)
        acc_ref[...] = jnp.zeros_like(acc_ref)

    xn = xn_ref[...]
    a = jnp.dot(xn, w1_ref[...], preferred_element_type=F32)
    b = jnp.dot(xn, w3_ref[...], preferred_element_type=F32)
    u = (a * jax.nn.sigmoid(a) * b).astype(BF16)
    acc_ref[...] += jnp.dot(u, w2_ref[...], preferred_element_type=F32)

    @pl.when(j == pl.num_programs(1) - 1)
    def _():
        o_ref[...] = h_ref[...] + _rms(acc_ref[...], gpost_ref[...])


def _ffn(h, gpre, w1, w3, w2, gpost, tf=512):
    m, d = h.shape
    f = w1.shape[1]
    tm = _pick_tile(m, 640, 128)
    return pl.pallas_call(
        _ffn_kernel,
        grid=(m // tm, f // tf),
        in_specs=[pl.BlockSpec((tm, d), lambda i, j: (i, 0)),
                  pl.BlockSpec((1, d), lambda i, j: (0, 0)),
                  pl.BlockSpec((d, tf), lambda i, j: (0, j)),
                  pl.BlockSpec((d, tf), lambda i, j: (0, j)),
                  pl.BlockSpec((tf, d), lambda i, j: (j, 0)),
                  pl.BlockSpec((1, d), lambda i, j: (0, 0))],
        out_specs=pl.BlockSpec((tm, d), lambda i, j: (i, 0)),
        out_shape=jax.ShapeDtypeStruct((m, d), F32),
        scratch_shapes=[pltpu.VMEM((tm, d), BF16), pltpu.VMEM((tm, d), F32)],
        compiler_params=_params(("parallel", "arbitrary")),
        name="ffn",
    )(h, gpre.reshape(1, d), w1, w3, w2, gpost.reshape(1, d))


def _cumsum_rows(tri, x):
    hi = x.astype(BF16)
    rest = x - hi.astype(F32)
    mid = rest.astype(BF16)
    lo = (rest - mid.astype(F32)).astype(BF16)
    return (jnp.dot(tri, hi, preferred_element_type=F32) + jnp.dot(tri, mid, preferred_element_type=F32)
            + jnp.dot(tri, lo, preferred_element_type=F32))


def _hgrn_kernel(q_ref, f_ref, v_ref, gate_ref, lbl_ref, gn_ref, o_ref,
                 st_ref, kpad_ref, bpad_ref, vpad_ref, *, n_heads, lb_row):
    c_rows, sub = HGRN_CHUNK, HGRN_SUB

    @pl.when(pl.program_id(1) == 0)
    def _():
        st_ref[...] = jnp.zeros_like(st_ref)
        zpad = jnp.zeros((n_heads, sub, HGRN_HEAD), F32)
        kpad_ref[:, 0:sub, :] = zpad
        bpad_ref[:, 0:sub, :] = zpad
        vpad_ref[:, 0:sub, :] = zpad

    logits = lbl_ref[...]
    ex = jnp.exp(logits - jnp.max(logits, axis=0, keepdims=True))
    lb_all = jnp.sum(ex[0:lb_row + 1, :], axis=0, keepdims=True) / jnp.sum(ex, axis=0, keepdims=True)

    r_i = lax.broadcasted_iota(jnp.int32, (c_rows, c_rows), 0)
    c_i = lax.broadcasted_iota(jnp.int32, (c_rows, c_rows), 1)
    tri = (r_i >= c_i).astype(BF16)
    row_id = lax.broadcasted_iota(jnp.int32, (c_rows, 1), 0)
    rowmod = row_id % sub

    for hh in range(n_heads):
        cols = slice(hh * HGRN_HEAD, (hh + 1) * HGRN_HEAD)
        lb = lb_all[:, cols]
        q = q_ref[:, cols]
        v = v_ref[:, cols]
        f = lb + (1.0 - lb) * jax.nn.sigmoid(f_ref[:, cols])
        k = 1.0 - f
        b = _cumsum_rows(tri, jnp.log2(f))
        b_last = b[c_rows - 1:c_rows, :]

        st = st_ref[hh]
        inter = lax.dot_general((q * jnp.exp2(b)).astype(BF16), st.astype(BF16),
                                (((1,), (1,)), ((), ())), preferred_element_type=F32)
        kt = (k * jnp.exp2(b_last - b)).astype(BF16)
        st_ref[hh] = st * jnp.exp2(b_last) + lax.dot_general(
            v.astype(BF16), kt, (((0,), (0,)), ((), ())), preferred_element_type=F32)

        att = [jnp.zeros((sub, c_rows), F32)]
        for i in range(1, c_rows // sub):
            lo = i * sub
            edge = b[lo - 1:lo, :]
            qi = (q[lo:lo + sub, :] * jnp.exp2(b[lo:lo + sub, :] - edge)).astype(BF16)
            earlier = row_id < lo
            kj = jnp.where(earlier, k * jnp.exp2(jnp.where(earlier, edge - b, 0.0)), 0.0).astype(BF16)
            att.append(lax.dot_general(qi, kj, (((1,), (1,)), ((), ())), preferred_element_type=F32))
        att = jnp.concatenate(att, axis=0).astype(BF16)
        o = inter + jnp.dot(att, v.astype(BF16), preferred_element_type=F32)

        o = o + jnp.sum(q * k, axis=-1, keepdims=True) * v
        kpad_ref[hh, sub:sub + c_rows, :] = k
        bpad_ref[hh, sub:sub + c_rows, :] = b
        vpad_ref[hh, sub:sub + c_rows, :] = v
        for d in range(1, sub):
            kd = kpad_ref[hh, sub - d:sub - d + c_rows, :]
            bd = bpad_ref[hh, sub - d:sub - d + c_rows, :]
            vd = vpad_ref[hh, sub - d:sub - d + c_rows, :]
            valid = rowmod >= d
            e = jnp.exp2(jnp.where(valid, b - bd, 0.0))
            a = jnp.sum(q * kd * e, axis=-1, keepdims=True)
            o = o + jnp.where(valid, a, 0.0) * vd

        gate = gate_ref[:, cols]
        o_ref[:, cols] = (_rms(o, gn_ref[...]) * (gate * jax.nn.sigmoid(gate))).astype(BF16)


def _hgrn(p, lb_logits, gn, lb_row):
    m = p.shape[0]
    d_a = lb_logits.shape[1]
    hp = 2
    width = hp * HGRN_HEAD
    groups = d_a // width
    tb = HGRN_CHUNK
    n_l = lb_logits.shape[0]
    col = lambda off: (lambda g, t: (t, off + g))
    kern = functools.partial(_hgrn_kernel, n_heads=hp, lb_row=lb_row)
    return pl.pallas_call(
        kern,
        grid=(groups, m // tb),
        in_specs=[pl.BlockSpec((tb, width), col(0)),
                  pl.BlockSpec((tb, width), col(groups)),
                  pl.BlockSpec((tb, width), col(2 * groups)),
                  pl.BlockSpec((tb, width), col(3 * groups)),
                  pl.BlockSpec((n_l, width), lambda g, t: (0, g)),
                  pl.BlockSpec((1, HGRN_HEAD), lambda g, t: (0, 0))],
        out_specs=pl.BlockSpec((tb, width), lambda g, t: (t, g)),
        out_shape=jax.ShapeDtypeStruct((m, d_a), BF16),
        scratch_shapes=[pltpu.VMEM((hp, HGRN_HEAD, HGRN_HEAD), F32),
                        pltpu.VMEM((hp, HGRN_SUB + HGRN_CHUNK, HGRN_HEAD), F32),
                        pltpu.VMEM((hp, HGRN_SUB + HGRN_CHUNK, HGRN_HEAD), F32),
                        pltpu.VMEM((hp, HGRN_SUB + HGRN_CHUNK, HGRN_HEAD), F32)],
        compiler_params=_params(("parallel", "arbitrary")),
        name="hgrn2",
    )(p, p, p, p, lb_logits, gn.reshape(1, HGRN_HEAD))


def _sconv_kernel(sx_ref, sb_ref, sc_ref, sxp_ref, scp_ref, w_ref, o_ref, ext_ref, *, taps):
    tm = sx_ref.shape[0]
    prev = sxp_ref[...] * scp_ref[...]
    ext_ref[0:8, :] = jnp.where(pl.program_id(0) > 0, prev, 0.0)
    ext_ref[8:8 + tm, :] = sx_ref[...] * sc_ref[...]
    y = jnp.zeros(sx_ref.shape, F32)
    for j in range(taps):
        s = 8 - (taps - 1) + j
        y = y + w_ref[j:j + 1, :] * ext_ref[s:s + tm, :]
    o_ref[...] = (sb_ref[...] * y).astype(BF16)


def _sconv(p, w, col0, width):
    m = p.shape[0]
    taps = w.shape[0]
    tm = _pick_tile(m, 640, 128)
    cb = col0 // width
    prev = lambda off: (lambda i: (jnp.maximum(i * (tm // 8) - 1, 0), off))
    return pl.pallas_call(
        functools.partial(_sconv_kernel, taps=taps),
        grid=(m // tm,),
        in_specs=[pl.BlockSpec((tm, width), lambda i: (i, cb)),
                  pl.BlockSpec((tm, width), lambda i: (i, cb + 1)),
                  pl.BlockSpec((tm, width), lambda i: (i, cb + 2)),
                  pl.BlockSpec((8, width), prev(cb)),
                  pl.BlockSpec((8, width), prev(cb + 2)),
                  pl.BlockSpec((taps, width), lambda i: (0, 0))],
        out_specs=pl.BlockSpec((tm, width), lambda i: (i, 0)),
        out_shape=jax.ShapeDtypeStruct((m, width), BF16),
        scratch_shapes=[pltpu.VMEM((tm + 8, width), F32)],
        compiler_params=_params(("parallel",)),
        name="sconv",
    )(p, p, p, p, p, w)


def _rglru_kernel(rx_ref, ry_ref, rxp_ref, cw_ref, cb_ref, wa_ref, ba_ref, wi_ref, bi_ref, lam_ref,
                  o_ref, ext_ref, a_ref, x_ref, hs_ref, h_ref, *, taps):
    tm, width = rx_ref.shape
    i = pl.program_id(0)

    @pl.when(i == 0)
    def _():
        h_ref[...] = jnp.zeros_like(h_ref)

    ext_ref[0:8, :] = jnp.where(i > 0, rxp_ref[...], 0.0)
    ext_ref[8:8 + tm, :] = rx_ref[...]
    u = jnp.zeros((tm, width), F32) + cb_ref[...]
    for j in range(taps):
        s = 8 - (taps - 1) + j
        u = u + cw_ref[j:j + 1, :] * ext_ref[s:s + tm, :]

    u_b = u.astype(BF16)
    r_parts, i_parts = [], []
    for n in range(width // RG_BLOCK):
        blk = slice(n * RG_BLOCK, (n + 1) * RG_BLOCK)
        r_parts.append(jnp.dot(u_b[:, blk], wa_ref[n], preferred_element_type=F32))
        i_parts.append(jnp.dot(u_b[:, blk], wi_ref[n], preferred_element_type=F32))
    r = jax.nn.sigmoid(jnp.concatenate(r_parts, axis=1) + ba_ref[...])
    ig = jax.nn.sigmoid(jnp.concatenate(i_parts, axis=1) + bi_ref[...])

    neg_lam = -lam_ref[...]
    softplus = jnp.maximum(neg_lam, 0.0) + jnp.log1p(jnp.exp(-jnp.abs(neg_lam)))
    log_a = -RG_C * r * softplus
    row = i * tm + lax.broadcasted_iota(jnp.int32, (tm, 1), 0)
    a = jnp.exp(log_a)
    xin = jnp.sqrt(1.0 - a * a) * (ig * u)
    a_ref[...] = a
    x_ref[...] = jnp.where(row >= PAD_ROWS, xin, 0.0)

    def group(gidx, h):
        base = pl.multiple_of(gidx * 8, 8)
        a8 = a_ref[pl.ds(base, 8), :]
        x8 = x_ref[pl.ds(base, 8), :]
        for rr in range(8):
            h = a8[rr:rr + 1, :] * h + x8[rr:rr + 1, :]
            hs_ref[pl.ds(base + rr, 1), :] = h
        return h

    h_ref[...] = lax.fori_loop(0, tm // 8, group, h_ref[...])
    o_ref[...] = (hs_ref[...] * jax.nn.gelu(ry_ref[...])).astype(BF16)


def _rglru(p, cw, cb, wa, ba, wi, bi, lam, width):
    m = p.shape[0]
    taps = cw.shape[0]
    tm = _pick_tile(m, 640, 128)
    nb = width // RG_BLOCK
    row = lambda v: v.reshape(1, width)
    full2 = lambda shape: pl.BlockSpec(shape, lambda i: (0, 0))
    full3 = lambda shape: pl.BlockSpec(shape, lambda i: (0, 0, 0))
    return pl.pallas_call(
        functools.partial(_rglru_kernel, taps=taps),
        grid=(m // tm,),
        in_specs=[pl.BlockSpec((tm, width), lambda i: (i, 0)),
                  pl.BlockSpec((tm, width), lambda i: (i, 1)),
                  pl.BlockSpec((8, width), lambda i: (jnp.maximum(i * (tm // 8) - 1, 0), 0)),
                  full2((taps, width)), full2((1, width)),
                  full3((nb, RG_BLOCK, RG_BLOCK)), full2((1, width)),
                  full3((nb, RG_BLOCK, RG_BLOCK)), full2((1, width)),
                  full2((1, width))],
        out_specs=pl.BlockSpec((tm, width), lambda i: (i, 0)),
        out_shape=jax.ShapeDtypeStruct((m, width), BF16),
        scratch_shapes=[pltpu.VMEM((tm + 8, width), F32),
                        pltpu.VMEM((tm, width), F32),
                        pltpu.VMEM((tm, width), F32),
                        pltpu.VMEM((tm, width), F32),
                        pltpu.VMEM((1, width), F32)],
        compiler_params=_params(("arbitrary",)),
        name="rglru",
    )(p, p, p, cw, row(cb), wa, row(ba), wi, row(bi), row(lam))


def _latent_kernel(c_ref, ct_ref, g_ref, gt_ref, o_ref, ot_ref):
    o_ref[...] = _rms(c_ref[...], g_ref[...]).astype(BF16)
    ct = ct_ref[...]
    ot_ref[0:KV_RANK, :] = (ct * lax.rsqrt(jnp.mean(ct * ct, axis=0, keepdims=True) + EPS)
                            * gt_ref[...]).astype(BF16)
    first = lax.broadcasted_iota(jnp.int32, (ONES_ROWS, ct.shape[1]), 0) == 0
    ot_ref[KV_RANK:KV_RANK + ONES_ROWS, :] = jnp.where(first, 1.0, 0.0).astype(BF16)


def _latent(p, pt, g, col0, row0):
    m = p.shape[0]
    r = g.shape[0]
    tm = _pick_tile(m, 640, 128)
    return pl.pallas_call(
        _latent_kernel,
        grid=(m // tm,),
        in_specs=[pl.BlockSpec((tm, r), lambda i: (i, col0 // r)),
                  pl.BlockSpec((r, tm), lambda i: (row0 // r, i)),
                  pl.BlockSpec((1, r), lambda i: (0, 0)),
                  pl.BlockSpec((r, 1), lambda i: (0, 0))],
        out_specs=[pl.BlockSpec((tm, r), lambda i: (i, 0)),
                   pl.BlockSpec((r + ONES_ROWS, tm), lambda i: (0, i))],
        out_shape=[jax.ShapeDtypeStruct((m, r), BF16), jax.ShapeDtypeStruct((r + ONES_ROWS, m), BF16)],
        compiler_params=_params(("parallel",)),
        name="latent_norm",
    )(p, pt, g.reshape(1, r), g.reshape(r, 1))


def _dsa_kernel(qt_ref, iqt_ref, iwt_ref, c_ref, ct_ref, ik_ref, wuk_ref, wuvt_ref, o_ref,
                sc_ref, iqp_ref, qlt_ref, acc_ref, m_ref, sa_ref, sb_ref, pa_ref, pb_ref, *, k_sel):
    i = pl.program_id(0)
    tq = Q_BLOCK
    n_heads = ATT_HEADS
    sub_blocks = KEY_CHUNK // tq
    n_chunks = (i * tq + KEY_CHUNK - 1) // KEY_CHUNK

    scale = ATT_HEAD_DIM ** -0.5 * LOG2_E
    for h in range(n_heads):
        qh = qt_ref[h * ATT_HEAD_DIM:(h + 1) * ATT_HEAD_DIM, :].astype(BF16)
        ql = jnp.dot(wuk_ref[h], qh, preferred_element_type=F32) * scale
        qlt_ref[:, h * tq:(h + 1) * tq] = ql.astype(BF16)

    iw = iwt_ref[...] * ((IDX_DIM ** -0.5) * (IDX_HEADS ** -0.5))
    k_loc = lax.broadcasted_iota(jnp.int32, (tq, tq), 0)
    q_loc = lax.broadcasted_iota(jnp.int32, (tq, tq), 1)

    for hp in range(IDX_HEADS // 2):
        pair = jnp.concatenate(
            [iqt_ref[(2 * hp) * IDX_DIM:(2 * hp + 1) * IDX_DIM, :],
             iqt_ref[(2 * hp + 1) * IDX_DIM:(2 * hp + 2) * IDX_DIM, :]], axis=1).astype(BF16)
        iqp_ref[hp, 0:IDX_DIM, :] = pair
        iqp_ref[hp, IDX_DIM:2 * IDX_DIM, :] = jnp.zeros_like(pair)

    def score_chunk(j, carry):
        for u in range(sub_blocks):
            kb = j * sub_blocks + 1 + u
            r0 = pl.multiple_of(kb * tq, tq)
            ikb = ik_ref[pl.ds(r0, tq), :]
            s = jnp.zeros((tq, tq), F32)
            for hp in range(IDX_HEADS // 2):
                x = jnp.dot(ikb, iqp_ref[hp], preferred_element_type=F32)
                s = s + jnp.maximum(x[:, 0:tq], 0.0) * iw[2 * hp:2 * hp + 1, :]
                s = s + jnp.maximum(x[:, tq:2 * tq], 0.0) * iw[2 * hp + 1:2 * hp + 2, :]
            visible = jnp.logical_or(kb < i, jnp.logical_and(kb == i, k_loc <= q_loc))
            sc_ref[pl.ds(r0, tq), :] = jnp.where(visible, s, -jnp.inf)
        return carry

    lax.fori_loop(0, n_chunks, score_chunk, 0)

    groups = KEY_CHUNK // 64

    def chunk_scores(j):
        r0 = pl.multiple_of(FRONT + j * KEY_CHUNK, FRONT)
        return sc_ref[pl.ds(r0, KEY_CHUNK), :].reshape(groups, 8, 8, tq)

    def count_ge(t):
        def body(j, cnt):
            kk = chunk_scores(j)
            for g in range(groups):
                cnt = jnp.where(kk[g] >= t, cnt + 1, cnt)
            return cnt
        cnt = lax.fori_loop(0, n_chunks, body, jnp.zeros((8, 8, tq), jnp.int32))
        return jnp.sum(jnp.sum(cnt, axis=0), axis=0, keepdims=True)

    def min_max(j, carry):
        mn, mx = carry
        kk = chunk_scores(j)
        for g in range(groups):
            mx = jnp.maximum(mx, kk[g])
            mn = jnp.minimum(mn, jnp.where(kk[g] == -jnp.inf, jnp.inf, kk[g]))
        return mn, mx

    mn, mx = lax.fori_loop(0, n_chunks, min_max,
                           (jnp.full((8, 8, tq), jnp.inf, F32), jnp.full((8, 8, tq), -jnp.inf, F32)))
    row_min = jnp.min(jnp.min(mn, axis=0), axis=0, keepdims=True)
    row_max = jnp.max(jnp.max(mx, axis=0), axis=0, keepdims=True)

    lane = lax.broadcasted_iota(jnp.int32, (1, tq), 1)
    n_visible = (i - 1) * tq + lane + 1
    k_row = jnp.minimum(k_sel, n_visible)

    def probe(mid, movable, lo, hi, cnt_lo):
        c = count_ge(mid)
        up = jnp.logical_and(c >= k_row, movable)
        down = jnp.logical_and(jnp.logical_not(up), movable)
        return jnp.where(up, mid, lo), jnp.where(down, mid, hi), jnp.where(up, c, cnt_lo)

    def midpoint(lo, hi, cnt_lo):
        mid = lo + 0.5 * (hi - lo)
        movable = jnp.logical_and(cnt_lo != k_row, jnp.logical_and(mid > lo, mid < hi))
        return mid, movable

    def any_lane(flag):
        return jnp.max(flag.astype(jnp.int32))

    lo, hi, cnt_lo = probe(row_max, n_visible > k_row, row_min, row_max, n_visible)

    def bisect_cond(state):
        return jnp.logical_and(state[0] > 0, state[1] < BISECT_CAP)

    def bisect_body(state):
        _, it, lo, hi, cnt_lo = state
        for _ in range(BISECT_STEPS):
            mid, movable = midpoint(lo, hi, cnt_lo)
            lo, hi, cnt_lo = probe(mid, movable, lo, hi, cnt_lo)
        return any_lane(midpoint(lo, hi, cnt_lo)[1]), it + 1, lo, hi, cnt_lo

    state = (any_lane(midpoint(lo, hi, cnt_lo)[1]), jnp.int32(0), lo, hi, cnt_lo)
    thr = lax.while_loop(bisect_cond, bisect_body, state)[2]

    qlt = qlt_ref[...]
    head_cols = [slice(h * tq, (h + 1) * tq) for h in range(n_heads)]

    def scores(rows, s_ref):
        s_ref[...] = jnp.dot(c_ref[rows, :], qlt, preferred_element_type=F32)

    def softmax_chunk(s_ref, p_ref, bias):
        cmax = []
        for cols in head_cols:
            sm = s_ref[:, cols] + bias
            s_ref[:, cols] = sm
            cmax.append(jnp.max(sm, axis=0, keepdims=True))
        m_old = m_ref[...]
        m_new = jnp.maximum(m_old, jnp.concatenate(cmax, axis=1))
        m_ref[...] = m_new
        for cols in head_cols:
            p_ref[:, cols] = jnp.exp2(s_ref[:, cols] - m_new[:, cols]).astype(BF16)
        return jnp.exp2(m_old - m_new)

    def accumulate(rows, p_ref, alpha):
        acc_ref[...] = alpha * acc_ref[...] + jnp.dot(ct_ref[:, rows], p_ref[...],
                                                      preferred_element_type=F32)

    m_ref[...] = jnp.full(m_ref.shape, NEG_BIG, F32)
    acc_ref[...] = jnp.zeros_like(acc_ref)

    q_row = i * tq + q_loc
    allowed0 = jnp.logical_and(jnp.logical_or(k_loc >= PAD_ROWS, k_loc == q_row), k_loc <= q_row)
    s0_ref, p0_ref = sa_ref.at[0:FRONT, :], pa_ref.at[0:FRONT, :]
    scores(slice(0, FRONT), s0_ref)
    alpha0 = softmax_chunk(s0_ref, p0_ref, jnp.where(allowed0, 0.0, NEG_BIG))
    accumulate(slice(0, FRONT), p0_ref, alpha0)

    def chunk_rows(j):
        jc = jnp.clip(j, 0, n_chunks - 1)
        return pl.ds(pl.multiple_of(FRONT + jc * KEY_CHUNK, FRONT), KEY_CHUNK)

    def chunk_bias(j):
        ok = jnp.logical_and(sc_ref[chunk_rows(j), :] >= thr, j < n_chunks)
        return jnp.where(ok, 0.0, NEG_BIG)

    @pl.when(i > 0)
    def _():
        pb_ref[...] = jnp.zeros_like(pb_ref)
        scores(chunk_rows(0), sa_ref)

        def chunk_pair(t, alpha_b):
            j = 2 * t
            scores(chunk_rows(j + 1), sb_ref)
            alpha_a = softmax_chunk(sa_ref, pa_ref, chunk_bias(j))
            accumulate(chunk_rows(j - 1), pb_ref, alpha_b)
            scores(chunk_rows(j + 2), sa_ref)
            alpha_b = softmax_chunk(sb_ref, pb_ref, chunk_bias(j + 1))
            accumulate(chunk_rows(j), pa_ref, alpha_a)
            return alpha_b

        n_pairs = (n_chunks + 1) // 2
        alpha_b = lax.fori_loop(0, n_pairs, chunk_pair, jnp.ones(m_ref.shape, F32))
        accumulate(chunk_rows(2 * n_pairs - 1), pb_ref, alpha_b)

    o_lat = (acc_ref[0:KV_RANK, :] / acc_ref[KV_RANK:KV_RANK + 1, :]).astype(BF16)
    for h in range(n_heads):
        oh = jnp.dot(wuvt_ref[h], o_lat[:, head_cols[h]], preferred_element_type=F32)
        o_ref[:, h * ATT_HEAD_DIM:(h + 1) * ATT_HEAD_DIM] = oh.T.astype(BF16)


def _dsa(pt, c, ct, ik, wuk, wuvt, k_sel, q_row0, iq_row0, iw_row0):
    m = c.shape[0]
    tq = Q_BLOCK
    d_q = ATT_HEADS * ATT_HEAD_DIM
    d_iq = IDX_HEADS * IDX_DIM
    full2 = lambda shape: pl.BlockSpec(shape, lambda i: (0, 0))
    full3 = lambda shape: pl.BlockSpec(shape, lambda i: (0, 0, 0))
    return pl.pallas_call(
        functools.partial(_dsa_kernel, k_sel=k_sel),
        grid=(m // tq,),
        in_specs=[pl.BlockSpec((d_q, tq), lambda i: (q_row0 // d_q, i)),
                  pl.BlockSpec((d_iq, tq), lambda i: (iq_row0 // d_iq, i)),
                  pl.BlockSpec((IDX_HEADS, tq), lambda i: (iw_row0 // IDX_HEADS, i)),
                  full2(c.shape), full2(ct.shape), full2(ik.shape),
                  full3(wuk.shape), full3(wuvt.shape)],
        out_specs=pl.BlockSpec((tq, d_q), lambda i: (i, 0)),
        out_shape=jax.ShapeDtypeStruct((m, d_q), BF16),
        scratch_shapes=[pltpu.VMEM((m, tq), F32),
                        pltpu.VMEM((IDX_HEADS // 2, 2 * IDX_DIM, 2 * tq), BF16),
                        pltpu.VMEM((KV_RANK, ATT_HEADS * tq), BF16),
                        pltpu.VMEM((KV_RANK + ONES_ROWS, ATT_HEADS * tq), F32),
                        pltpu.VMEM((1, ATT_HEADS * tq), F32),
                        pltpu.VMEM((KEY_CHUNK, ATT_HEADS * tq), F32),
                        pltpu.VMEM((KEY_CHUNK, ATT_HEADS * tq), F32),
                        pltpu.VMEM((KEY_CHUNK, ATT_HEADS * tq), BF16),
                        pltpu.VMEM((KEY_CHUNK, ATT_HEADS * tq), BF16)],
        compiler_params=_params(("arbitrary",)),
        name="dsa",
    )(pt, pt, pt, c, ct, ik, wuk, wuvt)


def _pad_cols(w, n):
    return jnp.pad(w, ((0, 0), (0, n - w.shape[1])))


def kernel(x, meta_tokens, ln_mix_pre, ln_mix_post, ln_ffn_pre, ln_ffn_post, ffn_w1, ffn_w3, ffn_w2,
           ab_w_in, ab_w_out, hgrn_lb_logits, hgrn_out_norm, sconv_w,
           cd_w_in, cd_w_out, rg_conv_w, rg_conv_b, rg_w_a, rg_b_a, rg_w_i, rg_b_i, rg_lambda,
           mla_kv_norm, mla_w_uk, mla_w_uv):
    assert x.shape[0] == 1
    seq, d = x.shape[1], x.shape[2]
    assert seq % KEY_CHUNK == 0
    d_a = hgrn_lb_logits.shape[1]
    d_b = sconv_w.shape[2]
    d_c = rg_lambda.shape[1]
    d_d = ATT_HEADS * ATT_HEAD_DIM
    d_iq = IDX_HEADS * IDX_DIM
    k_sel = min(TOPK_MAX, seq // 4)

    h = jnp.concatenate([jnp.zeros((PAD_ROWS, d), F32), meta_tokens.astype(F32), x[0]], axis=0)

    p0 = _norm_mm(h, ln_mix_pre[0], ab_w_in[0].astype(BF16), tn=1024)
    og = _hgrn(p0, hgrn_lb_logits, hgrn_out_norm[0], lb_row=0)
    yb = _sconv(p0, sconv_w[0], col0=4 * d_a, width=d_b)
    w_out = ab_w_out[0].astype(BF16)
    h = _out_proj(og, yb, w_out[:d_a], w_out[d_a:], h, ln_mix_post[0])
    h = _ffn(h, ln_ffn_pre[0], ffn_w1[0].astype(BF16), ffn_w3[0].astype(BF16),
             ffn_w2[0].astype(BF16), ln_ffn_post[0])

    w_in = cd_w_in[0]
    o_rx, o_ry, o_q, o_c = 0, d_c, 2 * d_c, 2 * d_c + d_d
    o_iq = o_c + KV_RANK
    o_ik = o_iq + d_iq
    o_iw = o_ik + IDX_DIM
    w_rows = jnp.concatenate([w_in[:, o_rx:o_q], w_in[:, o_c:o_iq], w_in[:, o_ik:o_iw]], axis=1)
    w_rows = _pad_cols(w_rows, -(-w_rows.shape[1] // 512) * 512).astype(BF16)
    w_cols = jnp.concatenate([w_in[:, o_q:o_c], w_in[:, o_iq:o_ik], w_in[:, o_c:o_iq], w_in[:, o_iw:]], axis=1)
    w_cols = _pad_cols(w_cols, -(-w_cols.shape[1] // 512) * 512).astype(BF16).T
    p1 = _norm_mm(h, ln_mix_pre[1], w_rows, tn=w_rows.shape[1] // 2)
    p1t = _norm_mm_t(h, ln_mix_pre[1], w_cols, tc=w_cols.shape[0] // 2)

    hc = _rglru(p1, rg_conv_w[0], rg_conv_b[0], rg_w_a[0].astype(BF16), rg_b_a[0],
                rg_w_i[0].astype(BF16), rg_b_i[0], rg_lambda[0], width=d_c)
    c, ct = _latent(p1, p1t, mla_kv_norm[0], col0=2 * d_c, row0=d_d + d_iq)
    ik = p1[:, 2 * d_c + KV_RANK:2 * d_c + KV_RANK + 2 * IDX_DIM].astype(BF16)
    wuk = jnp.transpose(mla_w_uk[0], (1, 0, 2)).astype(BF16)
    wuvt = jnp.transpose(mla_w_uv[0], (1, 2, 0)).astype(BF16)
    att = _dsa(p1t, c, ct, ik, wuk, wuvt, k_sel,
               q_row0=0, iq_row0=d_d, iw_row0=d_d + d_iq + KV_RANK)
    w_out = cd_w_out[0].astype(BF16)
    h = _out_proj(hc, att, w_out[:d_c], w_out[d_c:], h, ln_mix_post[1])
    h = _ffn(h, ln_ffn_pre[1], ffn_w1[1].astype(BF16), ffn_w3[1].astype(BF16),
             ffn_w2[1].astype(BF16), ln_ffn_post[1])
    return h[FRONT:][None]
```

```python
import functools

import jax
import jax.numpy as jnp
from jax import lax
from jax.experimental import pallas as pl
from jax.experimental.pallas import tpu as pltpu

F32 = jnp.float32
BF16 = jnp.bfloat16

EPS = 1e-6
N_META = 16
FRONT = 128
PAD_ROWS = FRONT - N_META
HGRN_HEAD = 128
HGRN_CHUNK = 128
RG_BLOCK = 128
RG_C = 8.0
ATT_HEADS = 8
ATT_HEAD_DIM = 128
KV_RANK = 256
IDX_HEADS = 16
IDX_DIM = 64
TOPK_MAX = 256
Q_BLOCK = 128
KEY_CHUNK = 512
ONES_ROWS = 16
LOG2_E = 1.4426950408889634
BISECT_STEPS = 4
BISECT_CAP = 64
NEG_BIG = -1e30
VMEM_LIMIT = 56 * 1024 * 1024


def _pick_tile(n, target, mult):
    best = None
    for t in range(mult, min(n, target) + 1, mult):
        if n % t == 0:
            best = t
    assert best is not None, (n, target, mult)
    return best


def _params(sem):
    return pltpu.CompilerParams(dimension_semantics=sem, vmem_limit_bytes=VMEM_LIMIT)


def _rms(x, g):
    return x * lax.rsqrt(jnp.mean(x * x, axis=-1, keepdims=True) + EPS) * g


def _norm_mm_kernel(x_ref, g_ref, w_ref, o_ref, xn_ref):
    @pl.when(pl.program_id(1) == 0)
    def _():
        xn_ref[...] = _rms(x_ref[...], g_ref[...]).astype(BF16)

    o_ref[...] = jnp.dot(xn_ref[...], w_ref[...], preferred_element_type=F32)


def _norm_mm(x, g, w, tn=512):
    m, d = x.shape
    n = w.shape[1]
    tm = _pick_tile(m, 640, 128)
    return pl.pallas_call(
        _norm_mm_kernel,
        grid=(m // tm, n // tn),
        in_specs=[pl.BlockSpec((tm, d), lambda i, j: (i, 0)),
                  pl.BlockSpec((1, d), lambda i, j: (0, 0)),
                  pl.BlockSpec((d, tn), lambda i, j: (0, j))],
        out_specs=pl.BlockSpec((tm, tn), lambda i, j: (i, j)),
        out_shape=jax.ShapeDtypeStruct((m, n), F32),
        scratch_shapes=[pltpu.VMEM((tm, d), BF16)],
        compiler_params=_params(("parallel", "arbitrary")),
        name="norm_proj",
    )(x, g.reshape(1, d), w)


def _norm_mm_t_kernel(x_ref, g_ref, wt_ref, o_ref, xn_ref):
    @pl.when(pl.program_id(1) == 0)
    def _():
        xn_ref[...] = _rms(x_ref[...], g_ref[...]).astype(BF16)

    o_ref[...] = lax.dot_general(wt_ref[...], xn_ref[...], (((1,), (1,)), ((), ())),
                                 preferred_element_type=F32)


def _norm_mm_t(x, g, wt, tc=512):
    m, d = x.shape
    n = wt.shape[0]
    tm = _pick_tile(m, 640, 128)
    return pl.pallas_call(
        _norm_mm_t_kernel,
        grid=(m // tm, n // tc),
        in_specs=[pl.BlockSpec((tm, d), lambda i, j: (i, 0)),
                  pl.BlockSpec((1, d), lambda i, j: (0, 0)),
                  pl.BlockSpec((tc, d), lambda i, j: (j, 0))],
        out_specs=pl.BlockSpec((tc, tm), lambda i, j: (j, i)),
        out_shape=jax.ShapeDtypeStruct((n, m), F32),
        scratch_shapes=[pltpu.VMEM((tm, d), BF16)],
        compiler_params=_params(("parallel", "arbitrary")),
        name="norm_proj_t",
    )(x, g.reshape(1, d), wt)


def _out_proj_kernel(a_ref, b_ref, wa_ref, wb_ref, h_ref, g_ref, o_ref):
    y = jnp.dot(a_ref[...], wa_ref[...], preferred_element_type=F32)
    y = y + jnp.dot(b_ref[...], wb_ref[...], preferred_element_type=F32)
    o_ref[...] = h_ref[...] + _rms(y, g_ref[...])


def _out_proj(a, b, wa, wb, h, g):
    m, d = h.shape
    ka, kb = a.shape[1], b.shape[1]
    tm = _pick_tile(m, 640, 128)
    return pl.pallas_call(
        _out_proj_kernel,
        grid=(m // tm,),
        in_specs=[pl.BlockSpec((tm, ka), lambda i: (i, 0)),
                  pl.BlockSpec((tm, kb), lambda i: (i, 0)),
                  pl.BlockSpec((ka, d), lambda i: (0, 0)),
                  pl.BlockSpec((kb, d), lambda i: (0, 0)),
                  pl.BlockSpec((tm, d), lambda i: (i, 0)),
                  pl.BlockSpec((1, d), lambda i: (0, 0))],
        out_specs=pl.BlockSpec((tm, d), lambda i: (i, 0)),
        out_shape=jax.ShapeDtypeStruct((m, d), F32),
        compiler_params=_params(("parallel",)),
        name="out_proj",
    )(a, b, wa, wb, h, g.reshape(1, d))


def _ffn_kernel(h_ref, gpre_ref, w1_ref, w3_ref, w2_ref, gpost_ref, o_ref, xn_ref, acc_ref):
    j = pl.program_id(1)

    @pl.when(j == 0)
    def _():
        xn_ref[...] = _rms(h_ref[...], gpre_ref[...]).astype(BF16)
        acc_ref[...] = jnp.zeros_like(acc_ref)

    xn = xn_ref[...]
    a = jnp.dot(xn, w1_ref[...], preferred_element_type=F32)
    b = jnp.dot(xn, w3_ref[...], preferred_element_type=F32)
    u = (a * jax.nn.sigmoid(a) * b).astype(BF16)
    acc_ref[...] += jnp.dot(u, w2_ref[...], preferred_element_type=F32)

    @pl.when(j == pl.num_programs(1) - 1)
    def _():
        o_ref[...] = h_ref[...] + _rms(acc_ref[...], gpost_ref[...])


def _ffn(h, gpre, w1, w3, w2, gpost, layer, tf=512):
    m, d = h.shape
    f = w1.shape[2]
    tm = _pick_tile(m, 640, 128)
    return pl.pallas_call(
        _ffn_kernel,
        grid=(m // tm, f // tf),
        in_specs=[pl.BlockSpec((tm, d), lambda i, j: (i, 0)),
                  pl.BlockSpec((1, d), lambda i, j: (0, 0)),
                  pl.BlockSpec((None, d, tf), lambda i, j: (layer, 0, j)),
                  pl.BlockSpec((None, d, tf), lambda i, j: (layer, 0, j)),
                  pl.BlockSpec((None, tf, d), lambda i, j: (layer, j, 0)),
                  pl.BlockSpec((1, d), lambda i, j: (0, 0))],
        out_specs=pl.BlockSpec((tm, d), lambda i, j: (i, 0)),
        out_shape=jax.ShapeDtypeStruct((m, d), F32),
        scratch_shapes=[pltpu.VMEM((tm, d), BF16), pltpu.VMEM((tm, d), F32)],
        compiler_params=_params(("parallel", "arbitrary")),
        name="ffn",
    )(h, gpre.reshape(1, d), w1, w3, w2, gpost.reshape(1, d))


def _cumsum_rows(tri, x):
    hi = x.astype(BF16)
    rest = x - hi.astype(F32)
    mid = rest.astype(BF16)
    lo = (rest - mid.astype(F32)).astype(BF16)
    return (jnp.dot(tri, hi, preferred_element_type=F32) + jnp.dot(tri, mid, preferred_element_type=F32)
            + jnp.dot(tri, lo, preferred_element_type=F32))


def _edge_rows(b, half):
    rows = b.shape[0]
    if half >= 8:
        parts = [jnp.broadcast_to(b[e:e + 1, :], (2 * half, b.shape[1]))
                 for e in range(half - 1, rows, 2 * half)]
        return parts[0] if len(parts) == 1 else jnp.concatenate(parts, axis=0)
    b3 = b.reshape(rows // 8, 8, b.shape[1])
    sub = lax.broadcasted_iota(jnp.int32, b3.shape, 1)
    pick = lambda r: jnp.broadcast_to(b3[:, r:r + 1, :], b3.shape)
    edge = pick(half - 1)
    for start in range(2 * half, 8, 2 * half):
        edge = jnp.where(sub >= start, pick(start + half - 1), edge)
    return edge.reshape(b.shape)


def _hgrn_kernel(q_ref, f_ref, v_ref, gate_ref, lbl_ref, gn_ref, o_ref, st_ref, *, n_heads, lb_row):
    c_rows = HGRN_CHUNK

    @pl.when(pl.program_id(1) == 0)
    def _():
        st_ref[...] = jnp.zeros_like(st_ref)

    logits = lbl_ref[...]
    ex = jnp.exp(logits - jnp.max(logits, axis=0, keepdims=True))
    lb_all = jnp.sum(ex[0:lb_row + 1, :], axis=0, keepdims=True) / jnp.sum(ex, axis=0, keepdims=True)

    r_i = lax.broadcasted_iota(jnp.int32, (c_rows, c_rows), 0)
    c_i = lax.broadcasted_iota(jnp.int32, (c_rows, c_rows), 1)
    tri = (r_i >= c_i).astype(BF16)
    levels = []
    half = c_rows // 2
    while half >= 1:
        shift = half.bit_length()
        upper = (lax.shift_right_logical(r_i, shift - 1) & 1) == 1
        same = lax.shift_right_logical(r_i, shift) == lax.shift_right_logical(c_i, shift)
        levels.append((half, upper, jnp.where(upper, 1.0, -1.0), same))
        half //= 2

    for hh in range(n_heads):
        cols = slice(hh * HGRN_HEAD, (hh + 1) * HGRN_HEAD)
        lb = lb_all[:, cols]
        q = q_ref[:, cols]
        v = v_ref[:, cols]
        f = lb + (1.0 - lb) * jax.nn.sigmoid(f_ref[:, cols])
        k = 1.0 - f
        b = _cumsum_rows(tri, jnp.log2(f))
        b_last = b[c_rows - 1:c_rows, :]

        st = st_ref[hh]
        inter = lax.dot_general((q * jnp.exp2(b)).astype(BF16), st.astype(BF16),
                                (((1,), (1,)), ((), ())), preferred_element_type=F32)
        kt = (k * jnp.exp2(b_last - b)).astype(BF16)
        st_ref[hh] = st * jnp.exp2(b_last) + lax.dot_general(
            v.astype(BF16), kt, (((0,), (0,)), ((), ())), preferred_element_type=F32)

        att = jnp.zeros((c_rows, c_rows), F32)
        for half, upper, sign, same in levels:
            decay = jnp.exp2((b - _edge_rows(b, half)) * sign)
            scaled = jnp.where(upper, q, k) * decay
            qt = jnp.where(upper, scaled, 0.0).astype(BF16)
            kl = jnp.where(upper, 0.0, scaled).astype(BF16)
            pair = lax.dot_general(qt, kl, (((1,), (1,)), ((), ())), preferred_element_type=F32)
            att = att + jnp.where(same, pair, 0.0)
        o = inter + jnp.dot(att.astype(BF16), v.astype(BF16), preferred_element_type=F32)
        o = o + jnp.sum(q * k, axis=-1, keepdims=True) * v

        gate = gate_ref[:, cols]
        o_ref[:, cols] = (_rms(o, gn_ref[...]) * (gate * jax.nn.sigmoid(gate))).astype(BF16)


def _hgrn(p, lb_logits, gn, lb_row):
    m = p.shape[0]
    d_a = lb_logits.shape[1]
    hp = 4
    width = hp * HGRN_HEAD
    groups = d_a // width
    tb = HGRN_CHUNK
    n_l = lb_logits.shape[0]
    col = lambda off: (lambda g, t: (t, off + g))
    kern = functools.partial(_hgrn_kernel, n_heads=hp, lb_row=lb_row)
    return pl.pallas_call(
        kern,
        grid=(groups, m // tb),
        in_specs=[pl.BlockSpec((tb, width), col(0)),
                  pl.BlockSpec((tb, width), col(groups)),
                  pl.BlockSpec((tb, width), col(2 * groups)),
                  pl.BlockSpec((tb, width), col(3 * groups)),
                  pl.BlockSpec((n_l, width), lambda g, t: (0, g)),
                  pl.BlockSpec((1, HGRN_HEAD), lambda g, t: (0, 0))],
        out_specs=pl.BlockSpec((tb, width), lambda g, t: (t, g)),
        out_shape=jax.ShapeDtypeStruct((m, d_a), BF16),
        scratch_shapes=[pltpu.VMEM((hp, HGRN_HEAD, HGRN_HEAD), F32)],
        compiler_params=_params(("parallel", "arbitrary")),
        name="hgrn2",
    )(p, p, p, p, lb_logits, gn.reshape(1, HGRN_HEAD))


def _sconv_kernel(sx_ref, sb_ref, sc_ref, sxp_ref, scp_ref, w_ref, o_ref, ext_ref, *, taps):
    tm = sx_ref.shape[0]
    prev = sxp_ref[...] * scp_ref[...]
    ext_ref[0:8, :] = jnp.where(pl.program_id(0) > 0, prev, 0.0)
    ext_ref[8:8 + tm, :] = sx_ref[...] * sc_ref[...]
    y = jnp.zeros(sx_ref.shape, F32)
    for j in range(taps):
        s = 8 - (taps - 1) + j
        y = y + w_ref[j:j + 1, :] * ext_ref[s:s + tm, :]
    o_ref[...] = (sb_ref[...] * y).astype(BF16)


def _sconv(p, w, col0, width):
    m = p.shape[0]
    taps = w.shape[0]
    tm = _pick_tile(m, 640, 128)
    cb = col0 // width
    prev = lambda off: (lambda i: (jnp.maximum(i * (tm // 8) - 1, 0), off))
    return pl.pallas_call(
        functools.partial(_sconv_kernel, taps=taps),
        grid=(m // tm,),
        in_specs=[pl.BlockSpec((tm, width), lambda i: (i, cb)),
                  pl.BlockSpec((tm, width), lambda i: (i, cb + 1)),
                  pl.BlockSpec((tm, width), lambda i: (i, cb + 2)),
                  pl.BlockSpec((8, width), prev(cb)),
                  pl.BlockSpec((8, width), prev(cb + 2)),
                  pl.BlockSpec((taps, width), lambda i: (0, 0))],
        out_specs=pl.BlockSpec((tm, width), lambda i: (i, 0)),
        out_shape=jax.ShapeDtypeStruct((m, width), BF16),
        scratch_shapes=[pltpu.VMEM((tm + 8, width), F32)],
        compiler_params=_params(("parallel",)),
        name="sconv",
    )(p, p, p, p, p, w)


def _rglru_kernel(rx_ref, ry_ref, rxp_ref, cw_ref, cb_ref, wa_ref, ba_ref, wi_ref, bi_ref, lam_ref,
                  o_ref, ext_ref, a_ref, x_ref, hs_ref, h_ref, *, taps):
    tm, width = rx_ref.shape
    i = pl.program_id(0)

    @pl.when(i == 0)
    def _():
        h_ref[...] = jnp.zeros_like(h_ref)

    ext_ref[0:8, :] = jnp.where(i > 0, rxp_ref[...], 0.0)
    ext_ref[8:8 + tm, :] = rx_ref[...]
    u = jnp.zeros((tm, width), F32) + cb_ref[...]
    for j in range(taps):
        s = 8 - (taps - 1) + j
        u = u + cw_ref[j:j + 1, :] * ext_ref[s:s + tm, :]

    u_b = u.astype(BF16)
    r_parts, i_parts = [], []
    for n in range(width // RG_BLOCK):
        blk = slice(n * RG_BLOCK, (n + 1) * RG_BLOCK)
        r_parts.append(jnp.dot(u_b[:, blk], wa_ref[n], preferred_element_type=F32))
        i_parts.append(jnp.dot(u_b[:, blk], wi_ref[n], preferred_element_type=F32))
    r = jax.nn.sigmoid(jnp.concatenate(r_parts, axis=1) + ba_ref[...])
    ig = jax.nn.sigmoid(jnp.concatenate(i_parts, axis=1) + bi_ref[...])

    neg_lam = -lam_ref[...]
    softplus = jnp.maximum(neg_lam, 0.0) + jnp.log1p(jnp.exp(-jnp.abs(neg_lam)))
    log_a = -RG_C * r * softplus
    row = i * tm + lax.broadcasted_iota(jnp.int32, (tm, 1), 0)
    a = jnp.exp(log_a)
    xin = jnp.sqrt(1.0 - a * a) * (ig * u)
    a_ref[...] = a
    x_ref[...] = jnp.where(row >= PAD_ROWS, xin, 0.0)

    def group(gidx, h):
        base = pl.multiple_of(gidx * 8, 8)
        a8 = a_ref[pl.ds(base, 8), :]
        x8 = x_ref[pl.ds(base, 8), :]
        for rr in range(8):
            h = a8[rr:rr + 1, :] * h + x8[rr:rr + 1, :]
            hs_ref[pl.ds(base + rr, 1), :] = h
        return h

    h_ref[...] = lax.fori_loop(0, tm // 8, group, h_ref[...])
    o_ref[...] = (hs_ref[...] * jax.nn.gelu(ry_ref[...])).astype(BF16)


def _rglru(p, cw, cb, wa, ba, wi, bi, lam, width):
    m = p.shape[0]
    taps = cw.shape[0]
    tm = _pick_tile(m, 640, 128)
    nb = width // RG_BLOCK
    row = lambda v: v.reshape(1, width)
    full2 = lambda shape: pl.BlockSpec(shape, lambda i: (0, 0))
    full3 = lambda shape: pl.BlockSpec(shape, lambda i: (0, 0, 0))
    return pl.pallas_call(
        functools.partial(_rglru_kernel, taps=taps),
        grid=(m // tm,),
        in_specs=[pl.BlockSpec((tm, width), lambda i: (i, 0)),
                  pl.BlockSpec((tm, width), lambda i: (i, 1)),
                  pl.BlockSpec((8, width), lambda i: (jnp.maximum(i * (tm // 8) - 1, 0), 0)),
                  full2((taps, width)), full2((1, width)),
                  full3((nb, RG_BLOCK, RG_BLOCK)), full2((1, width)),
                  full3((nb, RG_BLOCK, RG_BLOCK)), full2((1, width)),
                  full2((1, width))],
        out_specs=pl.BlockSpec((tm, width), lambda i: (i, 0)),
        out_shape=jax.ShapeDtypeStruct((m, width), BF16),
        scratch_shapes=[pltpu.VMEM((tm + 8, width), F32),
                        pltpu.VMEM((tm, width), F32),
                        pltpu.VMEM((tm, width), F32),
                        pltpu.VMEM((tm, width), F32),
                        pltpu.VMEM((1, width), F32)],
        compiler_params=_params(("arbitrary",)),
        name="rglru",
    )(p, p, p, cw, row(cb), wa, row(ba), wi, row(bi), row(lam))


def _latent_kernel(c_ref, ct_ref, g_ref, gt_ref, o_ref, ot_ref):
    o_ref[...] = _rms(c_ref[...], g_ref[...]).astype(BF16)
    ct = ct_ref[...]
    ot_ref[0:KV_RANK, :] = (ct * lax.rsqrt(jnp.mean(ct * ct, axis=0, keepdims=True) + EPS)
                            * gt_ref[...]).astype(BF16)
    first = lax.broadcasted_iota(jnp.int32, (ONES_ROWS, ct.shape[1]), 0) == 0
    ot_ref[KV_RANK:KV_RANK + ONES_ROWS, :] = jnp.where(first, 1.0, 0.0).astype(BF16)


def _latent(p, pt, g, col0, row0):
    m = p.shape[0]
    r = g.shape[0]
    tm = _pick_tile(m, 640, 128)
    return pl.pallas_call(
        _latent_kernel,
        grid=(m // tm,),
        in_specs=[pl.BlockSpec((tm, r), lambda i: (i, col0 // r)),
                  pl.BlockSpec((r, tm), lambda i: (row0 // r, i)),
                  pl.BlockSpec((1, r), lambda i: (0, 0)),
                  pl.BlockSpec((r, 1), lambda i: (0, 0))],
        out_specs=[pl.BlockSpec((tm, r), lambda i: (i, 0)),
                   pl.BlockSpec((r + ONES_ROWS, tm), lambda i: (0, i))],
        out_shape=[jax.ShapeDtypeStruct((m, r), BF16), jax.ShapeDtypeStruct((r + ONES_ROWS, m), BF16)],
        compiler_params=_params(("parallel",)),
        name="latent_norm",
    )(p, pt, g.reshape(1, r), g.reshape(r, 1))


def _dsa_kernel(qt_ref, iqt_ref, iwt_ref, c_ref, ct_ref, ik_ref, wuk_ref, wuvt_ref, o_ref,
                sc_ref, iqp_ref, qlt_ref, acc_ref, m_ref, sa_ref, sb_ref, *, k_sel):
    i = pl.program_id(0)
    tq = Q_BLOCK
    n_heads = ATT_HEADS
    sub_blocks = KEY_CHUNK // tq
    n_chunks = (i * tq + KEY_CHUNK - 1) // KEY_CHUNK

    scale = ATT_HEAD_DIM ** -0.5 * LOG2_E
    for h in range(n_heads):
        qh = qt_ref[h * ATT_HEAD_DIM:(h + 1) * ATT_HEAD_DIM, :].astype(BF16)
        ql = jnp.dot(wuk_ref[h], qh, preferred_element_type=F32) * scale
        qlt_ref[:, h * tq:(h + 1) * tq] = ql.astype(BF16)

    iw = iwt_ref[...] * ((IDX_DIM ** -0.5) * (IDX_HEADS ** -0.5))
    k_loc = lax.broadcasted_iota(jnp.int32, (tq, tq), 0)
    q_loc = lax.broadcasted_iota(jnp.int32, (tq, tq), 1)

    for hp in range(IDX_HEADS // 2):
        pair = jnp.concatenate(
            [iqt_ref[(2 * hp) * IDX_DIM:(2 * hp + 1) * IDX_DIM, :],
             iqt_ref[(2 * hp + 1) * IDX_DIM:(2 * hp + 2) * IDX_DIM, :]], axis=1).astype(BF16)
        iqp_ref[hp, 0:IDX_DIM, :] = pair
        iqp_ref[hp, IDX_DIM:2 * IDX_DIM, :] = jnp.zeros_like(pair)

    def score_chunk(j, carry):
        for u in range(sub_blocks):
            kb = j * sub_blocks + 1 + u
            r0 = pl.multiple_of(kb * tq, tq)
            ikb = ik_ref[pl.ds(r0, tq), :]
            s = jnp.zeros((tq, tq), F32)
            for hp in range(IDX_HEADS // 2):
                x = jnp.dot(ikb, iqp_ref[hp], preferred_element_type=F32)
                s = s + jnp.maximum(x[:, 0:tq], 0.0) * iw[2 * hp:2 * hp + 1, :]
                s = s + jnp.maximum(x[:, tq:2 * tq], 0.0) * iw[2 * hp + 1:2 * hp + 2, :]
            visible = jnp.logical_or(kb < i, jnp.logical_and(kb == i, k_loc <= q_loc))
            sc_ref[pl.ds(r0, tq), :] = jnp.where(visible, s, -jnp.inf)
        return carry

    lax.fori_loop(0, n_chunks, score_chunk, 0)

    groups = KEY_CHUNK // 64

    def chunk_scores(j):
        r0 = pl.multiple_of(FRONT + j * KEY_CHUNK, FRONT)
        return sc_ref[pl.ds(r0, KEY_CHUNK), :].reshape(groups, 8, 8, tq)

    def count_ge(t):
        def body(j, cnt):
            kk = chunk_scores(j)
            for g in range(groups):
                cnt = jnp.where(kk[g] >= t, cnt + 1, cnt)
            return cnt
        cnt = lax.fori_loop(0, n_chunks, body, jnp.zeros((8, 8, tq), jnp.int32))
        return jnp.sum(jnp.sum(cnt, axis=0), axis=0, keepdims=True)

    def min_max(j, carry):
        mn, mx = carry
        kk = chunk_scores(j)
        for g in range(groups):
            mx = jnp.maximum(mx, kk[g])
            mn = jnp.minimum(mn, jnp.where(kk[g] == -jnp.inf, jnp.inf, kk[g]))
        return mn, mx

    mn, mx = lax.fori_loop(0, n_chunks, min_max,
                           (jnp.full((8, 8, tq), jnp.inf, F32), jnp.full((8, 8, tq), -jnp.inf, F32)))
    row_min = jnp.min(jnp.min(mn, axis=0), axis=0, keepdims=True)
    row_max = jnp.max(jnp.max(mx, axis=0), axis=0, keepdims=True)

    lane = lax.broadcasted_iota(jnp.int32, (1, tq), 1)
    n_visible = (i - 1) * tq + lane + 1
    k_row = jnp.minimum(k_sel, n_visible)

    def probe(mid, movable, lo, hi, cnt_lo):
        c = count_ge(mid)
        up = jnp.logical_and(c >= k_row, movable)
        down = jnp.logical_and(jnp.logical_not(up), movable)
        return jnp.where(up, mid, lo), jnp.where(down, mid, hi), jnp.where(up, c, cnt_lo)

    def midpoint(lo, hi, cnt_lo):
        mid = lo + 0.5 * (hi - lo)
        movable = jnp.logical_and(cnt_lo != k_row, jnp.logical_and(mid > lo, mid < hi))
        return mid, movable

    def any_lane(flag):
        return jnp.max(flag.astype(jnp.int32))

    lo, hi, cnt_lo = probe(row_max, n_visible > k_row, row_min, row_max, n_visible)

    def bisect_cond(state):
        return jnp.logical_and(state[0] > 0, state[1] < BISECT_CAP)

    def bisect_body(state):
        _, it, lo, hi, cnt_lo = state
        for _ in range(BISECT_STEPS):
            mid, movable = midpoint(lo, hi, cnt_lo)
            lo, hi, cnt_lo = probe(mid, movable, lo, hi, cnt_lo)
        return any_lane(midpoint(lo, hi, cnt_lo)[1]), it + 1, lo, hi, cnt_lo

    state = (any_lane(midpoint(lo, hi, cnt_lo)[1]), jnp.int32(0), lo, hi, cnt_lo)
    thr = lax.while_loop(bisect_cond, bisect_body, state)[2]

    qlt = qlt_ref[...]
    head_cols = [slice(h * tq, (h + 1) * tq) for h in range(n_heads)]

    def masked_scores(rows, bias, s_ref):
        bias2 = jnp.concatenate([bias, bias], axis=1)
        cmax = []
        for hp in range(n_heads // 2):
            cols = slice(2 * hp * tq, (2 * hp + 2) * tq)
            sm = jnp.dot(c_ref[rows, :], qlt[:, cols], preferred_element_type=F32) + bias2
            s_ref[:, cols] = sm
            cmax.append(jnp.max(sm, axis=0, keepdims=True))
        return jnp.concatenate(cmax, axis=1)

    def accumulate(rows, s_ref, cmax):
        m_old = m_ref[...]
        m_new = jnp.maximum(m_old, cmax)
        m_ref[...] = m_new
        p = jnp.exp2(s_ref[...] - m_new).astype(BF16)
        acc_ref[...] = jnp.exp2(m_old - m_new) * acc_ref[...] + jnp.dot(
            ct_ref[:, rows], p, preferred_element_type=F32)

    m_ref[...] = jnp.full(m_ref.shape, NEG_BIG, F32)
    acc_ref[...] = jnp.zeros_like(acc_ref)

    q_row = i * tq + q_loc
    allowed0 = jnp.logical_and(jnp.logical_or(k_loc >= PAD_ROWS, k_loc == q_row), k_loc <= q_row)
    s0_ref = sa_ref.at[0:FRONT, :]
    cmax0 = masked_scores(slice(0, FRONT), jnp.where(allowed0, 0.0, NEG_BIG), s0_ref)
    accumulate(slice(0, FRONT), s0_ref, cmax0)

    def chunk_rows(j):
        jc = jnp.clip(j, 0, n_chunks - 1)
        return pl.ds(pl.multiple_of(FRONT + jc * KEY_CHUNK, FRONT), KEY_CHUNK)

    def chunk_scores_masked(j, s_ref):
        rows = chunk_rows(j)
        ok = jnp.logical_and(sc_ref[rows, :] >= thr, j < n_chunks)
        return masked_scores(rows, jnp.where(ok, 0.0, NEG_BIG), s_ref)

    @pl.when(i > 0)
    def _():
        def chunk_pair(t, cmax_a):
            j = 2 * t
            cmax_b = chunk_scores_masked(j + 1, sb_ref)
            accumulate(chunk_rows(j), sa_ref, cmax_a)
            cmax_a = chunk_scores_masked(j + 2, sa_ref)
            accumulate(chunk_rows(j + 1), sb_ref, cmax_b)
            return cmax_a

        lax.fori_loop(0, (n_chunks + 1) // 2, chunk_pair, chunk_scores_masked(0, sa_ref))

    o_lat = (acc_ref[0:KV_RANK, :] / acc_ref[KV_RANK:KV_RANK + 1, :]).astype(BF16)
    for h in range(n_heads):
        oh = jnp.dot(wuvt_ref[h], o_lat[:, head_cols[h]], preferred_element_type=F32)
        o_ref[:, h * ATT_HEAD_DIM:(h + 1) * ATT_HEAD_DIM] = oh.T.astype(BF16)


def _dsa(pt, c, ct, ik, wuk, wuvt, k_sel, q_row0, iq_row0, iw_row0):
    m = c.shape[0]
    tq = Q_BLOCK
    d_q = ATT_HEADS * ATT_HEAD_DIM
    d_iq = IDX_HEADS * IDX_DIM
    full2 = lambda shape: pl.BlockSpec(shape, lambda i: (0, 0))
    full3 = lambda shape: pl.BlockSpec(shape, lambda i: (0, 0, 0))
    return pl.pallas_call(
        functools.partial(_dsa_kernel, k_sel=k_sel),
        grid=(m // tq,),
        in_specs=[pl.BlockSpec((d_q, tq), lambda i: (q_row0 // d_q, i)),
                  pl.BlockSpec((d_iq, tq), lambda i: (iq_row0 // d_iq, i)),
                  pl.BlockSpec((IDX_HEADS, tq), lambda i: (iw_row0 // IDX_HEADS, i)),
                  full2(c.shape), full2(ct.shape), full2(ik.shape),
                  full3(wuk.shape), full3(wuvt.shape)],
        out_specs=pl.BlockSpec((tq, d_q), lambda i: (i, 0)),
        out_shape=jax.ShapeDtypeStruct((m, d_q), BF16),
        scratch_shapes=[pltpu.VMEM((m, tq), F32),
                        pltpu.VMEM((IDX_HEADS // 2, 2 * IDX_DIM, 2 * tq), BF16),
                        pltpu.VMEM((KV_RANK, ATT_HEADS * tq), BF16),
                        pltpu.VMEM((KV_RANK + ONES_ROWS, ATT_HEADS * tq), F32),
                        pltpu.VMEM((1, ATT_HEADS * tq), F32),
                        pltpu.VMEM((KEY_CHUNK, ATT_HEADS * tq), F32),
                        pltpu.VMEM((KEY_CHUNK, ATT_HEADS * tq), F32)],
        compiler_params=_params(("arbitrary",)),
        name="dsa",
    )(pt, pt, pt, c, ct, ik, wuk, wuvt)


def _pad_cols(w, n):
    return jnp.pad(w, ((0, 0), (0, n - w.shape[1])))


def kernel(x, meta_tokens, ln_mix_pre, ln_mix_post, ln_ffn_pre, ln_ffn_post, ffn_w1, ffn_w3, ffn_w2,
           ab_w_in, ab_w_out, hgrn_lb_logits, hgrn_out_norm, sconv_w,
           cd_w_in, cd_w_out, rg_conv_w, rg_conv_b, rg_w_a, rg_b_a, rg_w_i, rg_b_i, rg_lambda,
           mla_kv_norm, mla_w_uk, mla_w_uv):
    assert x.shape[0] == 1
    seq, d = x.shape[1], x.shape[2]
    assert seq % KEY_CHUNK == 0
    d_a = hgrn_lb_logits.shape[1]
    d_b = sconv_w.shape[2]
    d_c = rg_lambda.shape[1]
    d_d = ATT_HEADS * ATT_HEAD_DIM
    d_iq = IDX_HEADS * IDX_DIM
    k_sel = min(TOPK_MAX, seq // 4)

    h = jnp.concatenate([jnp.zeros((PAD_ROWS, d), F32), meta_tokens.astype(F32), x[0]], axis=0)
    w1, w3, w2 = ffn_w1.astype(BF16), ffn_w3.astype(BF16), ffn_w2.astype(BF16)

    p0 = _norm_mm(h, ln_mix_pre[0], ab_w_in[0].astype(BF16), tn=1024)
    og = _hgrn(p0, hgrn_lb_logits, hgrn_out_norm[0], lb_row=0)
    yb = _sconv(p0, sconv_w[0], col0=4 * d_a, width=d_b)
    w_out = ab_w_out[0].astype(BF16)
    h = _out_proj(og, yb, w_out[:d_a], w_out[d_a:], h, ln_mix_post[0])
    h = _ffn(h, ln_ffn_pre[0], w1, w3, w2, ln_ffn_post[0], layer=0)

    w_in = cd_w_in[0]
    o_rx, o_ry, o_q, o_c = 0, d_c, 2 * d_c, 2 * d_c + d_d
    o_iq = o_c + KV_RANK
    o_ik = o_iq + d_iq
    o_iw = o_ik + IDX_DIM
    w_rows = jnp.concatenate([w_in[:, o_rx:o_q], w_in[:, o_c:o_iq], w_in[:, o_ik:o_iw]], axis=1)
    w_rows = _pad_cols(w_rows, -(-w_rows.shape[1] // 512) * 512).astype(BF16)
    w_cols = jnp.concatenate([w_in[:, o_q:o_c], w_in[:, o_iq:o_ik], w_in[:, o_c:o_iq], w_in[:, o_iw:]], axis=1)
    w_cols = _pad_cols(w_cols, -(-w_cols.shape[1] // 512) * 512).astype(BF16).T
    p1 = _norm_mm(h, ln_mix_pre[1], w_rows, tn=w_rows.shape[1] // 2)
    p1t = _norm_mm_t(h, ln_mix_pre[1], w_cols, tc=w_cols.shape[0] // 2)

    hc = _rglru(p1, rg_conv_w[0], rg_conv_b[0], rg_w_a[0].astype(BF16), rg_b_a[0],
                rg_w_i[0].astype(BF16), rg_b_i[0], rg_lambda[0], width=d_c)
    c, ct = _latent(p1, p1t, mla_kv_norm[0], col0=2 * d_c, row0=d_d + d_iq)
    ik = p1[:, 2 * d_c + KV_RANK:2 * d_c + KV_RANK + 2 * IDX_DIM].astype(BF16)
    wuk = jnp.transpose(mla_w_uk[0], (1, 0, 2)).astype(BF16)
    wuvt = jnp.transpose(mla_w_uv[0], (1, 2, 0)).astype(BF16)
    att = _dsa(p1t, c, ct, ik, wuk, wuvt, k_sel,
               q_row0=0, iq_row0=d_d, iw_row0=d_d + d_iq + KV_RANK)
    w_out = cd_w_out[0].astype(BF16)
    h = _out_proj(hc, att, w_out[:d_c], w_out[d_c:], h, ln_mix_post[1])
    h = _ffn(h, ln_ffn_pre[1], w1, w3, w2, ln_ffn_post[1], layer=1)
    return h[FRONT:][None]
```

```python
import functools

import jax
import jax.numpy as jnp
from jax import lax
from jax.experimental import pallas as pl
from jax.experimental.pallas import tpu as pltpu

F32 = jnp.float32
BF16 = jnp.bfloat16

EPS = 1e-6
N_META = 16
FRONT = 128
PAD_ROWS = FRONT - N_META
HGRN_HEAD = 128
HGRN_CHUNK = 128
RG_BLOCK = 128
RG_C = 8.0
ATT_HEADS = 8
ATT_HEAD_DIM = 128
KV_RANK = 256
IDX_HEADS = 16
IDX_DIM = 64
TOPK_MAX = 256
Q_BLOCK = 128
KEY_CHUNK = 512
ONES_ROWS = 16
LOG2_E = 1.4426950408889634
BISECT_STEPS = 4
BISECT_CAP = 64
NEG_BIG = -1e30
VMEM_LIMIT = 56 * 1024 * 1024


def _pick_tile(n, target, mult):
    best = None
    for t in range(mult, min(n, target) + 1, mult):
        if n % t == 0:
            best = t
    assert best is not None, (n, target, mult)
    return best


def _params(sem):
    return pltpu.CompilerParams(dimension_semantics=sem, vmem_limit_bytes=VMEM_LIMIT)


def _rms(x, g):
    return x * lax.rsqrt(jnp.mean(x * x, axis=-1, keepdims=True) + EPS) * g


def _token_block(i, x_ref, front_ref):
    xb = x_ref[...]
    first = jnp.concatenate([front_ref[...], xb[0:xb.shape[0] - FRONT, :]], axis=0)
    return jnp.where(i == 0, first, xb)


def _token_specs(tm, d):
    x_map = lambda *a: (pl.multiple_of(jnp.maximum(a[0] * tm - FRONT, 0), FRONT), 0)
    return [pl.BlockSpec((pl.Element(tm), pl.Element(d)), x_map),
            pl.BlockSpec((FRONT, d), lambda *a: (0, 0))]


def _norm_mm_kernel(x_ref, g_ref, w_ref, o_ref, xn_ref):
    @pl.when(pl.program_id(1) == 0)
    def _():
        xn_ref[...] = _rms(x_ref[...], g_ref[...]).astype(BF16)

    o_ref[...] = jnp.dot(xn_ref[...], w_ref[...], preferred_element_type=F32)


def _norm_mm_tokens_kernel(x_ref, front_ref, g_ref, w_ref, o_ref, xn_ref):
    @pl.when(pl.program_id(1) == 0)
    def _():
        xn_ref[...] = _rms(_token_block(pl.program_id(0), x_ref, front_ref), g_ref[...]).astype(BF16)

    o_ref[...] = jnp.dot(xn_ref[...], w_ref[...], preferred_element_type=F32)


def _norm_mm(x, g, w, tn=512, front=None):
    d = x.shape[1]
    m = x.shape[0] + (0 if front is None else FRONT)
    n = w.shape[1]
    tm = _pick_tile(m, 640, 128)
    if front is None:
        kern, row_specs, rows = _norm_mm_kernel, [pl.BlockSpec((tm, d), lambda i, j: (i, 0))], (x,)
    else:
        kern, row_specs, rows = _norm_mm_tokens_kernel, _token_specs(tm, d), (x, front)
    return pl.pallas_call(
        kern,
        grid=(m // tm, n // tn),
        in_specs=row_specs + [pl.BlockSpec((1, d), lambda i, j: (0, 0)),
                              pl.BlockSpec((d, tn), lambda i, j: (0, j))],
        out_specs=pl.BlockSpec((tm, tn), lambda i, j: (i, j)),
        out_shape=jax.ShapeDtypeStruct((m, n), F32),
        scratch_shapes=[pltpu.VMEM((tm, d), BF16)],
        compiler_params=_params(("parallel", "arbitrary")),
        name="norm_proj",
    )(*rows, g.reshape(1, d), w)


def _norm_mm_t_kernel(x_ref, g_ref, wt_ref, o_ref, xn_ref):
    @pl.when(pl.program_id(1) == 0)
    def _():
        xn_ref[...] = _rms(x_ref[...], g_ref[...]).astype(BF16)

    o_ref[...] = lax.dot_general(wt_ref[...], xn_ref[...], (((1,), (1,)), ((), ())),
                                 preferred_element_type=F32)


def _norm_mm_t(x, g, wt, tc=512):
    m, d = x.shape
    n = wt.shape[0]
    tm = _pick_tile(m, 640, 128)
    return pl.pallas_call(
        _norm_mm_t_kernel,
        grid=(m // tm, n // tc),
        in_specs=[pl.BlockSpec((tm, d), lambda i, j: (i, 0)),
                  pl.BlockSpec((1, d), lambda i, j: (0, 0)),
                  pl.BlockSpec((tc, d), lambda i, j: (j, 0))],
        out_specs=pl.BlockSpec((tc, tm), lambda i, j: (j, i)),
        out_shape=jax.ShapeDtypeStruct((n, m), F32),
        scratch_shapes=[pltpu.VMEM((tm, d), BF16)],
        compiler_params=_params(("parallel", "arbitrary")),
        name="norm_proj_t",
    )(x, g.reshape(1, d), wt)


def _out_proj_kernel(a_ref, b_ref, wa_ref, wb_ref, g_ref, *rest, tokens):
    *h_refs, o_ref = rest
    y = jnp.dot(a_ref[...], wa_ref[...], preferred_element_type=F32)
    y = y + jnp.dot(b_ref[...], wb_ref[...], preferred_element_type=F32)
    h = _token_block(pl.program_id(0), *h_refs) if tokens else h_refs[0][...]
    o_ref[...] = h + _rms(y, g_ref[...])


def _out_proj(a, b, wa, wb, h, g, front=None):
    m, ka = a.shape
    kb = b.shape[1]
    d = wa.shape[1]
    tm = _pick_tile(m, 640, 128)
    if front is None:
        h_specs, hs = [pl.BlockSpec((tm, d), lambda i: (i, 0))], (h,)
    else:
        h_specs, hs = _token_specs(tm, d), (h, front)
    return pl.pallas_call(
        functools.partial(_out_proj_kernel, tokens=front is not None),
        grid=(m // tm,),
        in_specs=[pl.BlockSpec((tm, ka), lambda i: (i, 0)),
                  pl.BlockSpec((tm, kb), lambda i: (i, 0)),
                  pl.BlockSpec((ka, d), lambda i: (0, 0)),
                  pl.BlockSpec((kb, d), lambda i: (0, 0)),
                  pl.BlockSpec((1, d), lambda i: (0, 0))] + h_specs,
        out_specs=pl.BlockSpec((tm, d), lambda i: (i, 0)),
        out_shape=jax.ShapeDtypeStruct((m, d), F32),
        compiler_params=_params(("parallel",)),
        name="out_proj",
    )(a, b, wa, wb, g.reshape(1, d), *hs)


def _ffn_kernel(h_ref, gpre_ref, w1_ref, w3_ref, w2_ref, gpost_ref, o_ref, xn_ref, acc_ref):
    j = pl.program_id(1)

    @pl.when(j == 0)
    def _():
        xn_ref[...] = _rms(h_ref[...], gpre_ref[...]).astype(BF16)
        acc_ref[...] = jnp.zeros_like(acc_ref)

    xn = xn_ref[...]
    a = jnp.dot(xn, w1_ref[...], preferred_element_type=F32)
    b = jnp.dot(xn, w3_ref[...], preferred_element_type=F32)
    u = (a * jax.nn.sigmoid(a) * b).astype(BF16)
    acc_ref[...] += jnp.dot(u, w2_ref[...], preferred_element_type=F32)

    @pl.when(j == pl.num_programs(1) - 1)
    def _():
        o_ref[...] = h_ref[...] + _rms(acc_ref[...], gpost_ref[...])


def _ffn(h, gpre, w1, w3, w2, gpost, layer, tf=512, skip_front=False):
    d = h.shape[1]
    f = w1.shape[2]
    if skip_front:
        m = h.shape[0] - FRONT
        tm = _pick_tile(m, 640, 128)
        h_spec = pl.BlockSpec((pl.Element(tm), pl.Element(d)), lambda i, j: (pl.multiple_of(FRONT + i * tm, FRONT), 0))
    else:
        m = h.shape[0]
        tm = _pick_tile(m, 640, 128)
        h_spec = pl.BlockSpec((tm, d), lambda i, j: (i, 0))
    return pl.pallas_call(
        _ffn_kernel,
        grid=(m // tm, f // tf),
        in_specs=[h_spec,
                  pl.BlockSpec((1, d), lambda i, j: (0, 0)),
                  pl.BlockSpec((None, d, tf), lambda i, j: (layer, 0, j)),
                  pl.BlockSpec((None, d, tf), lambda i, j: (layer, 0, j)),
                  pl.BlockSpec((None, tf, d), lambda i, j: (layer, j, 0)),
                  pl.BlockSpec((1, d), lambda i, j: (0, 0))],
        out_specs=pl.BlockSpec((tm, d), lambda i, j: (i, 0)),
        out_shape=jax.ShapeDtypeStruct((m, d), F32),
        scratch_shapes=[pltpu.VMEM((tm, d), BF16), pltpu.VMEM((tm, d), F32)],
        compiler_params=_params(("parallel", "arbitrary")),
        name="ffn",
    )(h, gpre.reshape(1, d), w1, w3, w2, gpost.reshape(1, d))


def _cumsum_rows(tri, x):
    hi = x.astype(BF16)
    rest = x - hi.astype(F32)
    mid = rest.astype(BF16)
    lo = (rest - mid.astype(F32)).astype(BF16)
    return (jnp.dot(tri, hi, preferred_element_type=F32) + jnp.dot(tri, mid, preferred_element_type=F32)
            + jnp.dot(tri, lo, preferred_element_type=F32))


def _edge_rows(b, half):
    rows = b.shape[0]
    if half >= 8:
        parts = [jnp.broadcast_to(b[e:e + 1, :], (2 * half, b.shape[1]))
                 for e in range(half - 1, rows, 2 * half)]
        return parts[0] if len(parts) == 1 else jnp.concatenate(parts, axis=0)
    b3 = b.reshape(rows // 8, 8, b.shape[1])
    sub = lax.broadcasted_iota(jnp.int32, b3.shape, 1)
    pick = lambda r: jnp.broadcast_to(b3[:, r:r + 1, :], b3.shape)
    edge = pick(half - 1)
    for start in range(2 * half, 8, 2 * half):
        edge = jnp.where(sub >= start, pick(start + half - 1), edge)
    return edge.reshape(b.shape)


def _hgrn_kernel(q_ref, f_ref, v_ref, gate_ref, lbl_ref, gn_ref, o_ref, st_ref, *, n_heads, lb_row):
    c_rows = HGRN_CHUNK

    @pl.when(pl.program_id(1) == 0)
    def _():
        st_ref[...] = jnp.zeros_like(st_ref)

    logits = lbl_ref[...]
    ex = jnp.exp(logits - jnp.max(logits, axis=0, keepdims=True))
    lb_all = jnp.sum(ex[0:lb_row + 1, :], axis=0, keepdims=True) / jnp.sum(ex, axis=0, keepdims=True)

    r_i = lax.broadcasted_iota(jnp.int32, (c_rows, c_rows), 0)
    c_i = lax.broadcasted_iota(jnp.int32, (c_rows, c_rows), 1)
    tri = (r_i >= c_i).astype(BF16)
    levels = []
    half = c_rows // 2
    while half >= 1:
        shift = half.bit_length()
        upper = (lax.shift_right_logical(r_i, shift - 1) & 1) == 1
        same = lax.shift_right_logical(r_i, shift) == lax.shift_right_logical(c_i, shift)
        levels.append((half, upper, jnp.where(upper, 1.0, -1.0), same))
        half //= 2

    for hh in range(n_heads):
        cols = slice(hh * HGRN_HEAD, (hh + 1) * HGRN_HEAD)
        lb = lb_all[:, cols]
        q = q_ref[:, cols]
        v = v_ref[:, cols]
        f = lb + (1.0 - lb) * jax.nn.sigmoid(f_ref[:, cols])
        k = 1.0 - f
        b = _cumsum_rows(tri, jnp.log2(f))
        b_last = b[c_rows - 1:c_rows, :]

        st = st_ref[hh]
        inter = lax.dot_general((q * jnp.exp2(b)).astype(BF16), st.astype(BF16),
                                (((1,), (1,)), ((), ())), preferred_element_type=F32)
        kt = (k * jnp.exp2(b_last - b)).astype(BF16)
        st_ref[hh] = st * jnp.exp2(b_last) + lax.dot_general(
            v.astype(BF16), kt, (((0,), (0,)), ((), ())), preferred_element_type=F32)

        att = jnp.zeros((c_rows, c_rows), F32)
        for half, upper, sign, same in levels:
            decay = jnp.exp2((b - _edge_rows(b, half)) * sign)
            scaled = jnp.where(upper, q, k) * decay
            qt = jnp.where(upper, scaled, 0.0).astype(BF16)
            kl = jnp.where(upper, 0.0, scaled).astype(BF16)
            pair = lax.dot_general(qt, kl, (((1,), (1,)), ((), ())), preferred_element_type=F32)
            att = att + jnp.where(same, pair, 0.0)
        o = inter + jnp.dot(att.astype(BF16), v.astype(BF16), preferred_element_type=F32)
        o = o + jnp.sum(q * k, axis=-1, keepdims=True) * v

        gate = gate_ref[:, cols]
        o_ref[:, cols] = (_rms(o, gn_ref[...]) * (gate * jax.nn.sigmoid(gate))).astype(BF16)


def _hgrn(p, lb_logits, gn, lb_row):
    m = p.shape[0]
    d_a = lb_logits.shape[1]
    hp = 8
    width = hp * HGRN_HEAD
    groups = d_a // width
    tb = HGRN_CHUNK
    n_l = lb_logits.shape[0]
    col = lambda off: (lambda g, t: (t, off + g))
    kern = functools.partial(_hgrn_kernel, n_heads=hp, lb_row=lb_row)
    return pl.pallas_call(
        kern,
        grid=(groups, m // tb),
        in_specs=[pl.BlockSpec((tb, width), col(0)),
                  pl.BlockSpec((tb, width), col(groups)),
                  pl.BlockSpec((tb, width), col(2 * groups)),
                  pl.BlockSpec((tb, width), col(3 * groups)),
                  pl.BlockSpec((n_l, width), lambda g, t: (0, g)),
                  pl.BlockSpec((1, HGRN_HEAD), lambda g, t: (0, 0))],
        out_specs=pl.BlockSpec((tb, width), lambda g, t: (t, g)),
        out_shape=jax.ShapeDtypeStruct((m, d_a), BF16),
        scratch_shapes=[pltpu.VMEM((hp, HGRN_HEAD, HGRN_HEAD), F32)],
        compiler_params=_params(("parallel", "arbitrary")),
        name="hgrn2",
    )(p, p, p, p, lb_logits, gn.reshape(1, HGRN_HEAD))


def _sconv_kernel(sx_ref, sb_ref, sc_ref, sxp_ref, scp_ref, w_ref, o_ref, ext_ref, *, taps):
    tm = sx_ref.shape[0]
    prev = sxp_ref[...] * scp_ref[...]
    ext_ref[0:8, :] = jnp.where(pl.program_id(0) > 0, prev, 0.0)
    ext_ref[8:8 + tm, :] = sx_ref[...] * sc_ref[...]
    y = jnp.zeros(sx_ref.shape, F32)
    for j in range(taps):
        s = 8 - (taps - 1) + j
        y = y + w_ref[j:j + 1, :] * ext_ref[s:s + tm, :]
    o_ref[...] = (sb_ref[...] * y).astype(BF16)


def _sconv(p, w, col0, width):
    m = p.shape[0]
    taps = w.shape[0]
    tm = _pick_tile(m, 640, 128)
    cb = col0 // width
    prev = lambda off: (lambda i: (jnp.maximum(i * (tm // 8) - 1, 0), off))
    return pl.pallas_call(
        functools.partial(_sconv_kernel, taps=taps),
        grid=(m // tm,),
        in_specs=[pl.BlockSpec((tm, width), lambda i: (i, cb)),
                  pl.BlockSpec((tm, width), lambda i: (i, cb + 1)),
                  pl.BlockSpec((tm, width), lambda i: (i, cb + 2)),
                  pl.BlockSpec((8, width), prev(cb)),
                  pl.BlockSpec((8, width), prev(cb + 2)),
                  pl.BlockSpec((taps, width), lambda i: (0, 0))],
        out_specs=pl.BlockSpec((tm, width), lambda i: (i, 0)),
        out_shape=jax.ShapeDtypeStruct((m, width), BF16),
        scratch_shapes=[pltpu.VMEM((tm + 8, width), F32)],
        compiler_params=_params(("parallel",)),
        name="sconv",
    )(p, p, p, p, p, w)


def _rglru_kernel(rx_ref, ry_ref, rxp_ref, cw_ref, cb_ref, wa_ref, ba_ref, wi_ref, bi_ref, lam_ref,
                  o_ref, ext_ref, a_ref, x_ref, hs_ref, h_ref, *, taps):
    tm, width = rx_ref.shape
    i = pl.program_id(0)

    @pl.when(i == 0)
    def _():
        h_ref[...] = jnp.zeros_like(h_ref)

    ext_ref[0:8, :] = jnp.where(i > 0, rxp_ref[...], 0.0)
    ext_ref[8:8 + tm, :] = rx_ref[...]
    u = jnp.zeros((tm, width), F32) + cb_ref[...]
    for j in range(taps):
        s = 8 - (taps - 1) + j
        u = u + cw_ref[j:j + 1, :] * ext_ref[s:s + tm, :]

    u_b = u.astype(BF16)
    r_parts, i_parts = [], []
    for n in range(width // RG_BLOCK):
        blk = slice(n * RG_BLOCK, (n + 1) * RG_BLOCK)
        r_parts.append(jnp.dot(u_b[:, blk], wa_ref[n], preferred_element_type=F32))
        i_parts.append(jnp.dot(u_b[:, blk], wi_ref[n], preferred_element_type=F32))
    r = jax.nn.sigmoid(jnp.concatenate(r_parts, axis=1) + ba_ref[...])
    ig = jax.nn.sigmoid(jnp.concatenate(i_parts, axis=1) + bi_ref[...])

    neg_lam = -lam_ref[...]
    softplus = jnp.maximum(neg_lam, 0.0) + jnp.log1p(jnp.exp(-jnp.abs(neg_lam)))
    log_a = -RG_C * r * softplus
    row = i * tm + lax.broadcasted_iota(jnp.int32, (tm, 1), 0)
    a = jnp.exp(log_a)
    xin = jnp.sqrt(1.0 - a * a) * (ig * u)
    a_ref[...] = a
    x_ref[...] = jnp.where(row >= PAD_ROWS, xin, 0.0)

    def group(gidx, h):
        base = pl.multiple_of(gidx * 8, 8)
        a8 = a_ref[pl.ds(base, 8), :]
        x8 = x_ref[pl.ds(base, 8), :]
        for rr in range(8):
            h = a8[rr:rr + 1, :] * h + x8[rr:rr + 1, :]
            hs_ref[pl.ds(base + rr, 1), :] = h
        return h

    h_ref[...] = lax.fori_loop(0, tm // 8, group, h_ref[...])
    o_ref[...] = (hs_ref[...] * jax.nn.gelu(ry_ref[...])).astype(BF16)


def _rglru(p, cw, cb, wa, ba, wi, bi, lam, width):
    m = p.shape[0]
    taps = cw.shape[0]
    tm = _pick_tile(m, 640, 128)
    nb = width // RG_BLOCK
    row = lambda v: v.reshape(1, width)
    full2 = lambda shape: pl.BlockSpec(shape, lambda i: (0, 0))
    full3 = lambda shape: pl.BlockSpec(shape, lambda i: (0, 0, 0))
    return pl.pallas_call(
        functools.partial(_rglru_kernel, taps=taps),
        grid=(m // tm,),
        in_specs=[pl.BlockSpec((tm, width), lambda i: (i, 0)),
                  pl.BlockSpec((tm, width), lambda i: (i, 1)),
                  pl.BlockSpec((8, width), lambda i: (jnp.maximum(i * (tm // 8) - 1, 0), 0)),
                  full2((taps, width)), full2((1, width)),
                  full3((nb, RG_BLOCK, RG_BLOCK)), full2((1, width)),
                  full3((nb, RG_BLOCK, RG_BLOCK)), full2((1, width)),
                  full2((1, width))],
        out_specs=pl.BlockSpec((tm, width), lambda i: (i, 0)),
        out_shape=jax.ShapeDtypeStruct((m, width), BF16),
        scratch_shapes=[pltpu.VMEM((tm + 8, width), F32),
                        pltpu.VMEM((tm, width), F32),
                        pltpu.VMEM((tm, width), F32),
                        pltpu.VMEM((tm, width), F32),
                        pltpu.VMEM((1, width), F32)],
        compiler_params=_params(("arbitrary",)),
        name="rglru",
    )(p, p, p, cw, row(cb), wa, row(ba), wi, row(bi), row(lam))


def _latent_kernel(c_ref, ct_ref, g_ref, gt_ref, o_ref, ot_ref):
    o_ref[...] = _rms(c_ref[...], g_ref[...]).astype(BF16)
    ct = ct_ref[...]
    ot_ref[0:KV_RANK, :] = (ct * lax.rsqrt(jnp.mean(ct * ct, axis=0, keepdims=True) + EPS)
                            * gt_ref[...]).astype(BF16)
    first = lax.broadcasted_iota(jnp.int32, (ONES_ROWS, ct.shape[1]), 0) == 0
    ot_ref[KV_RANK:KV_RANK + ONES_ROWS, :] = jnp.where(first, 1.0, 0.0).astype(BF16)


def _latent(p, pt, g, col0, row0):
    m = p.shape[0]
    r = g.shape[0]
    tm = _pick_tile(m, 640, 128)
    return pl.pallas_call(
        _latent_kernel,
        grid=(m // tm,),
        in_specs=[pl.BlockSpec((tm, r), lambda i: (i, col0 // r)),
                  pl.BlockSpec((r, tm), lambda i: (row0 // r, i)),
                  pl.BlockSpec((1, r), lambda i: (0, 0)),
                  pl.BlockSpec((r, 1), lambda i: (0, 0))],
        out_specs=[pl.BlockSpec((tm, r), lambda i: (i, 0)),
                   pl.BlockSpec((r + ONES_ROWS, tm), lambda i: (0, i))],
        out_shape=[jax.ShapeDtypeStruct((m, r), BF16), jax.ShapeDtypeStruct((r + ONES_ROWS, m), BF16)],
        compiler_params=_params(("parallel",)),
        name="latent_norm",
    )(p, pt, g.reshape(1, r), g.reshape(r, 1))


def _dsa_kernel(qt_ref, iqt_ref, iwt_ref, c_ref, ct_ref, ik_ref, wuk_ref, wuvt_ref, o_ref,
                sc_ref, iqp_ref, qlt_ref, acc_ref, m_ref, sa_ref, sb_ref, *, k_sel):
    i = pl.program_id(0)
    tq = Q_BLOCK
    n_heads = ATT_HEADS
    sub_blocks = KEY_CHUNK // tq
    n_chunks = (i * tq + KEY_CHUNK - 1) // KEY_CHUNK

    scale = ATT_HEAD_DIM ** -0.5 * LOG2_E
    for h in range(n_heads):
        qh = qt_ref[h * ATT_HEAD_DIM:(h + 1) * ATT_HEAD_DIM, :].astype(BF16)
        ql = jnp.dot(wuk_ref[h], qh, preferred_element_type=F32) * scale
        qlt_ref[:, h * tq:(h + 1) * tq] = ql.astype(BF16)

    iw = iwt_ref[...] * ((IDX_DIM ** -0.5) * (IDX_HEADS ** -0.5))
    k_loc = lax.broadcasted_iota(jnp.int32, (tq, tq), 0)
    q_loc = lax.broadcasted_iota(jnp.int32, (tq, tq), 1)

    for hp in range(IDX_HEADS // 2):
        pair = jnp.concatenate(
            [iqt_ref[(2 * hp) * IDX_DIM:(2 * hp + 1) * IDX_DIM, :],
             iqt_ref[(2 * hp + 1) * IDX_DIM:(2 * hp + 2) * IDX_DIM, :]], axis=1).astype(BF16)
        iqp_ref[hp, 0:IDX_DIM, :] = pair
        iqp_ref[hp, IDX_DIM:2 * IDX_DIM, :] = jnp.zeros_like(pair)

    def score_chunk(j, carry):
        mn, mx = carry
        for u in range(sub_blocks):
            kb = j * sub_blocks + 1 + u
            r0 = pl.multiple_of(kb * tq, tq)
            ikb = ik_ref[pl.ds(r0, tq), :]
            s = jnp.zeros((tq, tq), F32)
            for hp in range(IDX_HEADS // 2):
                x = jnp.dot(ikb, iqp_ref[hp], preferred_element_type=F32)
                s = s + jnp.maximum(x[:, 0:tq], 0.0) * iw[2 * hp:2 * hp + 1, :]
                s = s + jnp.maximum(x[:, tq:2 * tq], 0.0) * iw[2 * hp + 1:2 * hp + 2, :]
            visible = jnp.logical_or(kb < i, jnp.logical_and(kb == i, k_loc <= q_loc))
            sc_ref[pl.ds(r0, tq), :] = jnp.where(visible, s, -jnp.inf)
            mx = jnp.maximum(mx, jnp.max(jnp.where(visible, s, -jnp.inf).reshape(tq // 8, 8, tq), axis=0))
            mn = jnp.minimum(mn, jnp.min(jnp.where(visible, s, jnp.inf).reshape(tq // 8, 8, tq), axis=0))
        return mn, mx

    mn, mx = lax.fori_loop(0, n_chunks, score_chunk,
                           (jnp.full((8, tq), jnp.inf, F32), jnp.full((8, tq), -jnp.inf, F32)))
    row_min = jnp.min(mn, axis=0, keepdims=True)
    row_max = jnp.max(mx, axis=0, keepdims=True)

    groups = KEY_CHUNK // 64

    def chunk_scores(j):
        r0 = pl.multiple_of(FRONT + j * KEY_CHUNK, FRONT)
        return sc_ref[pl.ds(r0, KEY_CHUNK), :].reshape(groups, 8, 8, tq)

    def count_ge(t):
        def body(j, cnt):
            kk = chunk_scores(j)
            for g in range(groups):
                cnt = jnp.where(kk[g] >= t, cnt + 1, cnt)
            return cnt
        cnt = lax.fori_loop(0, n_chunks, body, jnp.zeros((8, 8, tq), jnp.int32))
        return jnp.sum(jnp.sum(cnt, axis=0), axis=0, keepdims=True)

    lane = lax.broadcasted_iota(jnp.int32, (1, tq), 1)
    n_visible = (i - 1) * tq + lane + 1
    k_row = jnp.minimum(k_sel, n_visible)

    def probe(mid, movable, lo, hi, cnt_lo):
        c = count_ge(mid)
        up = jnp.logical_and(c >= k_row, movable)
        down = jnp.logical_and(jnp.logical_not(up), movable)
        return jnp.where(up, mid, lo), jnp.where(down, mid, hi), jnp.where(up, c, cnt_lo)

    def midpoint(lo, hi, cnt_lo):
        mid = lo + 0.5 * (hi - lo)
        movable = jnp.logical_and(cnt_lo != k_row, jnp.logical_and(mid > lo, mid < hi))
        return mid, movable

    def any_lane(flag):
        return jnp.max(flag.astype(jnp.int32))

    lo, hi, cnt_lo = probe(row_max, n_visible > k_row, row_min, row_max, n_visible)

    def bisect_cond(state):
        return jnp.logical_and(state[0] > 0, state[1] < BISECT_CAP)

    def bisect_body(state):
        _, it, lo, hi, cnt_lo = state
        for _ in range(BISECT_STEPS):
            mid, movable = midpoint(lo, hi, cnt_lo)
            lo, hi, cnt_lo = probe(mid, movable, lo, hi, cnt_lo)
        return any_lane(midpoint(lo, hi, cnt_lo)[1]), it + 1, lo, hi, cnt_lo

    state = (any_lane(midpoint(lo, hi, cnt_lo)[1]), jnp.int32(0), lo, hi, cnt_lo)
    thr = lax.while_loop(bisect_cond, bisect_body, state)[2]

    qlt = qlt_ref[...]
    head_cols = [slice(h * tq, (h + 1) * tq) for h in range(n_heads)]

    def masked_scores(rows, bias, s_ref):
        bias2 = jnp.concatenate([bias, bias], axis=1)
        cmax = []
        for hp in range(n_heads // 2):
            cols = slice(2 * hp * tq, (2 * hp + 2) * tq)
            sm = jnp.dot(c_ref[rows, :], qlt[:, cols], preferred_element_type=F32) + bias2
            s_ref[:, cols] = sm
            cmax.append(jnp.max(sm, axis=0, keepdims=True))
        return jnp.concatenate(cmax, axis=1)

    def accumulate(rows, s_ref, cmax):
        m_old = m_ref[...]
        m_new = jnp.maximum(m_old, cmax)
        m_ref[...] = m_new
        p = jnp.exp2(s_ref[...] - m_new).astype(BF16)
        acc_ref[...] = jnp.exp2(m_old - m_new) * acc_ref[...] + jnp.dot(
            ct_ref[:, rows], p, preferred_element_type=F32)

    m_ref[...] = jnp.full(m_ref.shape, NEG_BIG, F32)
    acc_ref[...] = jnp.zeros_like(acc_ref)

    q_row = i * tq + q_loc
    allowed0 = jnp.logical_and(jnp.logical_or(k_loc >= PAD_ROWS, k_loc == q_row), k_loc <= q_row)
    s0_ref = sa_ref.at[0:FRONT, :]
    cmax0 = masked_scores(slice(0, FRONT), jnp.where(allowed0, 0.0, NEG_BIG), s0_ref)
    accumulate(slice(0, FRONT), s0_ref, cmax0)

    def chunk_rows(j):
        jc = jnp.clip(j, 0, n_chunks - 1)
        return pl.ds(pl.multiple_of(FRONT + jc * KEY_CHUNK, FRONT), KEY_CHUNK)

    def chunk_scores_masked(j, s_ref):
        rows = chunk_rows(j)
        ok = jnp.logical_and(sc_ref[rows, :] >= thr, j < n_chunks)
        return masked_scores(rows, jnp.where(ok, 0.0, NEG_BIG), s_ref)

    @pl.when(i > 0)
    def _():
        def chunk_pair(t, cmax_a):
            j = 2 * t
            cmax_b = chunk_scores_masked(j + 1, sb_ref)
            accumulate(chunk_rows(j), sa_ref, cmax_a)
            cmax_a = chunk_scores_masked(j + 2, sa_ref)
            accumulate(chunk_rows(j + 1), sb_ref, cmax_b)
            return cmax_a

        lax.fori_loop(0, (n_chunks + 1) // 2, chunk_pair, chunk_scores_masked(0, sa_ref))

    o_lat = (acc_ref[0:KV_RANK, :] / acc_ref[KV_RANK:KV_RANK + 1, :]).astype(BF16)
    for h in range(n_heads):
        oh = jnp.dot(wuvt_ref[h], o_lat[:, head_cols[h]], preferred_element_type=F32)
        o_ref[:, h * ATT_HEAD_DIM:(h + 1) * ATT_HEAD_DIM] = oh.T.astype(BF16)


def _dsa(pt, c, ct, ik, wuk, wuvt, k_sel, q_row0, iq_row0, iw_row0):
    m = c.shape[0]
    tq = Q_BLOCK
    d_q = ATT_HEADS * ATT_HEAD_DIM
    d_iq = IDX_HEADS * IDX_DIM
    full2 = lambda shape: pl.BlockSpec(shape, lambda i: (0, 0))
    full3 = lambda shape: pl.BlockSpec(shape, lambda i: (0, 0, 0))
    return pl.pallas_call(
        functools.partial(_dsa_kernel, k_sel=k_sel),
        grid=(m // tq,),
        in_specs=[pl.BlockSpec((d_q, tq), lambda i: (q_row0 // d_q, i)),
                  pl.BlockSpec((d_iq, tq), lambda i: (iq_row0 // d_iq, i)),
                  pl.BlockSpec((IDX_HEADS, tq), lambda i: (iw_row0 // IDX_HEADS, i)),
                  full2(c.shape), full2(ct.shape), full2(ik.shape),
                  full3(wuk.shape), full3(wuvt.shape)],
        out_specs=pl.BlockSpec((tq, d_q), lambda i: (i, 0)),
        out_shape=jax.ShapeDtypeStruct((m, d_q), BF16),
        scratch_shapes=[pltpu.VMEM((m, tq), F32),
                        pltpu.VMEM((IDX_HEADS // 2, 2 * IDX_DIM, 2 * tq), BF16),
                        pltpu.VMEM((KV_RANK, ATT_HEADS * tq), BF16),
                        pltpu.VMEM((KV_RANK + ONES_ROWS, ATT_HEADS * tq), F32),
                        pltpu.VMEM((1, ATT_HEADS * tq), F32),
                        pltpu.VMEM((KEY_CHUNK, ATT_HEADS * tq), F32),
                        pltpu.VMEM((KEY_CHUNK, ATT_HEADS * tq), F32)],
        compiler_params=_params(("arbitrary",)),
        name="dsa",
    )(pt, pt, pt, c, ct, ik, wuk, wuvt)


def _pad_cols(w, n):
    return jnp.pad(w, ((0, 0), (0, n - w.shape[1])))


def kernel(x, meta_tokens, ln_mix_pre, ln_mix_post, ln_ffn_pre, ln_ffn_post, ffn_w1, ffn_w3, ffn_w2,
           ab_w_in, ab_w_out, hgrn_lb_logits, hgrn_out_norm, sconv_w,
           cd_w_in, cd_w_out, rg_conv_w, rg_conv_b, rg_w_a, rg_b_a, rg_w_i, rg_b_i, rg_lambda,
           mla_kv_norm, mla_w_uk, mla_w_uv):
    assert x.shape[0] == 1
    seq, d = x.shape[1], x.shape[2]
    assert seq % KEY_CHUNK == 0
    d_a = hgrn_lb_logits.shape[1]
    d_b = sconv_w.shape[2]
    d_c = rg_lambda.shape[1]
    d_d = ATT_HEADS * ATT_HEAD_DIM
    d_iq = IDX_HEADS * IDX_DIM
    k_sel = min(TOPK_MAX, seq // 4)

    x2 = x[0]
    front = jnp.concatenate([jnp.zeros((PAD_ROWS, d), F32), meta_tokens.astype(F32)], axis=0)
    w1, w3, w2 = ffn_w1.astype(BF16), ffn_w3.astype(BF16), ffn_w2.astype(BF16)

    p0 = _norm_mm(x2, ln_mix_pre[0], ab_w_in[0].astype(BF16), tn=1024, front=front)
    og = _hgrn(p0, hgrn_lb_logits, hgrn_out_norm[0], lb_row=0)
    yb = _sconv(p0, sconv_w[0], col0=4 * d_a, width=d_b)
    w_out = ab_w_out[0].astype(BF16)
    h = _out_proj(og, yb, w_out[:d_a], w_out[d_a:], x2, ln_mix_post[0], front=front)
    h = _ffn(h, ln_ffn_pre[0], w1, w3, w2, ln_ffn_post[0], layer=0)

    w_in = cd_w_in[0]
    o_rx, o_ry, o_q, o_c = 0, d_c, 2 * d_c, 2 * d_c + d_d
    o_iq = o_c + KV_RANK
    o_ik = o_iq + d_iq
    o_iw = o_ik + IDX_DIM
    w_rows = jnp.concatenate([w_in[:, o_rx:o_q], w_in[:, o_c:o_iq], w_in[:, o_ik:o_iw]], axis=1)
    w_rows = _pad_cols(w_rows, -(-w_rows.shape[1] // 512) * 512).astype(BF16)
    w_cols = jnp.concatenate([w_in[:, o_q:o_c], w_in[:, o_iq:o_ik], w_in[:, o_c:o_iq], w_in[:, o_iw:]], axis=1)
    w_cols = _pad_cols(w_cols, -(-w_cols.shape[1] // 512) * 512).astype(BF16).T
    p1 = _norm_mm(h, ln_mix_pre[1], w_rows, tn=w_rows.shape[1] // 2)
    p1t = _norm_mm_t(h, ln_mix_pre[1], w_cols, tc=w_cols.shape[0] // 2)

    hc = _rglru(p1, rg_conv_w[0], rg_conv_b[0], rg_w_a[0].astype(BF16), rg_b_a[0],
                rg_w_i[0].astype(BF16), rg_b_i[0], rg_lambda[0], width=d_c)
    c, ct = _latent(p1, p1t, mla_kv_norm[0], col0=2 * d_c, row0=d_d + d_iq)
    ik = p1[:, 2 * d_c + KV_RANK:2 * d_c + KV_RANK + 2 * IDX_DIM].astype(BF16)
    wuk = jnp.transpose(mla_w_uk[0], (1, 0, 2)).astype(BF16)
    wuvt = jnp.transpose(mla_w_uv[0], (1, 2, 0)).astype(BF16)
    att = _dsa(p1t, c, ct, ik, wuk, wuvt, k_sel,
               q_row0=0, iq_row0=d_d, iw_row0=d_d + d_iq + KV_RANK)
    w_out = cd_w_out[0].astype(BF16)
    h = _out_proj(hc, att, w_out[:d_c], w_out[d_c:], h, ln_mix_post[1])
    return _ffn(h, ln_ffn_pre[1], w1, w3, w2, ln_ffn_post[1], layer=1, skip_front=True)[None]
```

```python
import functools

import jax
import jax.numpy as jnp
from jax import lax
from jax.experimental import pallas as pl
from jax.experimental.pallas import tpu as pltpu

F32 = jnp.float32
BF16 = jnp.bfloat16

EPS = 1e-6
N_META = 16
FRONT = 128
PAD_ROWS = FRONT - N_META
HGRN_HEAD = 128
HGRN_CHUNK = 128
RG_BLOCK = 128
RG_C = 8.0
ATT_HEADS = 8
ATT_HEAD_DIM = 128
KV_RANK = 256
IDX_HEADS = 16
IDX_DIM = 64
TOPK_MAX = 256
Q_BLOCK = 128
KEY_CHUNK = 512
ONES_ROWS = 16
LOG2_E = 1.4426950408889634
BISECT_STEPS = 4
BISECT_CAP = 64
NEG_BIG = -1e30
VMEM_LIMIT = 56 * 1024 * 1024
PROJ_ROWS = 1664


def _pick_tile(n, target, mult):
    best = None
    for t in range(mult, min(n, target) + 1, mult):
        if n % t == 0:
            best = t
    assert best is not None, (n, target, mult)
    return best


def _params(sem):
    return pltpu.CompilerParams(dimension_semantics=sem, vmem_limit_bytes=VMEM_LIMIT)


def _rms(x, g):
    return x * lax.rsqrt(jnp.mean(x * x, axis=-1, keepdims=True) + EPS) * g


def _token_block(i, x_ref, front_ref):
    xb = x_ref[...]
    first = jnp.concatenate([front_ref[...], xb[0:xb.shape[0] - FRONT, :]], axis=0)
    return jnp.where(i == 0, first, xb)


def _token_specs(tm, d):
    x_map = lambda *a: (pl.multiple_of(jnp.maximum(a[0] * tm - FRONT, 0), FRONT), 0)
    return [pl.BlockSpec((pl.Element(tm), pl.Element(d)), x_map),
            pl.BlockSpec((FRONT, d), lambda *a: (0, 0))]


def _fill_normed(x_ref, front_ref, g_ref, xn_ref):
    g = g_ref[...]
    first = 0 if front_ref is None else (pl.program_id(0) == 0).astype(jnp.int32)

    def piece(p, carry):
        src = pl.multiple_of(jnp.maximum(p - first, 0) * FRONT, FRONT)
        rows = x_ref[pl.ds(src, FRONT), :]
        if front_ref is not None:
            rows = jnp.where(jnp.logical_and(first == 1, p == 0), front_ref[...], rows)
        xn_ref[pl.ds(pl.multiple_of(p * FRONT, FRONT), FRONT), :] = _rms(rows, g).astype(BF16)
        return carry

    lax.fori_loop(0, xn_ref.shape[0] // FRONT, piece, 0)


def _norm_mm_kernel(x_ref, *rest, tokens):
    front_ref = rest[0] if tokens else None
    g_ref, w_ref, o_ref, xn_ref = rest[-4:]

    @pl.when(pl.program_id(1) == 0)
    def _():
        _fill_normed(x_ref, front_ref, g_ref, xn_ref)

    o_ref[...] = jnp.dot(xn_ref[...], w_ref[...], preferred_element_type=F32)


def _norm_mm(x, g, w, tn=512, front=None):
    d = x.shape[1]
    m = x.shape[0] + (0 if front is None else FRONT)
    n = w.shape[1]
    tm = _pick_tile(m, PROJ_ROWS, FRONT)
    if front is None:
        row_specs, rows = [pl.BlockSpec((tm, d), lambda i, j: (i, 0))], (x,)
    else:
        row_specs, rows = _token_specs(tm, d), (x, front)
    return pl.pallas_call(
        functools.partial(_norm_mm_kernel, tokens=front is not None),
        grid=(m // tm, n // tn),
        in_specs=row_specs + [pl.BlockSpec((1, d), lambda i, j: (0, 0)),
                              pl.BlockSpec((d, tn), lambda i, j: (0, j))],
        out_specs=pl.BlockSpec((tm, tn), lambda i, j: (i, j)),
        out_shape=jax.ShapeDtypeStruct((m, n), F32),
        scratch_shapes=[pltpu.VMEM((tm, d), BF16)],
        compiler_params=_params(("parallel", "arbitrary")),
        name="norm_proj",
    )(*rows, g.reshape(1, d), w)


def _norm_mm_t_kernel(x_ref, g_ref, wt_ref, o_ref, xn_ref):
    @pl.when(pl.program_id(1) == 0)
    def _():
        _fill_normed(x_ref, None, g_ref, xn_ref)

    o_ref[...] = lax.dot_general(wt_ref[...], xn_ref[...], (((1,), (1,)), ((), ())),
                                 preferred_element_type=F32)


def _norm_mm_t(x, g, wt, tc=512):
    m, d = x.shape
    n = wt.shape[0]
    tm = _pick_tile(m, PROJ_ROWS, FRONT)
    return pl.pallas_call(
        _norm_mm_t_kernel,
        grid=(m // tm, n // tc),
        in_specs=[pl.BlockSpec((tm, d), lambda i, j: (i, 0)),
                  pl.BlockSpec((1, d), lambda i, j: (0, 0)),
                  pl.BlockSpec((tc, d), lambda i, j: (j, 0))],
        out_specs=pl.BlockSpec((tc, tm), lambda i, j: (j, i)),
        out_shape=jax.ShapeDtypeStruct((n, m), F32),
        scratch_shapes=[pltpu.VMEM((tm, d), BF16)],
        compiler_params=_params(("parallel", "arbitrary")),
        name="norm_proj_t",
    )(x, g.reshape(1, d), wt)


def _out_proj_kernel(a_ref, b_ref, wa_ref, wb_ref, g_ref, *rest, tokens):
    *h_refs, o_ref = rest
    y = jnp.dot(a_ref[...], wa_ref[...], preferred_element_type=F32)
    y = y + jnp.dot(b_ref[...], wb_ref[...], preferred_element_type=F32)
    h = _token_block(pl.program_id(0), *h_refs) if tokens else h_refs[0][...]
    o_ref[...] = h + _rms(y, g_ref[...])


def _out_proj(a, b, wa, wb, h, g, front=None):
    m, ka = a.shape
    kb = b.shape[1]
    d = wa.shape[1]
    tm = _pick_tile(m, 640, 128)
    if front is None:
        h_specs, hs = [pl.BlockSpec((tm, d), lambda i: (i, 0))], (h,)
    else:
        h_specs, hs = _token_specs(tm, d), (h, front)
    return pl.pallas_call(
        functools.partial(_out_proj_kernel, tokens=front is not None),
        grid=(m // tm,),
        in_specs=[pl.BlockSpec((tm, ka), lambda i: (i, 0)),
                  pl.BlockSpec((tm, kb), lambda i: (i, 0)),
                  pl.BlockSpec((ka, d), lambda i: (0, 0)),
                  pl.BlockSpec((kb, d), lambda i: (0, 0)),
                  pl.BlockSpec((1, d), lambda i: (0, 0))] + h_specs,
        out_specs=pl.BlockSpec((tm, d), lambda i: (i, 0)),
        out_shape=jax.ShapeDtypeStruct((m, d), F32),
        compiler_params=_params(("parallel",)),
        name="out_proj",
    )(a, b, wa, wb, g.reshape(1, d), *hs)


def _ffn_kernel(h_ref, gpre_ref, w1_ref, w3_ref, w2_ref, gpost_ref, o_ref, xn_ref, acc_ref):
    j = pl.program_id(1)

    @pl.when(j == 0)
    def _():
        xn_ref[...] = _rms(h_ref[...], gpre_ref[...]).astype(BF16)
        acc_ref[...] = jnp.zeros_like(acc_ref)

    xn = xn_ref[...]
    a = jnp.dot(xn, w1_ref[...], preferred_element_type=F32)
    b = jnp.dot(xn, w3_ref[...], preferred_element_type=F32)
    u = (a * jax.nn.sigmoid(a) * b).astype(BF16)
    acc_ref[...] += jnp.dot(u, w2_ref[...], preferred_element_type=F32)

    @pl.when(j == pl.num_programs(1) - 1)
    def _():
        o_ref[...] = h_ref[...] + _rms(acc_ref[...], gpost_ref[...])


def _ffn(h, gpre, w1, w3, w2, gpost, layer, tf=512, skip_front=False):
    d = h.shape[1]
    f = w1.shape[2]
    if skip_front:
        m = h.shape[0] - FRONT
        tm = _pick_tile(m, 640, 128)
        h_spec = pl.BlockSpec((pl.Element(tm), pl.Element(d)), lambda i, j: (pl.multiple_of(FRONT + i * tm, FRONT), 0))
    else:
        m = h.shape[0]
        tm = _pick_tile(m, 640, 128)
        h_spec = pl.BlockSpec((tm, d), lambda i, j: (i, 0))
    return pl.pallas_call(
        _ffn_kernel,
        grid=(m // tm, f // tf),
        in_specs=[h_spec,
                  pl.BlockSpec((1, d), lambda i, j: (0, 0)),
                  pl.BlockSpec((None, d, tf), lambda i, j: (layer, 0, j)),
                  pl.BlockSpec((None, d, tf), lambda i, j: (layer, 0, j)),
                  pl.BlockSpec((None, tf, d), lambda i, j: (layer, j, 0)),
                  pl.BlockSpec((1, d), lambda i, j: (0, 0))],
        out_specs=pl.BlockSpec((tm, d), lambda i, j: (i, 0)),
        out_shape=jax.ShapeDtypeStruct((m, d), F32),
        scratch_shapes=[pltpu.VMEM((tm, d), BF16), pltpu.VMEM((tm, d), F32)],
        compiler_params=_params(("parallel", "arbitrary")),
        name="ffn",
    )(h, gpre.reshape(1, d), w1, w3, w2, gpost.reshape(1, d))


def _cumsum_rows(tri, x):
    hi = x.astype(BF16)
    rest = x - hi.astype(F32)
    mid = rest.astype(BF16)
    lo = (rest - mid.astype(F32)).astype(BF16)
    return (jnp.dot(tri, hi, preferred_element_type=F32) + jnp.dot(tri, mid, preferred_element_type=F32)
            + jnp.dot(tri, lo, preferred_element_type=F32))


def _edge_rows(b, half):
    rows = b.shape[0]
    if half >= 8:
        parts = [jnp.broadcast_to(b[e:e + 1, :], (2 * half, b.shape[1]))
                 for e in range(half - 1, rows, 2 * half)]
        return parts[0] if len(parts) == 1 else jnp.concatenate(parts, axis=0)
    b3 = b.reshape(rows // 8, 8, b.shape[1])
    sub = lax.broadcasted_iota(jnp.int32, b3.shape, 1)
    pick = lambda r: jnp.broadcast_to(b3[:, r:r + 1, :], b3.shape)
    edge = pick(half - 1)
    for start in range(2 * half, 8, 2 * half):
        edge = jnp.where(sub >= start, pick(start + half - 1), edge)
    return edge.reshape(b.shape)


def _hgrn_kernel(q_ref, f_ref, v_ref, gate_ref, lbl_ref, gn_ref, o_ref, st_ref, *, n_heads, lb_row):
    c_rows = HGRN_CHUNK

    @pl.when(pl.program_id(1) == 0)
    def _():
        st_ref[...] = jnp.zeros_like(st_ref)

    logits = lbl_ref[...]
    ex = jnp.exp(logits - jnp.max(logits, axis=0, keepdims=True))
    lb_all = jnp.sum(ex[0:lb_row + 1, :], axis=0, keepdims=True) / jnp.sum(ex, axis=0, keepdims=True)

    r_i = lax.broadcasted_iota(jnp.int32, (c_rows, c_rows), 0)
    c_i = lax.broadcasted_iota(jnp.int32, (c_rows, c_rows), 1)
    tri = (r_i >= c_i).astype(BF16)
    levels = []
    half = c_rows // 2
    while half >= 1:
        shift = half.bit_length()
        upper = (lax.shift_right_logical(r_i, shift - 1) & 1) == 1
        same = lax.shift_right_logical(r_i, shift) == lax.shift_right_logical(c_i, shift)
        levels.append((half, upper, jnp.where(upper, 1.0, -1.0), same))
        half //= 2

    for hh in range(n_heads):
        cols = slice(hh * HGRN_HEAD, (hh + 1) * HGRN_HEAD)
        lb = lb_all[:, cols]
        q = q_ref[:, cols]
        v = v_ref[:, cols]
        f = lb + (1.0 - lb) * jax.nn.sigmoid(f_ref[:, cols])
        k = 1.0 - f
        b = _cumsum_rows(tri, jnp.log2(f))
        b_last = b[c_rows - 1:c_rows, :]

        st = st_ref[hh]
        inter = lax.dot_general((q * jnp.exp2(b)).astype(BF16), st.astype(BF16),
                                (((1,), (1,)), ((), ())), preferred_element_type=F32)
        kt = (k * jnp.exp2(b_last - b)).astype(BF16)
        st_ref[hh] = st * jnp.exp2(b_last) + lax.dot_general(
            v.astype(BF16), kt, (((0,), (0,)), ((), ())), preferred_element_type=F32)

        att = jnp.zeros((c_rows, c_rows), F32)
        for half, upper, sign, same in levels:
            decay = jnp.exp2((b - _edge_rows(b, half)) * sign)
            scaled = jnp.where(upper, q, k) * decay
            qt = jnp.where(upper, scaled, 0.0).astype(BF16)
            kl = jnp.where(upper, 0.0, scaled).astype(BF16)
            pair = lax.dot_general(qt, kl, (((1,), (1,)), ((), ())), preferred_element_type=F32)
            att = att + jnp.where(same, pair, 0.0)
        o = inter + jnp.dot(att.astype(BF16), v.astype(BF16), preferred_element_type=F32)
        o = o + jnp.sum(q * k, axis=-1, keepdims=True) * v

        gate = gate_ref[:, cols]
        o_ref[:, cols] = (_rms(o, gn_ref[...]) * (gate * jax.nn.sigmoid(gate))).astype(BF16)


def _hgrn(p, lb_logits, gn, lb_row):
    m = p.shape[0]
    d_a = lb_logits.shape[1]
    hp = 8
    width = hp * HGRN_HEAD
    groups = d_a // width
    tb = HGRN_CHUNK
    n_l = lb_logits.shape[0]
    col = lambda off: (lambda g, t: (t, off + g))
    kern = functools.partial(_hgrn_kernel, n_heads=hp, lb_row=lb_row)
    return pl.pallas_call(
        kern,
        grid=(groups, m // tb),
        in_specs=[pl.BlockSpec((tb, width), col(0)),
                  pl.BlockSpec((tb, width), col(groups)),
                  pl.BlockSpec((tb, width), col(2 * groups)),
                  pl.BlockSpec((tb, width), col(3 * groups)),
                  pl.BlockSpec((n_l, width), lambda g, t: (0, g)),
                  pl.BlockSpec((1, HGRN_HEAD), lambda g, t: (0, 0))],
        out_specs=pl.BlockSpec((tb, width), lambda g, t: (t, g)),
        out_shape=jax.ShapeDtypeStruct((m, d_a), BF16),
        scratch_shapes=[pltpu.VMEM((hp, HGRN_HEAD, HGRN_HEAD), F32)],
        compiler_params=_params(("parallel", "arbitrary")),
        name="hgrn2",
    )(p, p, p, p, lb_logits, gn.reshape(1, HGRN_HEAD))


def _sconv_kernel(sx_ref, sb_ref, sc_ref, sxp_ref, scp_ref, w_ref, o_ref, ext_ref, *, taps):
    tm = sx_ref.shape[0]
    prev = sxp_ref[...] * scp_ref[...]
    ext_ref[0:8, :] = jnp.where(pl.program_id(0) > 0, prev, 0.0)
    ext_ref[8:8 + tm, :] = sx_ref[...] * sc_ref[...]
    y = jnp.zeros(sx_ref.shape, F32)
    for j in range(taps):
        s = 8 - (taps - 1) + j
        y = y + w_ref[j:j + 1, :] * ext_ref[s:s + tm, :]
    o_ref[...] = (sb_ref[...] * y).astype(BF16)


def _sconv(p, w, col0, width):
    m = p.shape[0]
    taps = w.shape[0]
    tm = _pick_tile(m, 640, 128)
    cb = col0 // width
    prev = lambda off: (lambda i: (jnp.maximum(i * (tm // 8) - 1, 0), off))
    return pl.pallas_call(
        functools.partial(_sconv_kernel, taps=taps),
        grid=(m // tm,),
        in_specs=[pl.BlockSpec((tm, width), lambda i: (i, cb)),
                  pl.BlockSpec((tm, width), lambda i: (i, cb + 1)),
                  pl.BlockSpec((tm, width), lambda i: (i, cb + 2)),
                  pl.BlockSpec((8, width), prev(cb)),
                  pl.BlockSpec((8, width), prev(cb + 2)),
                  pl.BlockSpec((taps, width), lambda i: (0, 0))],
        out_specs=pl.BlockSpec((tm, width), lambda i: (i, 0)),
        out_shape=jax.ShapeDtypeStruct((m, width), BF16),
        scratch_shapes=[pltpu.VMEM((tm + 8, width), F32)],
        compiler_params=_params(("parallel",)),
        name="sconv",
    )(p, p, p, p, p, w)


def _rglru_kernel(rx_ref, ry_ref, rxp_ref, cw_ref, cb_ref, wa_ref, ba_ref, wi_ref, bi_ref, lam_ref,
                  o_ref, ext_ref, a_ref, x_ref, hs_ref, h_ref, *, taps):
    tm, width = rx_ref.shape
    i = pl.program_id(0)

    @pl.when(i == 0)
    def _():
        h_ref[...] = jnp.zeros_like(h_ref)

    ext_ref[0:8, :] = jnp.where(i > 0, rxp_ref[...], 0.0)
    ext_ref[8:8 + tm, :] = rx_ref[...]
    u = jnp.zeros((tm, width), F32) + cb_ref[...]
    for j in range(taps):
        s = 8 - (taps - 1) + j
        u = u + cw_ref[j:j + 1, :] * ext_ref[s:s + tm, :]

    u_b = u.astype(BF16)
    r_parts, i_parts = [], []
    for n in range(width // RG_BLOCK):
        blk = slice(n * RG_BLOCK, (n + 1) * RG_BLOCK)
        r_parts.append(jnp.dot(u_b[:, blk], wa_ref[n], preferred_element_type=F32))
        i_parts.append(jnp.dot(u_b[:, blk], wi_ref[n], preferred_element_type=F32))
    r = jax.nn.sigmoid(jnp.concatenate(r_parts, axis=1) + ba_ref[...])
    ig = jax.nn.sigmoid(jnp.concatenate(i_parts, axis=1) + bi_ref[...])

    neg_lam = -lam_ref[...]
    softplus = jnp.maximum(neg_lam, 0.0) + jnp.log1p(jnp.exp(-jnp.abs(neg_lam)))
    log_a = -RG_C * r * softplus
    row = i * tm + lax.broadcasted_iota(jnp.int32, (tm, 1), 0)
    a = jnp.exp(log_a)
    xin = jnp.sqrt(1.0 - a * a) * (ig * u)
    a_ref[...] = a
    x_ref[...] = jnp.where(row >= PAD_ROWS, xin, 0.0)

    def group(gidx, h):
        base = pl.multiple_of(gidx * 8, 8)
        a8 = a_ref[pl.ds(base, 8), :]
        x8 = x_ref[pl.ds(base, 8), :]
        for rr in range(8):
            h = a8[rr:rr + 1, :] * h + x8[rr:rr + 1, :]
            hs_ref[pl.ds(base + rr, 1), :] = h
        return h

    h_ref[...] = lax.fori_loop(0, tm // 8, group, h_ref[...])
    o_ref[...] = (hs_ref[...] * jax.nn.gelu(ry_ref[...])).astype(BF16)


def _rglru(p, cw, cb, wa, ba, wi, bi, lam, width):
    m = p.shape[0]
    taps = cw.shape[0]
    tm = _pick_tile(m, 640, 128)
    nb = width // RG_BLOCK
    row = lambda v: v.reshape(1, width)
    full2 = lambda shape: pl.BlockSpec(shape, lambda i: (0, 0))
    full3 = lambda shape: pl.BlockSpec(shape, lambda i: (0, 0, 0))
    return pl.pallas_call(
        functools.partial(_rglru_kernel, taps=taps),
        grid=(m // tm,),
        in_specs=[pl.BlockSpec((tm, width), lambda i: (i, 0)),
                  pl.BlockSpec((tm, width), lambda i: (i, 1)),
                  pl.BlockSpec((8, width), lambda i: (jnp.maximum(i * (tm // 8) - 1, 0), 0)),
                  full2((taps, width)), full2((1, width)),
                  full3((nb, RG_BLOCK, RG_BLOCK)), full2((1, width)),
                  full3((nb, RG_BLOCK, RG_BLOCK)), full2((1, width)),
                  full2((1, width))],
        out_specs=pl.BlockSpec((tm, width), lambda i: (i, 0)),
        out_shape=jax.ShapeDtypeStruct((m, width), BF16),
        scratch_shapes=[pltpu.VMEM((tm + 8, width), F32),
                        pltpu.VMEM((tm, width), F32),
                        pltpu.VMEM((tm, width), F32),
                        pltpu.VMEM((tm, width), F32),
                        pltpu.VMEM((1, width), F32)],
        compiler_params=_params(("arbitrary",)),
        name="rglru",
    )(p, p, p, cw, row(cb), wa, row(ba), wi, row(bi), row(lam))


def _latent_kernel(c_ref, ct_ref, g_ref, gt_ref, o_ref, ot_ref):
    o_ref[...] = _rms(c_ref[...], g_ref[...]).astype(BF16)
    ct = ct_ref[...]
    ot_ref[0:KV_RANK, :] = (ct * lax.rsqrt(jnp.mean(ct * ct, axis=0, keepdims=True) + EPS)
                            * gt_ref[...]).astype(BF16)
    first = lax.broadcasted_iota(jnp.int32, (ONES_ROWS, ct.shape[1]), 0) == 0
    ot_ref[KV_RANK:KV_RANK + ONES_ROWS, :] = jnp.where(first, 1.0, 0.0).astype(BF16)


def _latent(p, pt, g, col0, row0):
    m = p.shape[0]
    r = g.shape[0]
    tm = _pick_tile(m, 640, 128)
    return pl.pallas_call(
        _latent_kernel,
        grid=(m // tm,),
        in_specs=[pl.BlockSpec((tm, r), lambda i: (i, col0 // r)),
                  pl.BlockSpec((r, tm), lambda i: (row0 // r, i)),
                  pl.BlockSpec((1, r), lambda i: (0, 0)),
                  pl.BlockSpec((r, 1), lambda i: (0, 0))],
        out_specs=[pl.BlockSpec((tm, r), lambda i: (i, 0)),
                   pl.BlockSpec((r + ONES_ROWS, tm), lambda i: (0, i))],
        out_shape=[jax.ShapeDtypeStruct((m, r), BF16), jax.ShapeDtypeStruct((r + ONES_ROWS, m), BF16)],
        compiler_params=_params(("parallel",)),
        name="latent_norm",
    )(p, pt, g.reshape(1, r), g.reshape(r, 1))


def _dsa_kernel(qt_ref, iqt_ref, iwt_ref, c_ref, ct_ref, ik_ref, wuk_ref, wuvt_ref, o_ref,
                sc_ref, iqp_ref, qlt_ref, acc_ref, m_ref, sa_ref, sb_ref, *, k_sel):
    i = pl.program_id(0)
    tq = Q_BLOCK
    n_heads = ATT_HEADS
    sub_blocks = KEY_CHUNK // tq
    n_chunks = (i * tq + KEY_CHUNK - 1) // KEY_CHUNK

    scale = ATT_HEAD_DIM ** -0.5 * LOG2_E
    for h in range(n_heads):
        qh = qt_ref[h * ATT_HEAD_DIM:(h + 1) * ATT_HEAD_DIM, :].astype(BF16)
        ql = jnp.dot(wuk_ref[h], qh, preferred_element_type=F32) * scale
        qlt_ref[:, h * tq:(h + 1) * tq] = ql.astype(BF16)

    iw = iwt_ref[...] * ((IDX_DIM ** -0.5) * (IDX_HEADS ** -0.5))
    k_loc = lax.broadcasted_iota(jnp.int32, (tq, tq), 0)
    q_loc = lax.broadcasted_iota(jnp.int32, (tq, tq), 1)

    for hp in range(IDX_HEADS // 2):
        pair = jnp.concatenate(
            [iqt_ref[(2 * hp) * IDX_DIM:(2 * hp + 1) * IDX_DIM, :],
             iqt_ref[(2 * hp + 1) * IDX_DIM:(2 * hp + 2) * IDX_DIM, :]], axis=1).astype(BF16)
        iqp_ref[hp, 0:IDX_DIM, :] = pair
        iqp_ref[hp, IDX_DIM:2 * IDX_DIM, :] = jnp.zeros_like(pair)

    def score_chunk(j, carry):
        mn, mx = carry
        for u in range(sub_blocks):
            kb = j * sub_blocks + 1 + u
            r0 = pl.multiple_of(kb * tq, tq)
            ikb = ik_ref[pl.ds(r0, tq), :]
            s = jnp.zeros((tq, tq), F32)
            for hp in range(IDX_HEADS // 2):
                x = jnp.dot(ikb, iqp_ref[hp], preferred_element_type=F32)
                s = s + jnp.maximum(x[:, 0:tq], 0.0) * iw[2 * hp:2 * hp + 1, :]
                s = s + jnp.maximum(x[:, tq:2 * tq], 0.0) * iw[2 * hp + 1:2 * hp + 2, :]
            visible = jnp.logical_or(kb < i, jnp.logical_and(kb == i, k_loc <= q_loc))
            sc_ref[pl.ds(r0, tq), :] = jnp.where(visible, s, -jnp.inf)
            mx = jnp.maximum(mx, jnp.max(jnp.where(visible, s, -jnp.inf).reshape(tq // 8, 8, tq), axis=0))
            mn = jnp.minimum(mn, jnp.min(jnp.where(visible, s, jnp.inf).reshape(tq // 8, 8, tq), axis=0))
        return mn, mx

    mn, mx = lax.fori_loop(0, n_chunks, score_chunk,
                           (jnp.full((8, tq), jnp.inf, F32), jnp.full((8, tq), -jnp.inf, F32)))
    row_min = jnp.min(mn, axis=0, keepdims=True)
    row_max = jnp.max(mx, axis=0, keepdims=True)

    groups = KEY_CHUNK // 64

    def chunk_scores(j):
        r0 = pl.multiple_of(FRONT + j * KEY_CHUNK, FRONT)
        return sc_ref[pl.ds(r0, KEY_CHUNK), :].reshape(groups, 8, 8, tq)

    def count_ge(t):
        def body(j, cnt):
            kk = chunk_scores(j)
            for g in range(groups):
                cnt = jnp.where(kk[g] >= t, cnt + 1, cnt)
            return cnt
        cnt = lax.fori_loop(0, n_chunks, body, jnp.zeros((8, 8, tq), jnp.int32))
        return jnp.sum(jnp.sum(cnt, axis=0), axis=0, keepdims=True)

    lane = lax.broadcasted_iota(jnp.int32, (1, tq), 1)
    n_visible = (i - 1) * tq + lane + 1
    k_row = jnp.minimum(k_sel, n_visible)

    def probe(mid, movable, lo, hi, cnt_lo):
        c = count_ge(mid)
        up = jnp.logical_and(c >= k_row, movable)
        down = jnp.logical_and(jnp.logical_not(up), movable)
        return jnp.where(up, mid, lo), jnp.where(down, mid, hi), jnp.where(up, c, cnt_lo)

    def midpoint(lo, hi, cnt_lo):
        mid = lo + 0.5 * (hi - lo)
        movable = jnp.logical_and(cnt_lo != k_row, jnp.logical_and(mid > lo, mid < hi))
        return mid, movable

    def any_lane(flag):
        return jnp.max(flag.astype(jnp.int32))

    lo, hi, cnt_lo = probe(row_max, n_visible > k_row, row_min, row_max, n_visible)

    def bisect_cond(state):
        return jnp.logical_and(state[0] > 0, state[1] < BISECT_CAP)

    def bisect_body(state):
        _, it, lo, hi, cnt_lo = state
        for _ in range(BISECT_STEPS):
            mid, movable = midpoint(lo, hi, cnt_lo)
            lo, hi, cnt_lo = probe(mid, movable, lo, hi, cnt_lo)
        return any_lane(midpoint(lo, hi, cnt_lo)[1]), it + 1, lo, hi, cnt_lo

    state = (any_lane(midpoint(lo, hi, cnt_lo)[1]), jnp.int32(0), lo, hi, cnt_lo)
    thr = lax.while_loop(bisect_cond, bisect_body, state)[2]

    qlt = qlt_ref[...]
    head_cols = [slice(h * tq, (h + 1) * tq) for h in range(n_heads)]

    def masked_scores(rows, bias, s_ref):
        bias2 = jnp.concatenate([bias, bias], axis=1)
        cmax = []
        for hp in range(n_heads // 2):
            cols = slice(2 * hp * tq, (2 * hp + 2) * tq)
            sm = jnp.dot(c_ref[rows, :], qlt[:, cols], preferred_element_type=F32) + bias2
            s_ref[:, cols] = sm
            cmax.append(jnp.max(sm, axis=0, keepdims=True))
        return jnp.concatenate(cmax, axis=1)

    def accumulate(rows, s_ref, cmax):
        m_old = m_ref[...]
        m_new = jnp.maximum(m_old, cmax)
        m_ref[...] = m_new
        p = jnp.exp2(s_ref[...] - m_new).astype(BF16)
        acc_ref[...] = jnp.exp2(m_old - m_new) * acc_ref[...] + jnp.dot(
            ct_ref[:, rows], p, preferred_element_type=F32)

    m_ref[...] = jnp.full(m_ref.shape, NEG_BIG, F32)
    acc_ref[...] = jnp.zeros_like(acc_ref)

    q_row = i * tq + q_loc
    allowed0 = jnp.logical_and(jnp.logical_or(k_loc >= PAD_ROWS, k_loc == q_row), k_loc <= q_row)
    s0_ref = sa_ref.at[0:FRONT, :]
    cmax0 = masked_scores(slice(0, FRONT), jnp.where(allowed0, 0.0, NEG_BIG), s0_ref)
    accumulate(slice(0, FRONT), s0_ref, cmax0)

    def chunk_rows(j):
        jc = jnp.clip(j, 0, n_chunks - 1)
        return pl.ds(pl.multiple_of(FRONT + jc * KEY_CHUNK, FRONT), KEY_CHUNK)

    def chunk_scores_masked(j, s_ref):
        rows = chunk_rows(j)
        ok = jnp.logical_and(sc_ref[rows, :] >= thr, j < n_chunks)
        return masked_scores(rows, jnp.where(ok, 0.0, NEG_BIG), s_ref)

    @pl.when(i > 0)
    def _():
        def chunk_pair(t, cmax_a):
            j = 2 * t
            cmax_b = chunk_scores_masked(j + 1, sb_ref)
            accumulate(chunk_rows(j), sa_ref, cmax_a)
            cmax_a = chunk_scores_masked(j + 2, sa_ref)
            accumulate(chunk_rows(j + 1), sb_ref, cmax_b)
            return cmax_a

        lax.fori_loop(0, (n_chunks + 1) // 2, chunk_pair, chunk_scores_masked(0, sa_ref))

    o_lat = (acc_ref[0:KV_RANK, :] / acc_ref[KV_RANK:KV_RANK + 1, :]).astype(BF16)
    for h in range(n_heads):
        oh = jnp.dot(wuvt_ref[h], o_lat[:, head_cols[h]], preferred_element_type=F32)
        o_ref[:, h * ATT_HEAD_DIM:(h + 1) * ATT_HEAD_DIM] = oh.T.astype(BF16)


def _dsa(pt, c, ct, ik, wuk, wuvt, k_sel, q_row0, iq_row0, iw_row0):
    m = c.shape[0]
    tq = Q_BLOCK
    d_q = ATT_HEADS * ATT_HEAD_DIM
    d_iq = IDX_HEADS * IDX_DIM
    full2 = lambda shape: pl.BlockSpec(shape, lambda i: (0, 0))
    full3 = lambda shape: pl.BlockSpec(shape, lambda i: (0, 0, 0))
    return pl.pallas_call(
        functools.partial(_dsa_kernel, k_sel=k_sel),
        grid=(m // tq,),
        in_specs=[pl.BlockSpec((d_q, tq), lambda i: (q_row0 // d_q, i)),
                  pl.BlockSpec((d_iq, tq), lambda i: (iq_row0 // d_iq, i)),
                  pl.BlockSpec((IDX_HEADS, tq), lambda i: (iw_row0 // IDX_HEADS, i)),
                  full2(c.shape), full2(ct.shape), full2(ik.shape),
                  full3(wuk.shape), full3(wuvt.shape)],
        out_specs=pl.BlockSpec((tq, d_q), lambda i: (i, 0)),
        out_shape=jax.ShapeDtypeStruct((m, d_q), BF16),
        scratch_shapes=[pltpu.VMEM((m, tq), F32),
                        pltpu.VMEM((IDX_HEADS // 2, 2 * IDX_DIM, 2 * tq), BF16),
                        pltpu.VMEM((KV_RANK, ATT_HEADS * tq), BF16),
                        pltpu.VMEM((KV_RANK + ONES_ROWS, ATT_HEADS * tq), F32),
                        pltpu.VMEM((1, ATT_HEADS * tq), F32),
                        pltpu.VMEM((KEY_CHUNK, ATT_HEADS * tq), F32),
                        pltpu.VMEM((KEY_CHUNK, ATT_HEADS * tq), F32)],
        compiler_params=_params(("arbitrary",)),
        name="dsa",
    )(pt, pt, pt, c, ct, ik, wuk, wuvt)


def _pad_cols(w, n):
    return jnp.pad(w, ((0, 0), (0, n - w.shape[1])))


def kernel(x, meta_tokens, ln_mix_pre, ln_mix_post, ln_ffn_pre, ln_ffn_post, ffn_w1, ffn_w3, ffn_w2,
           ab_w_in, ab_w_out, hgrn_lb_logits, hgrn_out_norm, sconv_w,
           cd_w_in, cd_w_out, rg_conv_w, rg_conv_b, rg_w_a, rg_b_a, rg_w_i, rg_b_i, rg_lambda,
           mla_kv_norm, mla_w_uk, mla_w_uv):
    assert x.shape[0] == 1
    seq, d = x.shape[1], x.shape[2]
    assert seq % KEY_CHUNK == 0
    d_a = hgrn_lb_logits.shape[1]
    d_b = sconv_w.shape[2]
    d_c = rg_lambda.shape[1]
    d_d = ATT_HEADS * ATT_HEAD_DIM
    d_iq = IDX_HEADS * IDX_DIM
    k_sel = min(TOPK_MAX, seq // 4)

    x2 = x[0]
    front = jnp.concatenate([jnp.zeros((PAD_ROWS, d), F32), meta_tokens.astype(F32)], axis=0)
    w1, w3, w2 = ffn_w1.astype(BF16), ffn_w3.astype(BF16), ffn_w2.astype(BF16)

    p0 = _norm_mm(x2, ln_mix_pre[0], ab_w_in[0].astype(BF16), tn=512, front=front)
    og = _hgrn(p0, hgrn_lb_logits, hgrn_out_norm[0], lb_row=0)
    yb = _sconv(p0, sconv_w[0], col0=4 * d_a, width=d_b)
    w_out = ab_w_out[0].astype(BF16)
    h = _out_proj(og, yb, w_out[:d_a], w_out[d_a:], x2, ln_mix_post[0], front=front)
    h = _ffn(h, ln_ffn_pre[0], w1, w3, w2, ln_ffn_post[0], layer=0)

    w_in = cd_w_in[0]
    o_rx, o_ry, o_q, o_c = 0, d_c, 2 * d_c, 2 * d_c + d_d
    o_iq = o_c + KV_RANK
    o_ik = o_iq + d_iq
    o_iw = o_ik + IDX_DIM
    w_rows = jnp.concatenate([w_in[:, o_rx:o_q], w_in[:, o_c:o_iq], w_in[:, o_ik:o_iw]], axis=1)
    w_rows = _pad_cols(w_rows, -(-w_rows.shape[1] // 512) * 512).astype(BF16)
    w_cols = jnp.concatenate([w_in[:, o_q:o_c], w_in[:, o_iq:o_ik], w_in[:, o_c:o_iq], w_in[:, o_iw:]], axis=1)
    w_cols = _pad_cols(w_cols, -(-w_cols.shape[1] // 512) * 512).astype(BF16).T
    p1 = _norm_mm(h, ln_mix_pre[1], w_rows, tn=w_rows.shape[1] // 4)
    p1t = _norm_mm_t(h, ln_mix_pre[1], w_cols, tc=w_cols.shape[0] // 4)

    hc = _rglru(p1, rg_conv_w[0], rg_conv_b[0], rg_w_a[0].astype(BF16), rg_b_a[0],
                rg_w_i[0].astype(BF16), rg_b_i[0], rg_lambda[0], width=d_c)
    c, ct = _latent(p1, p1t, mla_kv_norm[0], col0=2 * d_c, row0=d_d + d_iq)
    ik = p1[:, 2 * d_c + KV_RANK:2 * d_c + KV_RANK + 2 * IDX_DIM].astype(BF16)
    wuk = jnp.transpose(mla_w_uk[0], (1, 0, 2)).astype(BF16)
    wuvt = jnp.transpose(mla_w_uv[0], (1, 2, 0)).astype(BF16)
    att = _dsa(p1t, c, ct, ik, wuk, wuvt, k_sel,
               q_row0=0, iq_row0=d_d, iw_row0=d_d + d_iq + KV_RANK)
    w_out = cd_w_out[0].astype(BF16)
    h = _out_proj(hc, att, w_out[:d_c], w_out[d_c:], h, ln_mix_post[1])
    return _ffn(h, ln_ffn_pre[1], w1, w3, w2, ln_ffn_post[1], layer=1, skip_front=True)[None]
```

```python
import functools

import jax
import jax.numpy as jnp
from jax import lax
from jax.experimental import pallas as pl
from jax.experimental.pallas import tpu as pltpu

F32 = jnp.float32
BF16 = jnp.bfloat16

EPS = 1e-6
N_META = 16
FRONT = 128
PAD_ROWS = FRONT - N_META
HGRN_HEAD = 128
HGRN_CHUNK = 128
RG_BLOCK = 128
RG_C = 8.0
ATT_HEADS = 8
ATT_HEAD_DIM = 128
KV_RANK = 256
IDX_HEADS = 16
IDX_DIM = 64
TOPK_MAX = 256
Q_BLOCK = 128
KEY_CHUNK = 512
ONES_ROWS = 16
LOG2_E = 1.4426950408889634
BISECT_STEPS = 4
BISECT_CAP = 64
NEG_BIG = -1e30
VMEM_LIMIT = 56 * 1024 * 1024
PROJ_ROWS = 1664


def _pick_tile(n, target, mult):
    best = None
    for t in range(mult, min(n, target) + 1, mult):
        if n % t == 0:
            best = t
    assert best is not None, (n, target, mult)
    return best


def _params(sem):
    return pltpu.CompilerParams(dimension_semantics=sem, vmem_limit_bytes=VMEM_LIMIT)


def _rms(x, g):
    return x * lax.rsqrt(jnp.mean(x * x, axis=-1, keepdims=True) + EPS) * g


def _token_specs(tm, d):
    x_map = lambda *a: (pl.multiple_of(jnp.maximum(a[0] * tm - FRONT, 0), FRONT), 0)
    return [pl.BlockSpec((pl.Element(tm), pl.Element(d)), x_map),
            pl.BlockSpec((FRONT, d), lambda *a: (0, 0))]


def _fill_normed(x_ref, front_ref, g_ref, xn_ref):
    g = g_ref[...]
    first = 0 if front_ref is None else (pl.program_id(0) == 0).astype(jnp.int32)

    def piece(p, carry):
        src = pl.multiple_of(jnp.maximum(p - first, 0) * FRONT, FRONT)
        rows = x_ref[pl.ds(src, FRONT), :]
        if front_ref is not None:
            rows = jnp.where(jnp.logical_and(first == 1, p == 0), front_ref[...], rows)
        xn_ref[pl.ds(pl.multiple_of(p * FRONT, FRONT), FRONT), :] = _rms(rows, g).astype(BF16)
        return carry

    lax.fori_loop(0, xn_ref.shape[0] // FRONT, piece, 0)


def _norm_mm_kernel(x_ref, *rest, tokens):
    front_ref = rest[0] if tokens else None
    g_ref, w_ref, o_ref, xn_ref = rest[-4:]

    @pl.when(pl.program_id(1) == 0)
    def _():
        _fill_normed(x_ref, front_ref, g_ref, xn_ref)

    o_ref[...] = jnp.dot(xn_ref[...], w_ref[...], preferred_element_type=F32)


def _norm_mm(x, g, w, tn=512, front=None):
    d = x.shape[1]
    m = x.shape[0] + (0 if front is None else FRONT)
    n = w.shape[1]
    tm = _pick_tile(m, PROJ_ROWS, FRONT)
    if front is None:
        row_specs, rows = [pl.BlockSpec((tm, d), lambda i, j: (i, 0))], (x,)
    else:
        row_specs, rows = _token_specs(tm, d), (x, front)
    return pl.pallas_call(
        functools.partial(_norm_mm_kernel, tokens=front is not None),
        grid=(m // tm, n // tn),
        in_specs=row_specs + [pl.BlockSpec((1, d), lambda i, j: (0, 0)),
                              pl.BlockSpec((d, tn), lambda i, j: (0, j))],
        out_specs=pl.BlockSpec((tm, tn), lambda i, j: (i, j)),
        out_shape=jax.ShapeDtypeStruct((m, n), F32),
        scratch_shapes=[pltpu.VMEM((tm, d), BF16)],
        compiler_params=_params(("parallel", "arbitrary")),
        name="norm_proj",
    )(*rows, g.reshape(1, d), w)


def _norm_mm_t_kernel(x_ref, g_ref, wt_ref, o_ref, xn_ref):
    @pl.when(pl.program_id(1) == 0)
    def _():
        _fill_normed(x_ref, None, g_ref, xn_ref)

    o_ref[...] = lax.dot_general(wt_ref[...], xn_ref[...], (((1,), (1,)), ((), ())),
                                 preferred_element_type=F32)


def _norm_mm_t(x, g, wt, tc=512):
    m, d = x.shape
    n = wt.shape[0]
    tm = _pick_tile(m, PROJ_ROWS, FRONT)
    return pl.pallas_call(
        _norm_mm_t_kernel,
        grid=(m // tm, n // tc),
        in_specs=[pl.BlockSpec((tm, d), lambda i, j: (i, 0)),
                  pl.BlockSpec((1, d), lambda i, j: (0, 0)),
                  pl.BlockSpec((tc, d), lambda i, j: (j, 0))],
        out_specs=pl.BlockSpec((tc, tm), lambda i, j: (j, i)),
        out_shape=jax.ShapeDtypeStruct((n, m), F32),
        scratch_shapes=[pltpu.VMEM((tm, d), BF16)],
        compiler_params=_params(("parallel", "arbitrary")),
        name="norm_proj_t",
    )(x, g.reshape(1, d), wt)


def _out_proj_kernel(a_ref, b_ref, wa_ref, wb_ref, g_ref, *rest, tokens):
    *h_refs, o_ref = rest
    y = jnp.dot(a_ref[...], wa_ref[...], preferred_element_type=F32)
    y = y + jnp.dot(b_ref[...], wb_ref[...], preferred_element_type=F32)
    y = _rms(y, g_ref[...])
    if not tokens:
        o_ref[...] = h_refs[0][...] + y
        return
    x_ref, front_ref = h_refs
    tm = y.shape[0]

    @pl.when(pl.program_id(0) == 0)
    def _():
        o_ref[0:FRONT, :] = front_ref[...] + y[0:FRONT, :]
        o_ref[FRONT:tm, :] = x_ref[0:tm - FRONT, :] + y[FRONT:tm, :]

    @pl.when(pl.program_id(0) > 0)
    def _():
        o_ref[...] = x_ref[...] + y


def _out_proj(a, b, wa, wb, h, g, front=None):
    m, ka = a.shape
    kb = b.shape[1]
    d = wa.shape[1]
    tm = _pick_tile(m, 640, 128)
    if front is None:
        h_specs, hs = [pl.BlockSpec((tm, d), lambda i: (i, 0))], (h,)
    else:
        h_specs, hs = _token_specs(tm, d), (h, front)
    return pl.pallas_call(
        functools.partial(_out_proj_kernel, tokens=front is not None),
        grid=(m // tm,),
        in_specs=[pl.BlockSpec((tm, ka), lambda i: (i, 0)),
                  pl.BlockSpec((tm, kb), lambda i: (i, 0)),
                  pl.BlockSpec((ka, d), lambda i: (0, 0)),
                  pl.BlockSpec((kb, d), lambda i: (0, 0)),
                  pl.BlockSpec((1, d), lambda i: (0, 0))] + h_specs,
        out_specs=pl.BlockSpec((tm, d), lambda i: (i, 0)),
        out_shape=jax.ShapeDtypeStruct((m, d), F32),
        compiler_params=_params(("parallel",)),
        name="out_proj",
    )(a, b, wa, wb, g.reshape(1, d), *hs)


def _ffn_kernel(h_ref, gpre_ref, w1_ref, w3_ref, w2_ref, gpost_ref, o_ref, xn_ref, acc_ref):
    j = pl.program_id(1)

    @pl.when(j == 0)
    def _():
        xn_ref[...] = _rms(h_ref[...], gpre_ref[...]).astype(BF16)
        acc_ref[...] = jnp.zeros_like(acc_ref)

    xn = xn_ref[...]
    a = jnp.dot(xn, w1_ref[...], preferred_element_type=F32)
    b = jnp.dot(xn, w3_ref[...], preferred_element_type=F32)
    u = (a * jax.nn.sigmoid(a) * b).astype(BF16)
    acc_ref[...] += jnp.dot(u, w2_ref[...], preferred_element_type=F32)

    @pl.when(j == pl.num_programs(1) - 1)
    def _():
        o_ref[...] = h_ref[...] + _rms(acc_ref[...], gpost_ref[...])


def _ffn(h, gpre, w1, w3, w2, gpost, layer, tf=512, skip_front=False):
    d = h.shape[1]
    f = w1.shape[2]
    if skip_front:
        m = h.shape[0] - FRONT
        tm = _pick_tile(m, 640, 128)
        h_spec = pl.BlockSpec((pl.Element(tm), pl.Element(d)), lambda i, j: (pl.multiple_of(FRONT + i * tm, FRONT), 0))
    else:
        m = h.shape[0]
        tm = _pick_tile(m, 640, 128)
        h_spec = pl.BlockSpec((tm, d), lambda i, j: (i, 0))
    return pl.pallas_call(
        _ffn_kernel,
        grid=(m // tm, f // tf),
        in_specs=[h_spec,
                  pl.BlockSpec((1, d), lambda i, j: (0, 0)),
                  pl.BlockSpec((None, d, tf), lambda i, j: (layer, 0, j)),
                  pl.BlockSpec((None, d, tf), lambda i, j: (layer, 0, j)),
                  pl.BlockSpec((None, tf, d), lambda i, j: (layer, j, 0)),
                  pl.BlockSpec((1, d), lambda i, j: (0, 0))],
        out_specs=pl.BlockSpec((tm, d), lambda i, j: (i, 0)),
        out_shape=jax.ShapeDtypeStruct((m, d), F32),
        scratch_shapes=[pltpu.VMEM((tm, d), BF16), pltpu.VMEM((tm, d), F32)],
        compiler_params=_params(("parallel", "arbitrary")),
        name="ffn",
    )(h, gpre.reshape(1, d), w1, w3, w2, gpost.reshape(1, d))


def _cumsum_rows(tri, x):
    hi = x.astype(BF16)
    rest = x - hi.astype(F32)
    mid = rest.astype(BF16)
    lo = (rest - mid.astype(F32)).astype(BF16)
    return (jnp.dot(tri, hi, preferred_element_type=F32) + jnp.dot(tri, mid, preferred_element_type=F32)
            + jnp.dot(tri, lo, preferred_element_type=F32))


def _edge_rows(b, half):
    rows = b.shape[0]
    if half >= 8:
        parts = [jnp.broadcast_to(b[e:e + 1, :], (2 * half, b.shape[1]))
                 for e in range(half - 1, rows, 2 * half)]
        return parts[0] if len(parts) == 1 else jnp.concatenate(parts, axis=0)
    b3 = b.reshape(rows // 8, 8, b.shape[1])
    sub = lax.broadcasted_iota(jnp.int32, b3.shape, 1)
    pick = lambda r: jnp.broadcast_to(b3[:, r:r + 1, :], b3.shape)
    edge = pick(half - 1)
    for start in range(2 * half, 8, 2 * half):
        edge = jnp.where(sub >= start, pick(start + half - 1), edge)
    return edge.reshape(b.shape)


def _hgrn_kernel(q_ref, f_ref, v_ref, gate_ref, lbl_ref, gn_ref, o_ref, st_ref, *, n_heads, lb_row):
    c_rows = HGRN_CHUNK

    @pl.when(pl.program_id(1) == 0)
    def _():
        st_ref[...] = jnp.zeros_like(st_ref)

    logits = lbl_ref[...]
    ex = jnp.exp(logits - jnp.max(logits, axis=0, keepdims=True))
    lb_all = jnp.sum(ex[0:lb_row + 1, :], axis=0, keepdims=True) / jnp.sum(ex, axis=0, keepdims=True)

    r_i = lax.broadcasted_iota(jnp.int32, (c_rows, c_rows), 0)
    c_i = lax.broadcasted_iota(jnp.int32, (c_rows, c_rows), 1)
    tri = (r_i >= c_i).astype(BF16)
    levels = []
    half = c_rows // 2
    while half >= 1:
        shift = half.bit_length()
        upper = (lax.shift_right_logical(r_i, shift - 1) & 1) == 1
        same = lax.shift_right_logical(r_i, shift) == lax.shift_right_logical(c_i, shift)
        levels.append((half, upper, jnp.where(upper, 1.0, -1.0), same))
        half //= 2

    for hh in range(n_heads):
        cols = slice(hh * HGRN_HEAD, (hh + 1) * HGRN_HEAD)
        lb = lb_all[:, cols]
        q = q_ref[:, cols]
        v = v_ref[:, cols]
        f = lb + (1.0 - lb) * jax.nn.sigmoid(f_ref[:, cols])
        k = 1.0 - f
        b = _cumsum_rows(tri, jnp.log2(f))
        b_last = b[c_rows - 1:c_rows, :]

        st = st_ref[hh]
        inter = lax.dot_general((q * jnp.exp2(b)).astype(BF16), st.astype(BF16),
                                (((1,), (1,)), ((), ())), preferred_element_type=F32)
        kt = (k * jnp.exp2(b_last - b)).astype(BF16)
        st_ref[hh] = st * jnp.exp2(b_last) + lax.dot_general(
            v.astype(BF16), kt, (((0,), (0,)), ((), ())), preferred_element_type=F32)

        att = jnp.zeros((c_rows, c_rows), F32)
        for half, upper, sign, same in levels:
            decay = jnp.exp2((b - _edge_rows(b, half)) * sign)
            scaled = jnp.where(upper, q, k) * decay
            qt = jnp.where(upper, scaled, 0.0).astype(BF16)
            kl = jnp.where(upper, 0.0, scaled).astype(BF16)
            pair = lax.dot_general(qt, kl, (((1,), (1,)), ((), ())), preferred_element_type=F32)
            att = att + jnp.where(same, pair, 0.0)
        o = inter + jnp.dot(att.astype(BF16), v.astype(BF16), preferred_element_type=F32)
        o = o + jnp.sum(q * k, axis=-1, keepdims=True) * v

        gate = gate_ref[:, cols]
        o_ref[:, cols] = (_rms(o, gn_ref[...]) * (gate * jax.nn.sigmoid(gate))).astype(BF16)


def _hgrn(p, lb_logits, gn, lb_row):
    m = p.shape[0]
    d_a = lb_logits.shape[1]
    hp = 8
    width = hp * HGRN_HEAD
    groups = d_a // width
    tb = HGRN_CHUNK
    n_l = lb_logits.shape[0]
    col = lambda off: (lambda g, t: (t, off + g))
    kern = functools.partial(_hgrn_kernel, n_heads=hp, lb_row=lb_row)
    return pl.pallas_call(
        kern,
        grid=(groups, m // tb),
        in_specs=[pl.BlockSpec((tb, width), col(0)),
                  pl.BlockSpec((tb, width), col(groups)),
                  pl.BlockSpec((tb, width), col(2 * groups)),
                  pl.BlockSpec((tb, width), col(3 * groups)),
                  pl.BlockSpec((n_l, width), lambda g, t: (0, g)),
                  pl.BlockSpec((1, HGRN_HEAD), lambda g, t: (0, 0))],
        out_specs=pl.BlockSpec((tb, width), lambda g, t: (t, g)),
        out_shape=jax.ShapeDtypeStruct((m, d_a), BF16),
        scratch_shapes=[pltpu.VMEM((hp, HGRN_HEAD, HGRN_HEAD), F32)],
        compiler_params=_params(("parallel", "arbitrary")),
        name="hgrn2",
    )(p, p, p, p, lb_logits, gn.reshape(1, HGRN_HEAD))


def _sconv_kernel(sx_ref, sb_ref, sc_ref, sxp_ref, scp_ref, w_ref, o_ref, ext_ref, *, taps):
    tm = sx_ref.shape[0]
    prev = sxp_ref[...] * scp_ref[...]
    ext_ref[0:8, :] = jnp.where(pl.program_id(0) > 0, prev, 0.0)
    ext_ref[8:8 + tm, :] = sx_ref[...] * sc_ref[...]
    y = jnp.zeros(sx_ref.shape, F32)
    for j in range(taps):
        s = 8 - (taps - 1) + j
        y = y + w_ref[j:j + 1, :] * ext_ref[s:s + tm, :]
    o_ref[...] = (sb_ref[...] * y).astype(BF16)


def _sconv(p, w, col0, width):
    m = p.shape[0]
    taps = w.shape[0]
    tm = _pick_tile(m, 640, 128)
    cb = col0 // width
    prev = lambda off: (lambda i: (jnp.maximum(i * (tm // 8) - 1, 0), off))
    return pl.pallas_call(
        functools.partial(_sconv_kernel, taps=taps),
        grid=(m // tm,),
        in_specs=[pl.BlockSpec((tm, width), lambda i: (i, cb)),
                  pl.BlockSpec((tm, width), lambda i: (i, cb + 1)),
                  pl.BlockSpec((tm, width), lambda i: (i, cb + 2)),
                  pl.BlockSpec((8, width), prev(cb)),
                  pl.BlockSpec((8, width), prev(cb + 2)),
                  pl.BlockSpec((taps, width), lambda i: (0, 0))],
        out_specs=pl.BlockSpec((tm, width), lambda i: (i, 0)),
        out_shape=jax.ShapeDtypeStruct((m, width), BF16),
        scratch_shapes=[pltpu.VMEM((tm + 8, width), F32)],
        compiler_params=_params(("parallel",)),
        name="sconv",
    )(p, p, p, p, p, w)


def _rglru_kernel(rx_ref, ry_ref, rxp_ref, cw_ref, cb_ref, wa_ref, ba_ref, wi_ref, bi_ref, lam_ref,
                  o_ref, ext_ref, a_ref, x_ref, hs_ref, h_ref, *, taps):
    tm, width = rx_ref.shape
    i = pl.program_id(0)

    @pl.when(i == 0)
    def _():
        h_ref[...] = jnp.zeros_like(h_ref)

    ext_ref[0:8, :] = jnp.where(i > 0, rxp_ref[...], 0.0)
    ext_ref[8:8 + tm, :] = rx_ref[...]
    u = jnp.zeros((tm, width), F32) + cb_ref[...]
    for j in range(taps):
        s = 8 - (taps - 1) + j
        u = u + cw_ref[j:j + 1, :] * ext_ref[s:s + tm, :]

    u_b = u.astype(BF16)
    r_parts, i_parts = [], []
    for n in range(width // RG_BLOCK):
        blk = slice(n * RG_BLOCK, (n + 1) * RG_BLOCK)
        r_parts.append(jnp.dot(u_b[:, blk], wa_ref[n], preferred_element_type=F32))
        i_parts.append(jnp.dot(u_b[:, blk], wi_ref[n], preferred_element_type=F32))
    r = jax.nn.sigmoid(jnp.concatenate(r_parts, axis=1) + ba_ref[...])
    ig = jax.nn.sigmoid(jnp.concatenate(i_parts, axis=1) + bi_ref[...])

    neg_lam = -lam_ref[...]
    softplus = jnp.maximum(neg_lam, 0.0) + jnp.log1p(jnp.exp(-jnp.abs(neg_lam)))
    log_a = -RG_C * r * softplus
    row = i * tm + lax.broadcasted_iota(jnp.int32, (tm, 1), 0)
    a = jnp.exp(log_a)
    xin = jnp.sqrt(1.0 - a * a) * (ig * u)
    a_ref[...] = a
    x_ref[...] = jnp.where(row >= PAD_ROWS, xin, 0.0)

    def group(gidx, h):
        base = pl.multiple_of(gidx * 8, 8)
        a8 = a_ref[pl.ds(base, 8), :]
        x8 = x_ref[pl.ds(base, 8), :]
        for rr in range(8):
            h = a8[rr:rr + 1, :] * h + x8[rr:rr + 1, :]
            hs_ref[pl.ds(base + rr, 1), :] = h
        return h

    h_ref[...] = lax.fori_loop(0, tm // 8, group, h_ref[...])
    o_ref[...] = (hs_ref[...] * jax.nn.gelu(ry_ref[...])).astype(BF16)


def _rglru(p, cw, cb, wa, ba, wi, bi, lam, width):
    m = p.shape[0]
    taps = cw.shape[0]
    tm = _pick_tile(m, 640, 128)
    nb = width // RG_BLOCK
    row = lambda v: v.reshape(1, width)
    full2 = lambda shape: pl.BlockSpec(shape, lambda i: (0, 0))
    full3 = lambda shape: pl.BlockSpec(shape, lambda i: (0, 0, 0))
    return pl.pallas_call(
        functools.partial(_rglru_kernel, taps=taps),
        grid=(m // tm,),
        in_specs=[pl.BlockSpec((tm, width), lambda i: (i, 0)),
                  pl.BlockSpec((tm, width), lambda i: (i, 1)),
                  pl.BlockSpec((8, width), lambda i: (jnp.maximum(i * (tm // 8) - 1, 0), 0)),
                  full2((taps, width)), full2((1, width)),
                  full3((nb, RG_BLOCK, RG_BLOCK)), full2((1, width)),
                  full3((nb, RG_BLOCK, RG_BLOCK)), full2((1, width)),
                  full2((1, width))],
        out_specs=pl.BlockSpec((tm, width), lambda i: (i, 0)),
        out_shape=jax.ShapeDtypeStruct((m, width), BF16),
        scratch_shapes=[pltpu.VMEM((tm + 8, width), F32),
                        pltpu.VMEM((tm, width), F32),
                        pltpu.VMEM((tm, width), F32),
                        pltpu.VMEM((tm, width), F32),
                        pltpu.VMEM((1, width), F32)],
        compiler_params=_params(("arbitrary",)),
        name="rglru",
    )(p, p, p, cw, row(cb), wa, row(ba), wi, row(bi), row(lam))


def _latent_kernel(c_ref, ct_ref, g_ref, gt_ref, o_ref, ot_ref):
    o_ref[...] = _rms(c_ref[...], g_ref[...]).astype(BF16)
    ct = ct_ref[...]
    ot_ref[0:KV_RANK, :] = (ct * lax.rsqrt(jnp.mean(ct * ct, axis=0, keepdims=True) + EPS)
                            * gt_ref[...]).astype(BF16)
    first = lax.broadcasted_iota(jnp.int32, (ONES_ROWS, ct.shape[1]), 0) == 0
    ot_ref[KV_RANK:KV_RANK + ONES_ROWS, :] = jnp.where(first, 1.0, 0.0).astype(BF16)


def _latent(p, pt, g, col0, row0):
    m = p.shape[0]
    r = g.shape[0]
    tm = _pick_tile(m, 640, 128)
    return pl.pallas_call(
        _latent_kernel,
        grid=(m // tm,),
        in_specs=[pl.BlockSpec((tm, r), lambda i: (i, col0 // r)),
                  pl.BlockSpec((r, tm), lambda i: (row0 // r, i)),
                  pl.BlockSpec((1, r), lambda i: (0, 0)),
                  pl.BlockSpec((r, 1), lambda i: (0, 0))],
        out_specs=[pl.BlockSpec((tm, r), lambda i: (i, 0)),
                   pl.BlockSpec((r + ONES_ROWS, tm), lambda i: (0, i))],
        out_shape=[jax.ShapeDtypeStruct((m, r), BF16), jax.ShapeDtypeStruct((r + ONES_ROWS, m), BF16)],
        compiler_params=_params(("parallel",)),
        name="latent_norm",
    )(p, pt, g.reshape(1, r), g.reshape(r, 1))


def _dsa_kernel(qt_ref, iqt_ref, iwt_ref, c_ref, ct_ref, ik_ref, wuk_ref, wuvt_ref, o_ref,
                sc_ref, iqp_ref, qlt_ref, acc_ref, m_ref, sa_ref, sb_ref, *, k_sel):
    i = pl.program_id(0)
    tq = Q_BLOCK
    n_heads = ATT_HEADS
    sub_blocks = KEY_CHUNK // tq
    n_chunks = (i * tq + KEY_CHUNK - 1) // KEY_CHUNK

    scale = ATT_HEAD_DIM ** -0.5 * LOG2_E
    for h in range(n_heads):
        qh = qt_ref[h * ATT_HEAD_DIM:(h + 1) * ATT_HEAD_DIM, :].astype(BF16)
        ql = jnp.dot(wuk_ref[h], qh, preferred_element_type=F32) * scale
        qlt_ref[:, h * tq:(h + 1) * tq] = ql.astype(BF16)

    iw = iwt_ref[...] * ((IDX_DIM ** -0.5) * (IDX_HEADS ** -0.5))
    k_loc = lax.broadcasted_iota(jnp.int32, (tq, tq), 0)
    q_loc = lax.broadcasted_iota(jnp.int32, (tq, tq), 1)

    for hp in range(IDX_HEADS // 2):
        pair = jnp.concatenate(
            [iqt_ref[(2 * hp) * IDX_DIM:(2 * hp + 1) * IDX_DIM, :],
             iqt_ref[(2 * hp + 1) * IDX_DIM:(2 * hp + 2) * IDX_DIM, :]], axis=1).astype(BF16)
        iqp_ref[hp, 0:IDX_DIM, :] = pair
        iqp_ref[hp, IDX_DIM:2 * IDX_DIM, :] = jnp.zeros_like(pair)

    def score_chunk(j, carry):
        mn, mx = carry
        for u in range(sub_blocks):
            kb = j * sub_blocks + 1 + u
            r0 = pl.multiple_of(kb * tq, tq)
            ikb = ik_ref[pl.ds(r0, tq), :]
            s = jnp.zeros((tq, tq), F32)
            for hp in range(IDX_HEADS // 2):
                x = jnp.dot(ikb, iqp_ref[hp], preferred_element_type=F32)
                s = s + jnp.maximum(x[:, 0:tq], 0.0) * iw[2 * hp:2 * hp + 1, :]
                s = s + jnp.maximum(x[:, tq:2 * tq], 0.0) * iw[2 * hp + 1:2 * hp + 2, :]
            visible = jnp.logical_or(kb < i, jnp.logical_and(kb == i, k_loc <= q_loc))
            sc_ref[pl.ds(r0, tq), :] = jnp.where(visible, s, -jnp.inf)
            mx = jnp.maximum(mx, jnp.max(jnp.where(visible, s, -jnp.inf).reshape(tq // 8, 8, tq), axis=0))
            mn = jnp.minimum(mn, jnp.min(jnp.where(visible, s, jnp.inf).reshape(tq // 8, 8, tq), axis=0))
        return mn, mx

    mn, mx = lax.fori_loop(0, n_chunks, score_chunk,
                           (jnp.full((8, tq), jnp.inf, F32), jnp.full((8, tq), -jnp.inf, F32)))
    row_min = jnp.min(mn, axis=0, keepdims=True)
    row_max = jnp.max(mx, axis=0, keepdims=True)

    groups = KEY_CHUNK // 64

    def chunk_scores(j):
        r0 = pl.multiple_of(FRONT + j * KEY_CHUNK, FRONT)
        return sc_ref[pl.ds(r0, KEY_CHUNK), :].reshape(groups, 8, 8, tq)

    def count_ge(t):
        def body(j, cnt):
            kk = chunk_scores(j)
            for g in range(groups):
                cnt = jnp.where(kk[g] >= t, cnt + 1, cnt)
            return cnt
        cnt = lax.fori_loop(0, n_chunks, body, jnp.zeros((8, 8, tq), jnp.int32))
        return jnp.sum(jnp.sum(cnt, axis=0), axis=0, keepdims=True)

    lane = lax.broadcasted_iota(jnp.int32, (1, tq), 1)
    n_visible = (i - 1) * tq + lane + 1
    k_row = jnp.minimum(k_sel, n_visible)

    def probe(mid, movable, lo, hi, cnt_lo):
        c = count_ge(mid)
        up = jnp.logical_and(c >= k_row, movable)
        down = jnp.logical_and(jnp.logical_not(up), movable)
        return jnp.where(up, mid, lo), jnp.where(down, mid, hi), jnp.where(up, c, cnt_lo)

    def midpoint(lo, hi, cnt_lo):
        mid = lo + 0.5 * (hi - lo)
        movable = jnp.logical_and(cnt_lo != k_row, jnp.logical_and(mid > lo, mid < hi))
        return mid, movable

    def any_lane(flag):
        return jnp.max(flag.astype(jnp.int32))

    lo, hi, cnt_lo = probe(row_max, n_visible > k_row, row_min, row_max, n_visible)

    def bisect_cond(state):
        return jnp.logical_and(state[0] > 0, state[1] < BISECT_CAP)

    def bisect_body(state):
        _, it, lo, hi, cnt_lo = state
        for _ in range(BISECT_STEPS):
            mid, movable = midpoint(lo, hi, cnt_lo)
            lo, hi, cnt_lo = probe(mid, movable, lo, hi, cnt_lo)
        return any_lane(midpoint(lo, hi, cnt_lo)[1]), it + 1, lo, hi, cnt_lo

    state = (any_lane(midpoint(lo, hi, cnt_lo)[1]), jnp.int32(0), lo, hi, cnt_lo)
    _, _, thr, _, cnt_thr = lax.while_loop(bisect_cond, bisect_body, state)

    tied = cnt_thr > k_row

    @pl.when(any_lane(tied) > 0)
    def _():
        key_idx = lax.broadcasted_iota(jnp.int32, (groups, 8, 8, tq), 0) * 64 + (
            lax.broadcasted_iota(jnp.int32, (groups, 8, 8, tq), 1) * 8
            + lax.broadcasted_iota(jnp.int32, (groups, 8, 8, tq), 2))

        def count_where(pred):
            def body(j, cnt):
                hit = pred(chunk_scores(j), j * KEY_CHUNK + key_idx)
                return cnt + jnp.sum(hit.astype(jnp.int32), axis=0)
            cnt = lax.fori_loop(0, n_chunks, body, jnp.zeros((8, 8, tq), jnp.int32))
            return jnp.sum(jnp.sum(cnt, axis=0), axis=0, keepdims=True)

        wanted = k_row - count_where(lambda kk, idx: kk > thr)

        def index_step(_, bounds):
            below, cap = bounds
            mid = below + lax.shift_right_logical(cap - below, 1)
            enough = count_where(lambda kk, idx: jnp.logical_and(kk == thr, idx <= mid)) >= wanted
            return jnp.where(enough, below, mid), jnp.where(enough, mid, cap)

        n_keys = n_chunks * KEY_CHUNK
        steps = max(1, (sc_ref.shape[0] - 1).bit_length())
        _, cap = lax.fori_loop(0, steps, index_step,
                               (jnp.full((1, tq), -1, jnp.int32), jnp.full((1, tq), 1, jnp.int32) * (n_keys - 1)))

        def drop_chunk(j, carry):
            r0 = pl.multiple_of(FRONT + j * KEY_CHUNK, FRONT)
            kk = chunk_scores(j)
            extra = jnp.logical_and(jnp.logical_and(kk == thr, j * KEY_CHUNK + key_idx > cap), tied)
            sc_ref[pl.ds(r0, KEY_CHUNK), :] = jnp.where(extra, -jnp.inf, kk).reshape(KEY_CHUNK, tq)
            return carry

        lax.fori_loop(0, n_chunks, drop_chunk, 0)

    qlt = qlt_ref[...]
    head_cols = [slice(h * tq, (h + 1) * tq) for h in range(n_heads)]

    def masked_scores(rows, bias, s_ref):
        bias2 = jnp.concatenate([bias, bias], axis=1)
        cmax = []
        for hp in range(n_heads // 2):
            cols = slice(2 * hp * tq, (2 * hp + 2) * tq)
            sm = jnp.dot(c_ref[rows, :], qlt[:, cols], preferred_element_type=F32) + bias2
            s_ref[:, cols] = sm
            cmax.append(jnp.max(sm, axis=0, keepdims=True))
        return jnp.concatenate(cmax, axis=1)

    def accumulate(rows, s_ref, cmax):
        m_old = m_ref[...]
        m_new = jnp.maximum(m_old, cmax)
        m_ref[...] = m_new
        p = jnp.exp2(s_ref[...] - m_new).astype(BF16)
        acc_ref[...] = jnp.exp2(m_old - m_new) * acc_ref[...] + jnp.dot(
            ct_ref[:, rows], p, preferred_element_type=F32)

    m_ref[...] = jnp.full(m_ref.shape, NEG_BIG, F32)
    acc_ref[...] = jnp.zeros_like(acc_ref)

    q_row = i * tq + q_loc
    allowed0 = jnp.logical_and(jnp.logical_or(k_loc >= PAD_ROWS, k_loc == q_row), k_loc <= q_row)
    s0_ref = sa_ref.at[0:FRONT, :]
    cmax0 = masked_scores(slice(0, FRONT), jnp.where(allowed0, 0.0, NEG_BIG), s0_ref)
    accumulate(slice(0, FRONT), s0_ref, cmax0)

    def chunk_rows(j):
        jc = jnp.clip(j, 0, n_chunks - 1)
        return pl.ds(pl.multiple_of(FRONT + jc * KEY_CHUNK, FRONT), KEY_CHUNK)

    def chunk_scores_masked(j, s_ref):
        rows = chunk_rows(j)
        ok = jnp.logical_and(sc_ref[rows, :] >= thr, j < n_chunks)
        return masked_scores(rows, jnp.where(ok, 0.0, NEG_BIG), s_ref)

    @pl.when(i > 0)
    def _():
        def chunk_pair(t, cmax_a):
            j = 2 * t
            cmax_b = chunk_scores_masked(j + 1, sb_ref)
            accumulate(chunk_rows(j), sa_ref, cmax_a)
            cmax_a = chunk_scores_masked(j + 2, sa_ref)
            accumulate(chunk_rows(j + 1), sb_ref, cmax_b)
            return cmax_a

        lax.fori_loop(0, (n_chunks + 1) // 2, chunk_pair, chunk_scores_masked(0, sa_ref))

    o_lat = (acc_ref[0:KV_RANK, :] / acc_ref[KV_RANK:KV_RANK + 1, :]).astype(BF16)
    for h in range(n_heads):
        oh = jnp.dot(wuvt_ref[h], o_lat[:, head_cols[h]], preferred_element_type=F32)
        o_ref[:, h * ATT_HEAD_DIM:(h + 1) * ATT_HEAD_DIM] = oh.T.astype(BF16)


def _dsa(pt, c, ct, ik, wuk, wuvt, k_sel, q_row0, iq_row0, iw_row0):
    m = c.shape[0]
    tq = Q_BLOCK
    d_q = ATT_HEADS * ATT_HEAD_DIM
    d_iq = IDX_HEADS * IDX_DIM
    full2 = lambda shape: pl.BlockSpec(shape, lambda i: (0, 0))
    full3 = lambda shape: pl.BlockSpec(shape, lambda i: (0, 0, 0))
    return pl.pallas_call(
        functools.partial(_dsa_kernel, k_sel=k_sel),
        grid=(m // tq,),
        in_specs=[pl.BlockSpec((d_q, tq), lambda i: (q_row0 // d_q, i)),
                  pl.BlockSpec((d_iq, tq), lambda i: (iq_row0 // d_iq, i)),
                  pl.BlockSpec((IDX_HEADS, tq), lambda i: (iw_row0 // IDX_HEADS, i)),
                  full2(c.shape), full2(ct.shape), full2(ik.shape),
                  full3(wuk.shape), full3(wuvt.shape)],
        out_specs=pl.BlockSpec((tq, d_q), lambda i: (i, 0)),
        out_shape=jax.ShapeDtypeStruct((m, d_q), BF16),
        scratch_shapes=[pltpu.VMEM((m, tq), F32),
                        pltpu.VMEM((IDX_HEADS // 2, 2 * IDX_DIM, 2 * tq), BF16),
                        pltpu.VMEM((KV_RANK, ATT_HEADS * tq), BF16),
                        pltpu.VMEM((KV_RANK + ONES_ROWS, ATT_HEADS * tq), F32),
                        pltpu.VMEM((1, ATT_HEADS * tq), F32),
                        pltpu.VMEM((KEY_CHUNK, ATT_HEADS * tq), F32),
                        pltpu.VMEM((KEY_CHUNK, ATT_HEADS * tq), F32)],
        compiler_params=_params(("arbitrary",)),
        name="dsa",
    )(pt, pt, pt, c, ct, ik, wuk, wuvt)


def _pad_cols(w, n):
    return jnp.pad(w, ((0, 0), (0, n - w.shape[1])))


def kernel(x, meta_tokens, ln_mix_pre, ln_mix_post, ln_ffn_pre, ln_ffn_post, ffn_w1, ffn_w3, ffn_w2,
           ab_w_in, ab_w_out, hgrn_lb_logits, hgrn_out_norm, sconv_w,
           cd_w_in, cd_w_out, rg_conv_w, rg_conv_b, rg_w_a, rg_b_a, rg_w_i, rg_b_i, rg_lambda,
           mla_kv_norm, mla_w_uk, mla_w_uv):
    assert x.shape[0] == 1
    seq, d = x.shape[1], x.shape[2]
    assert seq % KEY_CHUNK == 0
    d_a = hgrn_lb_logits.shape[1]
    d_b = sconv_w.shape[2]
    d_c = rg_lambda.shape[1]
    d_d = ATT_HEADS * ATT_HEAD_DIM
    d_iq = IDX_HEADS * IDX_DIM
    k_sel = min(TOPK_MAX, seq // 4)

    x2 = x[0]
    front = jnp.concatenate([jnp.zeros((PAD_ROWS, d), F32), meta_tokens.astype(F32)], axis=0)
    w1, w3, w2 = ffn_w1.astype(BF16), ffn_w3.astype(BF16), ffn_w2.astype(BF16)

    p0 = _norm_mm(x2, ln_mix_pre[0], ab_w_in[0].astype(BF16), tn=512, front=front)
    og = _hgrn(p0, hgrn_lb_logits, hgrn_out_norm[0], lb_row=0)
    yb = _sconv(p0, sconv_w[0], col0=4 * d_a, width=d_b)
    w_out = ab_w_out[0].astype(BF16)
    h = _out_proj(og, yb, w_out[:d_a], w_out[d_a:], x2, ln_mix_post[0], front=front)
    h = _ffn(h, ln_ffn_pre[0], w1, w3, w2, ln_ffn_post[0], layer=0)

    w_in = cd_w_in[0]
    o_rx, o_ry, o_q, o_c = 0, d_c, 2 * d_c, 2 * d_c + d_d
    o_iq = o_c + KV_RANK
    o_ik = o_iq + d_iq
    o_iw = o_ik + IDX_DIM
    w_rows = jnp.concatenate([w_in[:, o_rx:o_q], w_in[:, o_c:o_iq], w_in[:, o_ik:o_iw]], axis=1)
    w_rows = _pad_cols(w_rows, -(-w_rows.shape[1] // 512) * 512).astype(BF16)
    w_cols = jnp.concatenate([w_in[:, o_q:o_c], w_in[:, o_iq:o_ik], w_in[:, o_c:o_iq], w_in[:, o_iw:]], axis=1)
    w_cols = _pad_cols(w_cols, -(-w_cols.shape[1] // 512) * 512).astype(BF16).T
    p1 = _norm_mm(h, ln_mix_pre[1], w_rows, tn=w_rows.shape[1] // 4)
    p1t = _norm_mm_t(h, ln_mix_pre[1], w_cols, tc=w_cols.shape[0] // 4)

    hc = _rglru(p1, rg_conv_w[0], rg_conv_b[0], rg_w_a[0].astype(BF16), rg_b_a[0],
                rg_w_i[0].astype(BF16), rg_b_i[0], rg_lambda[0], width=d_c)
    c, ct = _latent(p1, p1t, mla_kv_norm[0], col0=2 * d_c, row0=d_d + d_iq)
    ik = p1[:, 2 * d_c + KV_RANK:2 * d_c + KV_RANK + 2 * IDX_DIM].astype(BF16)
    wuk = jnp.transpose(mla_w_uk[0], (1, 0, 2)).astype(BF16)
    wuvt = jnp.transpose(mla_w_uv[0], (1, 2, 0)).astype(BF16)
    att = _dsa(p1t, c, ct, ik, wuk, wuvt, k_sel,
               q_row0=0, iq_row0=d_d, iw_row0=d_d + d_iq + KV_RANK)
    w_out = cd_w_out[0].astype(BF16)
    h = _out_proj(hc, att, w_out[:d_c], w_out[d_c:], h, ln_mix_post[1])
    return _ffn(h, ln_ffn_pre[1], w1, w3, w2, ln_ffn_post[1], layer=1, skip_front=True)[None]
```

```python
import functools
import math

import jax
import jax.numpy as jnp
from jax import lax
from jax.experimental import pallas as pl
from jax.experimental.pallas import tpu as pltpu

F32 = jnp.float32
BF16 = jnp.bfloat16

EPS = 1e-6
N_META = 16
FRONT = 128
PAD_ROWS = FRONT - N_META
HGRN_HEAD = 128
HGRN_CHUNK = 128
RG_BLOCK = 128
RG_C = 8.0
ATT_HEADS = 8
ATT_HEAD_DIM = 128
KV_RANK = 256
IDX_HEADS = 16
IDX_DIM = 64
TOPK_MAX = 256
Q_BLOCK = 128
KEY_CHUNK = 512
ONES_ROWS = 16
LOG2_E = 1.4426950408889634
BISECT_STEPS = 4
BISECT_CAP = 64
NEG_BIG = -1e30
VMEM_LIMIT = 56 * 1024 * 1024
PROJ_ROWS = 1664


def _pick_tile(n, target, mult):
    best = None
    for t in range(mult, min(n, target) + 1, mult):
        if n % t == 0:
            best = t
    assert best is not None, (n, target, mult)
    return best


def _params(sem):
    return pltpu.CompilerParams(dimension_semantics=sem, vmem_limit_bytes=VMEM_LIMIT)


def _rms(x, g):
    return x * lax.rsqrt(jnp.mean(x * x, axis=-1, keepdims=True) + EPS) * g


def _token_specs(tm, d):
    align = math.gcd(tm, FRONT)
    x_map = lambda *a: (pl.multiple_of(jnp.maximum(a[0] * tm - FRONT, 0), align), 0)
    return [pl.BlockSpec((pl.Element(tm), pl.Element(d)), x_map),
            pl.BlockSpec((FRONT, d), lambda *a: (0, 0))]


def _fill_normed(x_ref, front_ref, g_ref, xn_ref):
    g = g_ref[...]
    first = 0 if front_ref is None else (pl.program_id(0) == 0).astype(jnp.int32)

    def piece(p, carry):
        src = pl.multiple_of(jnp.maximum(p - first, 0) * FRONT, FRONT)
        rows = x_ref[pl.ds(src, FRONT), :]
        if front_ref is not None:
            rows = jnp.where(jnp.logical_and(first == 1, p == 0), front_ref[...], rows)
        xn_ref[pl.ds(pl.multiple_of(p * FRONT, FRONT), FRONT), :] = _rms(rows, g).astype(BF16)
        return carry

    lax.fori_loop(0, xn_ref.shape[0] // FRONT, piece, 0)


def _norm_mm_kernel(x_ref, *rest, tokens):
    front_ref = rest[0] if tokens else None
    g_ref, w_ref, o_ref, xn_ref = rest[-4:]

    @pl.when(pl.program_id(1) == 0)
    def _():
        _fill_normed(x_ref, front_ref, g_ref, xn_ref)

    o_ref[...] = jnp.dot(xn_ref[...], w_ref[...], preferred_element_type=F32)


def _norm_mm(x, g, w, tn=512, front=None):
    d = x.shape[1]
    m = x.shape[0] + (0 if front is None else FRONT)
    n = w.shape[1]
    tm = _pick_tile(m, PROJ_ROWS, FRONT)
    if front is None:
        row_specs, rows = [pl.BlockSpec((tm, d), lambda i, j: (i, 0))], (x,)
    else:
        row_specs, rows = _token_specs(tm, d), (x, front)
    return pl.pallas_call(
        functools.partial(_norm_mm_kernel, tokens=front is not None),
        grid=(m // tm, n // tn),
        in_specs=row_specs + [pl.BlockSpec((1, d), lambda i, j: (0, 0)),
                              pl.BlockSpec((d, tn), lambda i, j: (0, j))],
        out_specs=pl.BlockSpec((tm, tn), lambda i, j: (i, j)),
        out_shape=jax.ShapeDtypeStruct((m, n), F32),
        scratch_shapes=[pltpu.VMEM((tm, d), BF16)],
        compiler_params=_params(("parallel", "arbitrary")),
        name="norm_proj",
    )(*rows, g.reshape(1, d), w)


def _norm_mm_t_kernel(x_ref, g_ref, wt_ref, o_ref, xn_ref):
    @pl.when(pl.program_id(1) == 0)
    def _():
        _fill_normed(x_ref, None, g_ref, xn_ref)

    o_ref[...] = lax.dot_general(wt_ref[...], xn_ref[...], (((1,), (1,)), ((), ())),
                                 preferred_element_type=F32)


def _norm_mm_t(x, g, wt, tc=512):
    m, d = x.shape
    n = wt.shape[0]
    tm = _pick_tile(m, PROJ_ROWS, FRONT)
    return pl.pallas_call(
        _norm_mm_t_kernel,
        grid=(m // tm, n // tc),
        in_specs=[pl.BlockSpec((tm, d), lambda i, j: (i, 0)),
                  pl.BlockSpec((1, d), lambda i, j: (0, 0)),
                  pl.BlockSpec((tc, d), lambda i, j: (j, 0))],
        out_specs=pl.BlockSpec((tc, tm), lambda i, j: (j, i)),
        out_shape=jax.ShapeDtypeStruct((n, m), F32),
        scratch_shapes=[pltpu.VMEM((tm, d), BF16)],
        compiler_params=_params(("parallel", "arbitrary")),
        name="norm_proj_t",
    )(x, g.reshape(1, d), wt)


def _out_proj_kernel(a_ref, b_ref, wa_ref, wb_ref, g_ref, h_ref, o_ref):
    y = jnp.dot(a_ref[...], wa_ref[...], preferred_element_type=F32)
    y = y + jnp.dot(b_ref[...], wb_ref[...], preferred_element_type=F32)
    o_ref[...] = h_ref[...] + _rms(y, g_ref[...])


def _out_proj(a, b, wa, wb, h, g):
    m, ka = a.shape
    kb = b.shape[1]
    d = wa.shape[1]
    tm = _pick_tile(m, 640, 128)
    return pl.pallas_call(
        _out_proj_kernel,
        grid=(m // tm,),
        in_specs=[pl.BlockSpec((tm, ka), lambda i: (i, 0)),
                  pl.BlockSpec((tm, kb), lambda i: (i, 0)),
                  pl.BlockSpec((ka, d), lambda i: (0, 0)),
                  pl.BlockSpec((kb, d), lambda i: (0, 0)),
                  pl.BlockSpec((1, d), lambda i: (0, 0)),
                  pl.BlockSpec((tm, d), lambda i: (i, 0))],
        out_specs=pl.BlockSpec((tm, d), lambda i: (i, 0)),
        out_shape=jax.ShapeDtypeStruct((m, d), F32),
        compiler_params=_params(("parallel",)),
        name="out_proj",
    )(a, b, wa, wb, g.reshape(1, d), h)


def _out_proj_conv_kernel(a_ref, sx_ref, sb_ref, sc_ref, sxp_ref, scp_ref, cw_ref, wa_ref, wb_ref, g_ref,
                          x_ref, front_ref, o_ref, ext_ref, *, taps):
    tm = a_ref.shape[0]
    i = pl.program_id(0)
    ext_ref[0:8, :] = jnp.where(i > 0, sxp_ref[...] * scp_ref[...], 0.0)
    ext_ref[8:8 + tm, :] = sx_ref[...] * sc_ref[...]
    conv = jnp.zeros(sx_ref.shape, F32)
    for j in range(taps):
        s = 8 - (taps - 1) + j
        conv = conv + cw_ref[j:j + 1, :] * ext_ref[s:s + tm, :]
    yb = (sb_ref[...] * conv).astype(BF16)

    y = jnp.dot(a_ref[...], wa_ref[...], preferred_element_type=F32)
    y = _rms(y + jnp.dot(yb, wb_ref[...], preferred_element_type=F32), g_ref[...])

    @pl.when(i == 0)
    def _():
        o_ref[0:FRONT, :] = front_ref[...] + y[0:FRONT, :]
        o_ref[FRONT:tm, :] = x_ref[0:tm - FRONT, :] + y[FRONT:tm, :]

    @pl.when(i > 0)
    def _():
        o_ref[...] = x_ref[...] + y


def _out_proj_conv(a, p, cw, col0, wa, wb, x, front, g):
    m, ka = a.shape
    taps, width = cw.shape
    d = wa.shape[1]
    tm = _pick_tile(m, 320, 64)
    cb = col0 // width
    prev = lambda off: (lambda i: (jnp.maximum(i * (tm // 8) - 1, 0), off))
    return pl.pallas_call(
        functools.partial(_out_proj_conv_kernel, taps=taps),
        grid=(m // tm,),
        in_specs=[pl.BlockSpec((tm, ka), lambda i: (i, 0)),
                  pl.BlockSpec((tm, width), lambda i: (i, cb)),
                  pl.BlockSpec((tm, width), lambda i: (i, cb + 1)),
                  pl.BlockSpec((tm, width), lambda i: (i, cb + 2)),
                  pl.BlockSpec((8, width), prev(cb)),
                  pl.BlockSpec((8, width), prev(cb + 2)),
                  pl.BlockSpec((taps, width), lambda i: (0, 0)),
                  pl.BlockSpec((ka, d), lambda i: (0, 0)),
                  pl.BlockSpec((width, d), lambda i: (0, 0)),
                  pl.BlockSpec((1, d), lambda i: (0, 0))] + _token_specs(tm, d),
        out_specs=pl.BlockSpec((tm, d), lambda i: (i, 0)),
        out_shape=jax.ShapeDtypeStruct((m, d), F32),
        scratch_shapes=[pltpu.VMEM((tm + 8, width), F32)],
        compiler_params=_params(("parallel",)),
        name="out_proj_conv",
    )(a, p, p, p, p, p, cw, wa, wb, g.reshape(1, d), x, front)


def _ffn_kernel(h_ref, gpre_ref, w1_ref, w3_ref, w2_ref, gpost_ref, o_ref, xn_ref, acc_ref):
    j = pl.program_id(1)

    @pl.when(j == 0)
    def _():
        xn_ref[...] = _rms(h_ref[...], gpre_ref[...]).astype(BF16)
        acc_ref[...] = jnp.zeros_like(acc_ref)

    xn = xn_ref[...]
    a = jnp.dot(xn, w1_ref[...], preferred_element_type=F32)
    b = jnp.dot(xn, w3_ref[...], preferred_element_type=F32)
    u = (a * jax.nn.sigmoid(a) * b).astype(BF16)
    acc_ref[...] += jnp.dot(u, w2_ref[...], preferred_element_type=F32)

    @pl.when(j == pl.num_programs(1) - 1)
    def _():
        o_ref[...] = h_ref[...] + _rms(acc_ref[...], gpost_ref[...])


def _ffn(h, gpre, w1, w3, w2, gpost, layer, tf=512, skip_front=False):
    d = h.shape[1]
    f = w1.shape[2]
    if skip_front:
        m = h.shape[0] - FRONT
        tm = _pick_tile(m, 640, 128)
        h_spec = pl.BlockSpec((pl.Element(tm), pl.Element(d)), lambda i, j: (pl.multiple_of(FRONT + i * tm, FRONT), 0))
    else:
        m = h.shape[0]
        tm = _pick_tile(m, 640, 128)
        h_spec = pl.BlockSpec((tm, d), lambda i, j: (i, 0))
    return pl.pallas_call(
        _ffn_kernel,
        grid=(m // tm, f // tf),
        in_specs=[h_spec,
                  pl.BlockSpec((1, d), lambda i, j: (0, 0)),
                  pl.BlockSpec((None, d, tf), lambda i, j: (layer, 0, j)),
                  pl.BlockSpec((None, d, tf), lambda i, j: (layer, 0, j)),
                  pl.BlockSpec((None, tf, d), lambda i, j: (layer, j, 0)),
                  pl.BlockSpec((1, d), lambda i, j: (0, 0))],
        out_specs=pl.BlockSpec((tm, d), lambda i, j: (i, 0)),
        out_shape=jax.ShapeDtypeStruct((m, d), F32),
        scratch_shapes=[pltpu.VMEM((tm, d), BF16), pltpu.VMEM((tm, d), F32)],
        compiler_params=_params(("parallel", "arbitrary")),
        name="ffn",
    )(h, gpre.reshape(1, d), w1, w3, w2, gpost.reshape(1, d))


def _cumsum_rows(tri, x):
    hi = x.astype(BF16)
    rest = x - hi.astype(F32)
    mid = rest.astype(BF16)
    lo = (rest - mid.astype(F32)).astype(BF16)
    return (jnp.dot(tri, hi, preferred_element_type=F32) + jnp.dot(tri, mid, preferred_element_type=F32)
            + jnp.dot(tri, lo, preferred_element_type=F32))


def _edge_rows(b, half):
    rows = b.shape[0]
    if half >= 8:
        parts = [jnp.broadcast_to(b[e:e + 1, :], (2 * half, b.shape[1]))
                 for e in range(half - 1, rows, 2 * half)]
        return parts[0] if len(parts) == 1 else jnp.concatenate(parts, axis=0)
    b3 = b.reshape(rows // 8, 8, b.shape[1])
    sub = lax.broadcasted_iota(jnp.int32, b3.shape, 1)
    pick = lambda r: jnp.broadcast_to(b3[:, r:r + 1, :], b3.shape)
    edge = pick(half - 1)
    for start in range(2 * half, 8, 2 * half):
        edge = jnp.where(sub >= start, pick(start + half - 1), edge)
    return edge.reshape(b.shape)


def _hgrn_kernel(q_ref, f_ref, v_ref, gate_ref, lbl_ref, gn_ref, o_ref, st_ref, *, n_heads, lb_row):
    c_rows = HGRN_CHUNK

    @pl.when(pl.program_id(1) == 0)
    def _():
        st_ref[...] = jnp.zeros_like(st_ref)

    logits = lbl_ref[...]
    ex = jnp.exp(logits - jnp.max(logits, axis=0, keepdims=True))
    lb_all = jnp.sum(ex[0:lb_row + 1, :], axis=0, keepdims=True) / jnp.sum(ex, axis=0, keepdims=True)

    r_i = lax.broadcasted_iota(jnp.int32, (c_rows, c_rows), 0)
    c_i = lax.broadcasted_iota(jnp.int32, (c_rows, c_rows), 1)
    tri = (r_i >= c_i).astype(BF16)
    levels = []
    half = c_rows // 2
    while half >= 1:
        shift = half.bit_length()
        upper = (lax.shift_right_logical(r_i, shift - 1) & 1) == 1
        same = lax.shift_right_logical(r_i, shift) == lax.shift_right_logical(c_i, shift)
        levels.append((half, upper, jnp.where(upper, 1.0, -1.0), same))
        half //= 2

    for hh in range(n_heads):
        cols = slice(hh * HGRN_HEAD, (hh + 1) * HGRN_HEAD)
        lb = lb_all[:, cols]
        q = q_ref[:, cols]
        v = v_ref[:, cols]
        f = lb + (1.0 - lb) * jax.nn.sigmoid(f_ref[:, cols])
        k = 1.0 - f
        b = _cumsum_rows(tri, jnp.log2(f))
        b_last = b[c_rows - 1:c_rows, :]

        st = st_ref[hh]
        inter = lax.dot_general((q * jnp.exp2(b)).astype(BF16), st.astype(BF16),
                                (((1,), (1,)), ((), ())), preferred_element_type=F32)
        kt = (k * jnp.exp2(b_last - b)).astype(BF16)
        st_ref[hh] = st * jnp.exp2(b_last) + lax.dot_general(
            v.astype(BF16), kt, (((0,), (0,)), ((), ())), preferred_element_type=F32)

        att = jnp.zeros((c_rows, c_rows), F32)
        for half, upper, sign, same in levels:
            decay = jnp.exp2((b - _edge_rows(b, half)) * sign)
            scaled = jnp.where(upper, q, k) * decay
            qt = jnp.where(upper, scaled, 0.0).astype(BF16)
            kl = jnp.where(upper, 0.0, scaled).astype(BF16)
            pair = lax.dot_general(qt, kl, (((1,), (1,)), ((), ())), preferred_element_type=F32)
            att = att + jnp.where(same, pair, 0.0)
        o = inter + jnp.dot(att.astype(BF16), v.astype(BF16), preferred_element_type=F32)
        o = o + jnp.sum(q * k, axis=-1, keepdims=True) * v

        gate = gate_ref[:, cols]
        o_ref[:, cols] = (_rms(o, gn_ref[...]) * (gate * jax.nn.sigmoid(gate))).astype(BF16)


def _hgrn(p, lb_logits, gn, lb_row):
    m = p.shape[0]
    d_a = lb_logits.shape[1]
    hp = 8
    width = hp * HGRN_HEAD
    groups = d_a // width
    tb = HGRN_CHUNK
    n_l = lb_logits.shape[0]
    col = lambda off: (lambda g, t: (t, off + g))
    kern = functools.partial(_hgrn_kernel, n_heads=hp, lb_row=lb_row)
    return pl.pallas_call(
        kern,
        grid=(groups, m // tb),
        in_specs=[pl.BlockSpec((tb, width), col(0)),
                  pl.BlockSpec((tb, width), col(groups)),
                  pl.BlockSpec((tb, width), col(2 * groups)),
                  pl.BlockSpec((tb, width), col(3 * groups)),
                  pl.BlockSpec((n_l, width), lambda g, t: (0, g)),
                  pl.BlockSpec((1, HGRN_HEAD), lambda g, t: (0, 0))],
        out_specs=pl.BlockSpec((tb, width), lambda g, t: (t, g)),
        out_shape=jax.ShapeDtypeStruct((m, d_a), BF16),
        scratch_shapes=[pltpu.VMEM((hp, HGRN_HEAD, HGRN_HEAD), F32)],
        compiler_params=_params(("parallel", "arbitrary")),
        name="hgrn2",
    )(p, p, p, p, lb_logits, gn.reshape(1, HGRN_HEAD))


def _rglru_kernel(rx_ref, ry_ref, rxp_ref, cw_ref, cb_ref, wa_ref, ba_ref, wi_ref, bi_ref, lam_ref,
                  o_ref, ext_ref, a_ref, x_ref, hs_ref, h_ref, *, taps):
    tm, width = rx_ref.shape
    i = pl.program_id(0)

    @pl.when(i == 0)
    def _():
        h_ref[...] = jnp.zeros_like(h_ref)

    ext_ref[0:8, :] = jnp.where(i > 0, rxp_ref[...], 0.0)
    ext_ref[8:8 + tm, :] = rx_ref[...]
    u = jnp.zeros((tm, width), F32) + cb_ref[...]
    for j in range(taps):
        s = 8 - (taps - 1) + j
        u = u + cw_ref[j:j + 1, :] * ext_ref[s:s + tm, :]

    u_b = u.astype(BF16)
    r_parts, i_parts = [], []
    for n in range(width // RG_BLOCK):
        blk = slice(n * RG_BLOCK, (n + 1) * RG_BLOCK)
        r_parts.append(jnp.dot(u_b[:, blk], wa_ref[n], preferred_element_type=F32))
        i_parts.append(jnp.dot(u_b[:, blk], wi_ref[n], preferred_element_type=F32))
    r = jax.nn.sigmoid(jnp.concatenate(r_parts, axis=1) + ba_ref[...])
    ig = jax.nn.sigmoid(jnp.concatenate(i_parts, axis=1) + bi_ref[...])

    neg_lam = -lam_ref[...]
    softplus = jnp.maximum(neg_lam, 0.0) + jnp.log1p(jnp.exp(-jnp.abs(neg_lam)))
    log_a = -RG_C * r * softplus
    row = i * tm + lax.broadcasted_iota(jnp.int32, (tm, 1), 0)
    a = jnp.exp(log_a)
    xin = jnp.sqrt(1.0 - a * a) * (ig * u)
    a_ref[...] = a
    x_ref[...] = jnp.where(row >= PAD_ROWS, xin, 0.0)

    def group(gidx, h):
        base = pl.multiple_of(gidx * 8, 8)
        a8 = a_ref[pl.ds(base, 8), :]
        x8 = x_ref[pl.ds(base, 8), :]
        for rr in range(8):
            h = a8[rr:rr + 1, :] * h + x8[rr:rr + 1, :]
            hs_ref[pl.ds(base + rr, 1), :] = h
        return h

    h_ref[...] = lax.fori_loop(0, tm // 8, group, h_ref[...])
    o_ref[...] = (hs_ref[...] * jax.nn.gelu(ry_ref[...])).astype(BF16)


def _rglru(p, cw, cb, wa, ba, wi, bi, lam, width):
    m = p.shape[0]
    taps = cw.shape[0]
    tm = _pick_tile(m, 640, 128)
    nb = width // RG_BLOCK
    row = lambda v: v.reshape(1, width)
    full2 = lambda shape: pl.BlockSpec(shape, lambda i: (0, 0))
    full3 = lambda shape: pl.BlockSpec(shape, lambda i: (0, 0, 0))
    return pl.pallas_call(
        functools.partial(_rglru_kernel, taps=taps),
        grid=(m // tm,),
        in_specs=[pl.BlockSpec((tm, width), lambda i: (i, 0)),
                  pl.BlockSpec((tm, width), lambda i: (i, 1)),
                  pl.BlockSpec((8, width), lambda i: (jnp.maximum(i * (tm // 8) - 1, 0), 0)),
                  full2((taps, width)), full2((1, width)),
                  full3((nb, RG_BLOCK, RG_BLOCK)), full2((1, width)),
                  full3((nb, RG_BLOCK, RG_BLOCK)), full2((1, width)),
                  full2((1, width))],
        out_specs=pl.BlockSpec((tm, width), lambda i: (i, 0)),
        out_shape=jax.ShapeDtypeStruct((m, width), BF16),
        scratch_shapes=[pltpu.VMEM((tm + 8, width), F32),
                        pltpu.VMEM((tm, width), F32),
                        pltpu.VMEM((tm, width), F32),
                        pltpu.VMEM((tm, width), F32),
                        pltpu.VMEM((1, width), F32)],
        compiler_params=_params(("arbitrary",)),
        name="rglru",
    )(p, p, p, cw, row(cb), wa, row(ba), wi, row(bi), row(lam))


def _latent_kernel(c_ref, ct_ref, g_ref, gt_ref, o_ref, ot_ref):
    o_ref[...] = _rms(c_ref[...], g_ref[...]).astype(BF16)
    ct = ct_ref[...]
    ot_ref[0:KV_RANK, :] = (ct * lax.rsqrt(jnp.mean(ct * ct, axis=0, keepdims=True) + EPS)
                            * gt_ref[...]).astype(BF16)
    first = lax.broadcasted_iota(jnp.int32, (ONES_ROWS, ct.shape[1]), 0) == 0
    ot_ref[KV_RANK:KV_RANK + ONES_ROWS, :] = jnp.where(first, 1.0, 0.0).astype(BF16)


def _latent(p, pt, g, col0, row0):
    m = p.shape[0]
    r = g.shape[0]
    tm = _pick_tile(m, 640, 128)
    return pl.pallas_call(
        _latent_kernel,
        grid=(m // tm,),
        in_specs=[pl.BlockSpec((tm, r), lambda i: (i, col0 // r)),
                  pl.BlockSpec((r, tm), lambda i: (row0 // r, i)),
                  pl.BlockSpec((1, r), lambda i: (0, 0)),
                  pl.BlockSpec((r, 1), lambda i: (0, 0))],
        out_specs=[pl.BlockSpec((tm, r), lambda i: (i, 0)),
                   pl.BlockSpec((r + ONES_ROWS, tm), lambda i: (0, i))],
        out_shape=[jax.ShapeDtypeStruct((m, r), BF16), jax.ShapeDtypeStruct((r + ONES_ROWS, m), BF16)],
        compiler_params=_params(("parallel",)),
        name="latent_norm",
    )(p, pt, g.reshape(1, r), g.reshape(r, 1))


def _dsa_kernel(qt_ref, iqt_ref, iwt_ref, c_ref, ct_ref, ik_ref, wuk_ref, wuvt_ref, o_ref,
                sc_ref, iqp_ref, qlt_ref, acc_ref, m_ref, sa_ref, sb_ref, *, k_sel):
    i = pl.program_id(0)
    tq = Q_BLOCK
    n_heads = ATT_HEADS
    sub_blocks = KEY_CHUNK // tq
    n_chunks = (i * tq + KEY_CHUNK - 1) // KEY_CHUNK

    scale = ATT_HEAD_DIM ** -0.5 * LOG2_E
    for h in range(n_heads):
        qh = qt_ref[h * ATT_HEAD_DIM:(h + 1) * ATT_HEAD_DIM, :].astype(BF16)
        ql = jnp.dot(wuk_ref[h], qh, preferred_element_type=F32) * scale
        qlt_ref[:, h * tq:(h + 1) * tq] = ql.astype(BF16)

    iw = iwt_ref[...] * ((IDX_DIM ** -0.5) * (IDX_HEADS ** -0.5))
    k_loc = lax.broadcasted_iota(jnp.int32, (tq, tq), 0)
    q_loc = lax.broadcasted_iota(jnp.int32, (tq, tq), 1)

    for hp in range(IDX_HEADS // 2):
        pair = jnp.concatenate(
            [iqt_ref[(2 * hp) * IDX_DIM:(2 * hp + 1) * IDX_DIM, :],
             iqt_ref[(2 * hp + 1) * IDX_DIM:(2 * hp + 2) * IDX_DIM, :]], axis=1).astype(BF16)
        iqp_ref[hp, 0:IDX_DIM, :] = pair
        iqp_ref[hp, IDX_DIM:2 * IDX_DIM, :] = jnp.zeros_like(pair)

    def score_chunk(j, carry):
        mn, mx = carry
        for u in range(sub_blocks):
            kb = j * sub_blocks + 1 + u
            r0 = pl.multiple_of(kb * tq, tq)
            ikb = ik_ref[pl.ds(r0, tq), :]
            s = jnp.zeros((tq, tq), F32)
            for hp in range(IDX_HEADS // 2):
                x = jnp.dot(ikb, iqp_ref[hp], preferred_element_type=F32)
                s = s + jnp.maximum(x[:, 0:tq], 0.0) * iw[2 * hp:2 * hp + 1, :]
                s = s + jnp.maximum(x[:, tq:2 * tq], 0.0) * iw[2 * hp + 1:2 * hp + 2, :]
            visible = jnp.logical_or(kb < i, jnp.logical_and(kb == i, k_loc <= q_loc))
            sc_ref[pl.ds(r0, tq), :] = jnp.where(visible, s, -jnp.inf)
            mx = jnp.maximum(mx, jnp.max(jnp.where(visible, s, -jnp.inf).reshape(tq // 8, 8, tq), axis=0))
            mn = jnp.minimum(mn, jnp.min(jnp.where(visible, s, jnp.inf).reshape(tq // 8, 8, tq), axis=0))
        return mn, mx

    mn, mx = lax.fori_loop(0, n_chunks, score_chunk,
                           (jnp.full((8, tq), jnp.inf, F32), jnp.full((8, tq), -jnp.inf, F32)))
    row_min = jnp.min(mn, axis=0, keepdims=True)
    row_max = jnp.max(mx, axis=0, keepdims=True)

    groups = KEY_CHUNK // 64

    def chunk_scores(j):
        r0 = pl.multiple_of(FRONT + j * KEY_CHUNK, FRONT)
        return sc_ref[pl.ds(r0, KEY_CHUNK), :].reshape(groups, 8, 8, tq)

    def count_ge(t):
        def body(j, cnt):
            kk = chunk_scores(j)
            for g in range(groups):
                cnt = jnp.where(kk[g] >= t, cnt + 1, cnt)
            return cnt
        cnt = lax.fori_loop(0, n_chunks, body, jnp.zeros((8, 8, tq), jnp.int32))
        return jnp.sum(jnp.sum(cnt, axis=0), axis=0, keepdims=True)

    lane = lax.broadcasted_iota(jnp.int32, (1, tq), 1)
    n_visible = (i - 1) * tq + lane + 1
    k_row = jnp.minimum(k_sel, n_visible)

    def probe(mid, movable, lo, hi, cnt_lo):
        c = count_ge(mid)
        up = jnp.logical_and(c >= k_row, movable)
        down = jnp.logical_and(jnp.logical_not(up), movable)
        return jnp.where(up, mid, lo), jnp.where(down, mid, hi), jnp.where(up, c, cnt_lo)

    def midpoint(lo, hi, cnt_lo):
        mid = lo + 0.5 * (hi - lo)
        movable = jnp.logical_and(cnt_lo != k_row, jnp.logical_and(mid > lo, mid < hi))
        return mid, movable

    def lane_flags(lo, hi, cnt_lo):
        moving = midpoint(lo, hi, cnt_lo)[1].astype(jnp.int32)
        return jnp.sum(moving + 256 * (cnt_lo > k_row).astype(jnp.int32))

    lo, hi, cnt_lo = probe(row_max, n_visible > k_row, row_min, row_max, n_visible)

    def bisect_cond(state):
        return jnp.logical_and(state[0] % 256 > 0, state[1] < BISECT_CAP)

    def bisect_body(state):
        _, it, lo, hi, cnt_lo = state
        for _ in range(BISECT_STEPS):
            mid, movable = midpoint(lo, hi, cnt_lo)
            lo, hi, cnt_lo = probe(mid, movable, lo, hi, cnt_lo)
        return lane_flags(lo, hi, cnt_lo), it + 1, lo, hi, cnt_lo

    state = (lane_flags(lo, hi, cnt_lo), jnp.int32(0), lo, hi, cnt_lo)
    flags, _, thr, _, cnt_thr = lax.while_loop(bisect_cond, bisect_body, state)

    tied = cnt_thr > k_row

    @pl.when(flags >= 256)
    def _():
        key_idx = lax.broadcasted_iota(jnp.int32, (groups, 8, 8, tq), 0) * 64 + (
            lax.broadcasted_iota(jnp.int32, (groups, 8, 8, tq), 1) * 8
            + lax.broadcasted_iota(jnp.int32, (groups, 8, 8, tq), 2))

        def count_where(pred):
            def body(j, cnt):
                hit = pred(chunk_scores(j), j * KEY_CHUNK + key_idx)
                return cnt + jnp.sum(hit.astype(jnp.int32), axis=0)
            cnt = lax.fori_loop(0, n_chunks, body, jnp.zeros((8, 8, tq), jnp.int32))
            return jnp.sum(jnp.sum(cnt, axis=0), axis=0, keepdims=True)

        wanted = k_row - count_where(lambda kk, idx: kk > thr)

        def index_step(_, bounds):
            below, cap = bounds
            mid = below + lax.shift_right_logical(cap - below, 1)
            enough = count_where(lambda kk, idx: jnp.logical_and(kk == thr, idx <= mid)) >= wanted
            return jnp.where(enough, below, mid), jnp.where(enough, mid, cap)

        n_keys = n_chunks * KEY_CHUNK
        steps = max(1, (sc_ref.shape[0] - 1).bit_length())
        _, cap = lax.fori_loop(0, steps, index_step,
                               (jnp.full((1, tq), -1, jnp.int32), jnp.full((1, tq), 1, jnp.int32) * (n_keys - 1)))

        def drop_chunk(j, carry):
            r0 = pl.multiple_of(FRONT + j * KEY_CHUNK, FRONT)
            kk = chunk_scores(j)
            extra = jnp.logical_and(jnp.logical_and(kk == thr, j * KEY_CHUNK + key_idx > cap), tied)
            sc_ref[pl.ds(r0, KEY_CHUNK), :] = jnp.where(extra, -jnp.inf, kk).reshape(KEY_CHUNK, tq)
            return carry

        lax.fori_loop(0, n_chunks, drop_chunk, 0)

    qlt = qlt_ref[...]
    head_cols = [slice(h * tq, (h + 1) * tq) for h in range(n_heads)]

    def masked_scores(rows, bias, s_ref):
        bias2 = jnp.concatenate([bias, bias], axis=1)
        cmax = []
        for hp in range(n_heads // 2):
            cols = slice(2 * hp * tq, (2 * hp + 2) * tq)
            sm = jnp.dot(c_ref[rows, :], qlt[:, cols], preferred_element_type=F32) + bias2
            s_ref[:, cols] = sm
            cmax.append(jnp.max(sm, axis=0, keepdims=True))
        return jnp.concatenate(cmax, axis=1)

    def accumulate(rows, s_ref, cmax):
        m_old = m_ref[...]
        m_new = jnp.maximum(m_old, cmax)
        m_ref[...] = m_new
        p = jnp.exp2(s_ref[...] - m_new).astype(BF16)
        acc_ref[...] = jnp.exp2(m_old - m_new) * acc_ref[...] + jnp.dot(
            ct_ref[:, rows], p, preferred_element_type=F32)

    m_ref[...] = jnp.full(m_ref.shape, NEG_BIG, F32)
    acc_ref[...] = jnp.zeros_like(acc_ref)

    q_row = i * tq + q_loc
    allowed0 = jnp.logical_and(jnp.logical_or(k_loc >= PAD_ROWS, k_loc == q_row), k_loc <= q_row)
    s0_ref = sa_ref.at[0:FRONT, :]
    cmax0 = masked_scores(slice(0, FRONT), jnp.where(allowed0, 0.0, NEG_BIG), s0_ref)
    accumulate(slice(0, FRONT), s0_ref, cmax0)

    def chunk_rows(j):
        jc = jnp.clip(j, 0, n_chunks - 1)
        return pl.ds(pl.multiple_of(FRONT + jc * KEY_CHUNK, FRONT), KEY_CHUNK)

    def chunk_scores_masked(j, s_ref):
        rows = chunk_rows(j)
        ok = jnp.logical_and(sc_ref[rows, :] >= thr, j < n_chunks)
        return masked_scores(rows, jnp.where(ok, 0.0, NEG_BIG), s_ref)

    @pl.when(i > 0)
    def _():
        def chunk_pair(t, cmax_a):
            j = 2 * t
            cmax_b = chunk_scores_masked(j + 1, sb_ref)
            accumulate(chunk_rows(j), sa_ref, cmax_a)
            cmax_a = chunk_scores_masked(j + 2, sa_ref)
            accumulate(chunk_rows(j + 1), sb_ref, cmax_b)
            return cmax_a

        lax.fori_loop(0, (n_chunks + 1) // 2, chunk_pair, chunk_scores_masked(0, sa_ref))

    o_lat = (acc_ref[0:KV_RANK, :] / acc_ref[KV_RANK:KV_RANK + 1, :]).astype(BF16)
    for h in range(n_heads):
        oh = jnp.dot(wuvt_ref[h], o_lat[:, head_cols[h]], preferred_element_type=F32)
        o_ref[:, h * ATT_HEAD_DIM:(h + 1) * ATT_HEAD_DIM] = oh.T.astype(BF16)


def _dsa(pt, c, ct, ik, wuk, wuvt, k_sel, q_row0, iq_row0, iw_row0):
    m = c.shape[0]
    tq = Q_BLOCK
    d_q = ATT_HEADS * ATT_HEAD_DIM
    d_iq = IDX_HEADS * IDX_DIM
    full2 = lambda shape: pl.BlockSpec(shape, lambda i: (0, 0))
    full3 = lambda shape: pl.BlockSpec(shape, lambda i: (0, 0, 0))
    return pl.pallas_call(
        functools.partial(_dsa_kernel, k_sel=k_sel),
        grid=(m // tq,),
        in_specs=[pl.BlockSpec((d_q, tq), lambda i: (q_row0 // d_q, i)),
                  pl.BlockSpec((d_iq, tq), lambda i: (iq_row0 // d_iq, i)),
                  pl.BlockSpec((IDX_HEADS, tq), lambda i: (iw_row0 // IDX_HEADS, i)),
                  full2(c.shape), full2(ct.shape), full2(ik.shape),
                  full3(wuk.shape), full3(wuvt.shape)],
        out_specs=pl.BlockSpec((tq, d_q), lambda i: (i, 0)),
        out_shape=jax.ShapeDtypeStruct((m, d_q), BF16),
        scratch_shapes=[pltpu.VMEM((m, tq), F32),
                        pltpu.VMEM((IDX_HEADS // 2, 2 * IDX_DIM, 2 * tq), BF16),
                        pltpu.VMEM((KV_RANK, ATT_HEADS * tq), BF16),
                        pltpu.VMEM((KV_RANK + ONES_ROWS, ATT_HEADS * tq), F32),
                        pltpu.VMEM((1, ATT_HEADS * tq), F32),
                        pltpu.VMEM((KEY_CHUNK, ATT_HEADS * tq), F32),
                        pltpu.VMEM((KEY_CHUNK, ATT_HEADS * tq), F32)],
        compiler_params=_params(("arbitrary",)),
        name="dsa",
    )(pt, pt, pt, c, ct, ik, wuk, wuvt)


def _pad_cols(w, n):
    return jnp.pad(w, ((0, 0), (0, n - w.shape[1])))


def kernel(x, meta_tokens, ln_mix_pre, ln_mix_post, ln_ffn_pre, ln_ffn_post, ffn_w1, ffn_w3, ffn_w2,
           ab_w_in, ab_w_out, hgrn_lb_logits, hgrn_out_norm, sconv_w,
           cd_w_in, cd_w_out, rg_conv_w, rg_conv_b, rg_w_a, rg_b_a, rg_w_i, rg_b_i, rg_lambda,
           mla_kv_norm, mla_w_uk, mla_w_uv):
    assert x.shape[0] == 1
    seq, d = x.shape[1], x.shape[2]
    assert seq % KEY_CHUNK == 0
    d_a = hgrn_lb_logits.shape[1]
    d_b = sconv_w.shape[2]
    d_c = rg_lambda.shape[1]
    d_d = ATT_HEADS * ATT_HEAD_DIM
    d_iq = IDX_HEADS * IDX_DIM
    k_sel = min(TOPK_MAX, seq // 4)

    x2 = x[0]
    front = jnp.concatenate([jnp.zeros((PAD_ROWS, d), F32), meta_tokens.astype(F32)], axis=0)
    w1, w3, w2 = ffn_w1.astype(BF16), ffn_w3.astype(BF16), ffn_w2.astype(BF16)

    p0 = _norm_mm(x2, ln_mix_pre[0], ab_w_in[0].astype(BF16), tn=512, front=front)
    og = _hgrn(p0, hgrn_lb_logits, hgrn_out_norm[0], lb_row=0)
    w_out = ab_w_out[0].astype(BF16)
    h = _out_proj_conv(og, p0, sconv_w[0], 4 * d_a, w_out[:d_a], w_out[d_a:], x2, front, ln_mix_post[0])
    h = _ffn(h, ln_ffn_pre[0], w1, w3, w2, ln_ffn_post[0], layer=0)

    w_in = cd_w_in[0]
    o_rx, o_ry, o_q, o_c = 0, d_c, 2 * d_c, 2 * d_c + d_d
    o_iq = o_c + KV_RANK
    o_ik = o_iq + d_iq
    o_iw = o_ik + IDX_DIM
    w_rows = jnp.concatenate([w_in[:, o_rx:o_q], w_in[:, o_c:o_iq], w_in[:, o_ik:o_iw]], axis=1)
    w_rows = _pad_cols(w_rows, -(-w_rows.shape[1] // 512) * 512).astype(BF16)
    w_cols = jnp.concatenate([w_in[:, o_q:o_c], w_in[:, o_iq:o_ik], w_in[:, o_c:o_iq], w_in[:, o_iw:]], axis=1)
    w_cols = _pad_cols(w_cols, -(-w_cols.shape[1] // 512) * 512).astype(BF16).T
    p1 = _norm_mm(h, ln_mix_pre[1], w_rows, tn=w_rows.shape[1] // 4)
    p1t = _norm_mm_t(h, ln_mix_pre[1], w_cols, tc=w_cols.shape[0] // 4)

    hc = _rglru(p1, rg_conv_w[0], rg_conv_b[0], rg_w_a[0].astype(BF16), rg_b_a[0],
                rg_w_i[0].astype(BF16), rg_b_i[0], rg_lambda[0], width=d_c)
    c, ct = _latent(p1, p1t, mla_kv_norm[0], col0=2 * d_c, row0=d_d + d_iq)
    ik = p1[:, 2 * d_c + KV_RANK:2 * d_c + KV_RANK + 2 * IDX_DIM].astype(BF16)
    wuk = jnp.transpose(mla_w_uk[0], (1, 0, 2)).astype(BF16)
    wuvt = jnp.transpose(mla_w_uv[0], (1, 2, 0)).astype(BF16)
    att = _dsa(p1t, c, ct, ik, wuk, wuvt, k_sel,
               q_row0=0, iq_row0=d_d, iw_row0=d_d + d_iq + KV_RANK)
    w_out = cd_w_out[0].astype(BF16)
    h = _out_proj(hc, att, w_out[:d_c], w_out[d_c:], h, ln_mix_post[1])
    return _ffn(h, ln_ffn_pre[1], w1, w3, w2, ln_ffn_post[1], layer=1, skip_front=True)[None]
```

```python
import functools
import math

import jax
import jax.numpy as jnp
from jax import lax
from jax.experimental import pallas as pl
from jax.experimental.pallas import tpu as pltpu

F32 = jnp.float32
BF16 = jnp.bfloat16

EPS = 1e-6
N_META = 16
FRONT = 128
PAD_ROWS = FRONT - N_META
HGRN_HEAD = 128
HGRN_CHUNK = 128
RG_BLOCK = 128
RG_C = 8.0
ATT_HEADS = 8
ATT_HEAD_DIM = 128
KV_RANK = 256
IDX_HEADS = 16
IDX_DIM = 64
TOPK_MAX = 256
Q_BLOCK = 128
KEY_CHUNK = 512
ONES_ROWS = 16
LOG2_E = 1.4426950408889634
BISECT_STEPS = 4
BISECT_BLIND = 2
BISECT_CAP = 64
NEG_BIG = -1e30
VMEM_LIMIT = 56 * 1024 * 1024
PROJ_ROWS = 1664


def _pick_tile(n, target, mult):
    best = None
    for t in range(mult, min(n, target) + 1, mult):
        if n % t == 0:
            best = t
    assert best is not None, (n, target, mult)
    return best


def _params(sem):
    return pltpu.CompilerParams(dimension_semantics=sem, vmem_limit_bytes=VMEM_LIMIT)


def _rms(x, g):
    return x * lax.rsqrt(jnp.mean(x * x, axis=-1, keepdims=True) + EPS) * g


def _token_specs(tm, d):
    align = math.gcd(tm, FRONT)
    x_map = lambda *a: (pl.multiple_of(jnp.maximum(a[0] * tm - FRONT, 0), align), 0)
    return [pl.BlockSpec((pl.Element(tm), pl.Element(d)), x_map),
            pl.BlockSpec((FRONT, d), lambda *a: (0, 0))]


def _fill_normed(x_ref, front_ref, g_ref, xn_ref):
    g = g_ref[...]
    first = 0 if front_ref is None else (pl.program_id(0) == 0).astype(jnp.int32)

    def piece(p, carry):
        src = pl.multiple_of(jnp.maximum(p - first, 0) * FRONT, FRONT)
        rows = x_ref[pl.ds(src, FRONT), :]
        if front_ref is not None:
            rows = jnp.where(jnp.logical_and(first == 1, p == 0), front_ref[...], rows)
        xn_ref[pl.ds(pl.multiple_of(p * FRONT, FRONT), FRONT), :] = _rms(rows, g).astype(BF16)
        return carry

    lax.fori_loop(0, xn_ref.shape[0] // FRONT, piece, 0)


def _norm_mm_kernel(x_ref, *rest, tokens):
    front_ref = rest[0] if tokens else None
    g_ref, w_ref, o_ref, xn_ref = rest[-4:]

    @pl.when(pl.program_id(1) == 0)
    def _():
        _fill_normed(x_ref, front_ref, g_ref, xn_ref)

    o_ref[...] = jnp.dot(xn_ref[...], w_ref[...], preferred_element_type=F32)


def _norm_mm(x, g, w, tn=512, front=None):
    d = x.shape[1]
    m = x.shape[0] + (0 if front is None else FRONT)
    n = w.shape[1]
    tm = _pick_tile(m, PROJ_ROWS, FRONT)
    if front is None:
        row_specs, rows = [pl.BlockSpec((tm, d), lambda i, j: (i, 0))], (x,)
    else:
        row_specs, rows = _token_specs(tm, d), (x, front)
    return pl.pallas_call(
        functools.partial(_norm_mm_kernel, tokens=front is not None),
        grid=(m // tm, n // tn),
        in_specs=row_specs + [pl.BlockSpec((1, d), lambda i, j: (0, 0)),
                              pl.BlockSpec((d, tn), lambda i, j: (0, j))],
        out_specs=pl.BlockSpec((tm, tn), lambda i, j: (i, j)),
        out_shape=jax.ShapeDtypeStruct((m, n), F32),
        scratch_shapes=[pltpu.VMEM((tm, d), BF16)],
        compiler_params=_params(("parallel", "arbitrary")),
        name="norm_proj",
    )(*rows, g.reshape(1, d), w)


def _norm_mm_t_kernel(x_ref, g_ref, wt_ref, o_ref, xn_ref):
    @pl.when(pl.program_id(1) == 0)
    def _():
        _fill_normed(x_ref, None, g_ref, xn_ref)

    o_ref[...] = lax.dot_general(wt_ref[...], xn_ref[...], (((1,), (1,)), ((), ())),
                                 preferred_element_type=F32)


def _norm_mm_t(x, g, wt, tc=512):
    m, d = x.shape
    n = wt.shape[0]
    tm = _pick_tile(m, PROJ_ROWS, FRONT)
    return pl.pallas_call(
        _norm_mm_t_kernel,
        grid=(m // tm, n // tc),
        in_specs=[pl.BlockSpec((tm, d), lambda i, j: (i, 0)),
                  pl.BlockSpec((1, d), lambda i, j: (0, 0)),
                  pl.BlockSpec((tc, d), lambda i, j: (j, 0))],
        out_specs=pl.BlockSpec((tc, tm), lambda i, j: (j, i)),
        out_shape=jax.ShapeDtypeStruct((n, m), F32),
        scratch_shapes=[pltpu.VMEM((tm, d), BF16)],
        compiler_params=_params(("parallel", "arbitrary")),
        name="norm_proj_t",
    )(x, g.reshape(1, d), wt)


def _out_proj_kernel(a_ref, b_ref, wa_ref, wb_ref, g_ref, h_ref, o_ref):
    y = jnp.dot(a_ref[...], wa_ref[...], preferred_element_type=F32)
    y = y + jnp.dot(b_ref[...], wb_ref[...], preferred_element_type=F32)
    o_ref[...] = h_ref[...] + _rms(y, g_ref[...])


def _out_proj(a, b, wa, wb, h, g):
    m, ka = a.shape
    kb = b.shape[1]
    d = wa.shape[1]
    tm = _pick_tile(m, 640, 128)
    return pl.pallas_call(
        _out_proj_kernel,
        grid=(m // tm,),
        in_specs=[pl.BlockSpec((tm, ka), lambda i: (i, 0)),
                  pl.BlockSpec((tm, kb), lambda i: (i, 0)),
                  pl.BlockSpec((ka, d), lambda i: (0, 0)),
                  pl.BlockSpec((kb, d), lambda i: (0, 0)),
                  pl.BlockSpec((1, d), lambda i: (0, 0)),
                  pl.BlockSpec((tm, d), lambda i: (i, 0))],
        out_specs=pl.BlockSpec((tm, d), lambda i: (i, 0)),
        out_shape=jax.ShapeDtypeStruct((m, d), F32),
        compiler_params=_params(("parallel",)),
        name="out_proj",
    )(a, b, wa, wb, g.reshape(1, d), h)


def _out_proj_conv_kernel(a_ref, sx_ref, sb_ref, sc_ref, sxp_ref, scp_ref, cw_ref, wa_ref, wb_ref, g_ref,
                          x_ref, front_ref, o_ref, ext_ref, *, taps):
    tm = a_ref.shape[0]
    i = pl.program_id(0)
    ext_ref[0:8, :] = jnp.where(i > 0, sxp_ref[...] * scp_ref[...], 0.0)
    ext_ref[8:8 + tm, :] = sx_ref[...] * sc_ref[...]
    conv = jnp.zeros(sx_ref.shape, F32)
    for j in range(taps):
        s = 8 - (taps - 1) + j
        conv = conv + cw_ref[j:j + 1, :] * ext_ref[s:s + tm, :]
    yb = (sb_ref[...] * conv).astype(BF16)

    y = jnp.dot(a_ref[...], wa_ref[...], preferred_element_type=F32)
    y = _rms(y + jnp.dot(yb, wb_ref[...], preferred_element_type=F32), g_ref[...])

    @pl.when(i == 0)
    def _():
        o_ref[0:FRONT, :] = front_ref[...] + y[0:FRONT, :]
        o_ref[FRONT:tm, :] = x_ref[0:tm - FRONT, :] + y[FRONT:tm, :]

    @pl.when(i > 0)
    def _():
        o_ref[...] = x_ref[...] + y


def _out_proj_conv(a, p, cw, col0, wa, wb, x, front, g):
    m, ka = a.shape
    taps, width = cw.shape
    d = wa.shape[1]
    tm = _pick_tile(m, 320, 64)
    cb = col0 // width
    prev = lambda off: (lambda i: (jnp.maximum(i * (tm // 8) - 1, 0), off))
    return pl.pallas_call(
        functools.partial(_out_proj_conv_kernel, taps=taps),
        grid=(m // tm,),
        in_specs=[pl.BlockSpec((tm, ka), lambda i: (i, 0)),
                  pl.BlockSpec((tm, width), lambda i: (i, cb)),
                  pl.BlockSpec((tm, width), lambda i: (i, cb + 1)),
                  pl.BlockSpec((tm, width), lambda i: (i, cb + 2)),
                  pl.BlockSpec((8, width), prev(cb)),
                  pl.BlockSpec((8, width), prev(cb + 2)),
                  pl.BlockSpec((taps, width), lambda i: (0, 0)),
                  pl.BlockSpec((ka, d), lambda i: (0, 0)),
                  pl.BlockSpec((width, d), lambda i: (0, 0)),
                  pl.BlockSpec((1, d), lambda i: (0, 0))] + _token_specs(tm, d),
        out_specs=pl.BlockSpec((tm, d), lambda i: (i, 0)),
        out_shape=jax.ShapeDtypeStruct((m, d), F32),
        scratch_shapes=[pltpu.VMEM((tm + 8, width), F32)],
        compiler_params=_params(("parallel",)),
        name="out_proj_conv",
    )(a, p, p, p, p, p, cw, wa, wb, g.reshape(1, d), x, front)


def _ffn_kernel(h_ref, gpre_ref, w1_ref, w3_ref, w2_ref, gpost_ref, o_ref, xn_ref, acc_ref):
    j = pl.program_id(1)

    @pl.when(j == 0)
    def _():
        xn_ref[...] = _rms(h_ref[...], gpre_ref[...]).astype(BF16)
        acc_ref[...] = jnp.zeros_like(acc_ref)

    xn = xn_ref[...]
    a = jnp.dot(xn, w1_ref[...], preferred_element_type=F32)
    b = jnp.dot(xn, w3_ref[...], preferred_element_type=F32)
    u = (a * jax.nn.sigmoid(a) * b).astype(BF16)
    acc_ref[...] += jnp.dot(u, w2_ref[...], preferred_element_type=F32)

    @pl.when(j == pl.num_programs(1) - 1)
    def _():
        o_ref[...] = h_ref[...] + _rms(acc_ref[...], gpost_ref[...])


def _ffn(h, gpre, w1, w3, w2, gpost, layer, tf=512, skip_front=False):
    d = h.shape[1]
    f = w1.shape[2]
    if skip_front:
        m = h.shape[0] - FRONT
        tm = _pick_tile(m, 640, 128)
        h_spec = pl.BlockSpec((pl.Element(tm), pl.Element(d)), lambda i, j: (pl.multiple_of(FRONT + i * tm, FRONT), 0))
    else:
        m = h.shape[0]
        tm = _pick_tile(m, 640, 128)
        h_spec = pl.BlockSpec((tm, d), lambda i, j: (i, 0))
    return pl.pallas_call(
        _ffn_kernel,
        grid=(m // tm, f // tf),
        in_specs=[h_spec,
                  pl.BlockSpec((1, d), lambda i, j: (0, 0)),
                  pl.BlockSpec((None, d, tf), lambda i, j: (layer, 0, j)),
                  pl.BlockSpec((None, d, tf), lambda i, j: (layer, 0, j)),
                  pl.BlockSpec((None, tf, d), lambda i, j: (layer, j, 0)),
                  pl.BlockSpec((1, d), lambda i, j: (0, 0))],
        out_specs=pl.BlockSpec((tm, d), lambda i, j: (i, 0)),
        out_shape=jax.ShapeDtypeStruct((m, d), F32),
        scratch_shapes=[pltpu.VMEM((tm, d), BF16), pltpu.VMEM((tm, d), F32)],
        compiler_params=_params(("parallel", "arbitrary")),
        name="ffn",
    )(h, gpre.reshape(1, d), w1, w3, w2, gpost.reshape(1, d))


def _cumsum_rows(tri, x):
    hi = x.astype(BF16)
    rest = x - hi.astype(F32)
    mid = rest.astype(BF16)
    lo = (rest - mid.astype(F32)).astype(BF16)
    return (jnp.dot(tri, hi, preferred_element_type=F32) + jnp.dot(tri, mid, preferred_element_type=F32)
            + jnp.dot(tri, lo, preferred_element_type=F32))


def _edge_rows(b, half):
    rows = b.shape[0]
    if half >= 8:
        parts = [jnp.broadcast_to(b[e:e + 1, :], (2 * half, b.shape[1]))
                 for e in range(half - 1, rows, 2 * half)]
        return parts[0] if len(parts) == 1 else jnp.concatenate(parts, axis=0)
    b3 = b.reshape(rows // 8, 8, b.shape[1])
    sub = lax.broadcasted_iota(jnp.int32, b3.shape, 1)
    pick = lambda r: jnp.broadcast_to(b3[:, r:r + 1, :], b3.shape)
    edge = pick(half - 1)
    for start in range(2 * half, 8, 2 * half):
        edge = jnp.where(sub >= start, pick(start + half - 1), edge)
    return edge.reshape(b.shape)


def _hgrn_kernel(q_ref, f_ref, v_ref, gate_ref, lbl_ref, gn_ref, o_ref, st_ref, *, n_heads, lb_row):
    c_rows = HGRN_CHUNK

    @pl.when(pl.program_id(1) == 0)
    def _():
        st_ref[...] = jnp.zeros_like(st_ref)

    logits = lbl_ref[...]
    ex = jnp.exp(logits - jnp.max(logits, axis=0, keepdims=True))
    lb_all = jnp.sum(ex[0:lb_row + 1, :], axis=0, keepdims=True) / jnp.sum(ex, axis=0, keepdims=True)

    r_i = lax.broadcasted_iota(jnp.int32, (c_rows, c_rows), 0)
    c_i = lax.broadcasted_iota(jnp.int32, (c_rows, c_rows), 1)
    tri = (r_i >= c_i).astype(BF16)
    levels = []
    half = c_rows // 2
    while half >= 1:
        shift = half.bit_length()
        upper = (lax.shift_right_logical(r_i, shift - 1) & 1) == 1
        same = lax.shift_right_logical(r_i, shift) == lax.shift_right_logical(c_i, shift)
        levels.append((half, upper, jnp.where(upper, 1.0, -1.0), same))
        half //= 2

    for hh in range(n_heads):
        cols = slice(hh * HGRN_HEAD, (hh + 1) * HGRN_HEAD)
        lb = lb_all[:, cols]
        q = q_ref[:, cols]
        v = v_ref[:, cols]
        f = lb + (1.0 - lb) * jax.nn.sigmoid(f_ref[:, cols])
        k = 1.0 - f
        b = _cumsum_rows(tri, jnp.log2(f))
        b_last = b[c_rows - 1:c_rows, :]

        st = st_ref[hh]
        inter = lax.dot_general((q * jnp.exp2(b)).astype(BF16), st.astype(BF16),
                                (((1,), (1,)), ((), ())), preferred_element_type=F32)
        kt = (k * jnp.exp2(b_last - b)).astype(BF16)
        st_ref[hh] = st * jnp.exp2(b_last) + lax.dot_general(
            v.astype(BF16), kt, (((0,), (0,)), ((), ())), preferred_element_type=F32)

        att = jnp.zeros((c_rows, c_rows), F32)
        for half, upper, sign, same in levels:
            decay = jnp.exp2((b - _edge_rows(b, half)) * sign)
            scaled = jnp.where(upper, q, k) * decay
            qt = jnp.where(upper, scaled, 0.0).astype(BF16)
            kl = jnp.where(upper, 0.0, scaled).astype(BF16)
            pair = lax.dot_general(qt, kl, (((1,), (1,)), ((), ())), preferred_element_type=F32)
            att = att + jnp.where(same, pair, 0.0)
        o = inter + jnp.dot(att.astype(BF16), v.astype(BF16), preferred_element_type=F32)
        o = o + jnp.sum(q * k, axis=-1, keepdims=True) * v

        gate = gate_ref[:, cols]
        o_ref[:, cols] = (_rms(o, gn_ref[...]) * (gate * jax.nn.sigmoid(gate))).astype(BF16)


def _hgrn(p, lb_logits, gn, lb_row):
    m = p.shape[0]
    d_a = lb_logits.shape[1]
    hp = 8
    width = hp * HGRN_HEAD
    groups = d_a // width
    tb = HGRN_CHUNK
    n_l = lb_logits.shape[0]
    col = lambda off: (lambda g, t: (t, off + g))
    kern = functools.partial(_hgrn_kernel, n_heads=hp, lb_row=lb_row)
    return pl.pallas_call(
        kern,
        grid=(groups, m // tb),
        in_specs=[pl.BlockSpec((tb, width), col(0)),
                  pl.BlockSpec((tb, width), col(groups)),
                  pl.BlockSpec((tb, width), col(2 * groups)),
                  pl.BlockSpec((tb, width), col(3 * groups)),
                  pl.BlockSpec((n_l, width), lambda g, t: (0, g)),
                  pl.BlockSpec((1, HGRN_HEAD), lambda g, t: (0, 0))],
        out_specs=pl.BlockSpec((tb, width), lambda g, t: (t, g)),
        out_shape=jax.ShapeDtypeStruct((m, d_a), BF16),
        scratch_shapes=[pltpu.VMEM((hp, HGRN_HEAD, HGRN_HEAD), F32)],
        compiler_params=_params(("parallel", "arbitrary")),
        name="hgrn2",
    )(p, p, p, p, lb_logits, gn.reshape(1, HGRN_HEAD))


def _rglru_kernel(rx_ref, ry_ref, rxp_ref, cw_ref, cb_ref, wa_ref, ba_ref, wi_ref, bi_ref, lam_ref,
                  o_ref, ext_ref, a_ref, x_ref, hs_ref, h_ref, *, taps):
    tm, width = rx_ref.shape
    i = pl.program_id(0)

    @pl.when(i == 0)
    def _():
        h_ref[...] = jnp.zeros_like(h_ref)

    ext_ref[0:8, :] = jnp.where(i > 0, rxp_ref[...], 0.0)
    ext_ref[8:8 + tm, :] = rx_ref[...]
    u = jnp.zeros((tm, width), F32) + cb_ref[...]
    for j in range(taps):
        s = 8 - (taps - 1) + j
        u = u + cw_ref[j:j + 1, :] * ext_ref[s:s + tm, :]

    u_b = u.astype(BF16)
    r_parts, i_parts = [], []
    for n in range(width // RG_BLOCK):
        blk = slice(n * RG_BLOCK, (n + 1) * RG_BLOCK)
        r_parts.append(jnp.dot(u_b[:, blk], wa_ref[n], preferred_element_type=F32))
        i_parts.append(jnp.dot(u_b[:, blk], wi_ref[n], preferred_element_type=F32))
    r = jax.nn.sigmoid(jnp.concatenate(r_parts, axis=1) + ba_ref[...])
    ig = jax.nn.sigmoid(jnp.concatenate(i_parts, axis=1) + bi_ref[...])

    neg_lam = -lam_ref[...]
    softplus = jnp.maximum(neg_lam, 0.0) + jnp.log1p(jnp.exp(-jnp.abs(neg_lam)))
    log_a = -RG_C * r * softplus
    row = i * tm + lax.broadcasted_iota(jnp.int32, (tm, 1), 0)
    a = jnp.exp(log_a)
    xin = jnp.sqrt(1.0 - a * a) * (ig * u)
    a_ref[...] = a
    x_ref[...] = jnp.where(row >= PAD_ROWS, xin, 0.0)

    def group(gidx, h):
        base = pl.multiple_of(gidx * 8, 8)
        a8 = a_ref[pl.ds(base, 8), :]
        x8 = x_ref[pl.ds(base, 8), :]
        for rr in range(8):
            h = a8[rr:rr + 1, :] * h + x8[rr:rr + 1, :]
            hs_ref[pl.ds(base + rr, 1), :] = h
        return h

    h_ref[...] = lax.fori_loop(0, tm // 8, group, h_ref[...])
    o_ref[...] = (hs_ref[...] * jax.nn.gelu(ry_ref[...])).astype(BF16)


def _rglru(p, cw, cb, wa, ba, wi, bi, lam, width):
    m = p.shape[0]
    taps = cw.shape[0]
    tm = _pick_tile(m, 640, 128)
    nb = width // RG_BLOCK
    row = lambda v: v.reshape(1, width)
    full2 = lambda shape: pl.BlockSpec(shape, lambda i: (0, 0))
    full3 = lambda shape: pl.BlockSpec(shape, lambda i: (0, 0, 0))
    return pl.pallas_call(
        functools.partial(_rglru_kernel, taps=taps),
        grid=(m // tm,),
        in_specs=[pl.BlockSpec((tm, width), lambda i: (i, 0)),
                  pl.BlockSpec((tm, width), lambda i: (i, 1)),
                  pl.BlockSpec((8, width), lambda i: (jnp.maximum(i * (tm // 8) - 1, 0), 0)),
                  full2((taps, width)), full2((1, width)),
                  full3((nb, RG_BLOCK, RG_BLOCK)), full2((1, width)),
                  full3((nb, RG_BLOCK, RG_BLOCK)), full2((1, width)),
                  full2((1, width))],
        out_specs=pl.BlockSpec((tm, width), lambda i: (i, 0)),
        out_shape=jax.ShapeDtypeStruct((m, width), BF16),
        scratch_shapes=[pltpu.VMEM((tm + 8, width), F32),
                        pltpu.VMEM((tm, width), F32),
                        pltpu.VMEM((tm, width), F32),
                        pltpu.VMEM((tm, width), F32),
                        pltpu.VMEM((1, width), F32)],
        compiler_params=_params(("arbitrary",)),
        name="rglru",
    )(p, p, p, cw, row(cb), wa, row(ba), wi, row(bi), row(lam))


def _latent_kernel(c_ref, ct_ref, g_ref, gt_ref, o_ref, ot_ref):
    o_ref[...] = _rms(c_ref[...], g_ref[...]).astype(BF16)
    ct = ct_ref[...]
    ot_ref[0:KV_RANK, :] = (ct * lax.rsqrt(jnp.mean(ct * ct, axis=0, keepdims=True) + EPS)
                            * gt_ref[...]).astype(BF16)
    first = lax.broadcasted_iota(jnp.int32, (ONES_ROWS, ct.shape[1]), 0) == 0
    ot_ref[KV_RANK:KV_RANK + ONES_ROWS, :] = jnp.where(first, 1.0, 0.0).astype(BF16)


def _latent(p, pt, g, col0, row0):
    m = p.shape[0]
    r = g.shape[0]
    tm = _pick_tile(m, 640, 128)
    return pl.pallas_call(
        _latent_kernel,
        grid=(m // tm,),
        in_specs=[pl.BlockSpec((tm, r), lambda i: (i, col0 // r)),
                  pl.BlockSpec((r, tm), lambda i: (row0 // r, i)),
                  pl.BlockSpec((1, r), lambda i: (0, 0)),
                  pl.BlockSpec((r, 1), lambda i: (0, 0))],
        out_specs=[pl.BlockSpec((tm, r), lambda i: (i, 0)),
                   pl.BlockSpec((r + ONES_ROWS, tm), lambda i: (0, i))],
        out_shape=[jax.ShapeDtypeStruct((m, r), BF16), jax.ShapeDtypeStruct((r + ONES_ROWS, m), BF16)],
        compiler_params=_params(("parallel",)),
        name="latent_norm",
    )(p, pt, g.reshape(1, r), g.reshape(r, 1))


def _dsa_kernel(qt_ref, iqt_ref, iwt_ref, c_ref, ct_ref, ik_ref, wuk_ref, wuvt_ref, o_ref,
                sc_ref, iqp_ref, qlt_ref, acc_ref, m_ref, sa_ref, sb_ref, *, k_sel):
    i = pl.program_id(0)
    tq = Q_BLOCK
    n_heads = ATT_HEADS
    sub_blocks = KEY_CHUNK // tq
    n_chunks = (i * tq + KEY_CHUNK - 1) // KEY_CHUNK

    scale = ATT_HEAD_DIM ** -0.5 * LOG2_E
    for h in range(n_heads):
        qh = qt_ref[h * ATT_HEAD_DIM:(h + 1) * ATT_HEAD_DIM, :].astype(BF16)
        ql = jnp.dot(wuk_ref[h], qh, preferred_element_type=F32) * scale
        qlt_ref[:, h * tq:(h + 1) * tq] = ql.astype(BF16)

    iw = iwt_ref[...] * ((IDX_DIM ** -0.5) * (IDX_HEADS ** -0.5))
    k_loc = lax.broadcasted_iota(jnp.int32, (tq, tq), 0)
    q_loc = lax.broadcasted_iota(jnp.int32, (tq, tq), 1)

    for hp in range(IDX_HEADS // 2):
        pair = jnp.concatenate(
            [iqt_ref[(2 * hp) * IDX_DIM:(2 * hp + 1) * IDX_DIM, :],
             iqt_ref[(2 * hp + 1) * IDX_DIM:(2 * hp + 2) * IDX_DIM, :]], axis=1).astype(BF16)
        iqp_ref[hp, 0:IDX_DIM, :] = pair
        iqp_ref[hp, IDX_DIM:2 * IDX_DIM, :] = jnp.zeros_like(pair)

    def score_chunk(j, carry):
        mn, mx = carry
        for u in range(sub_blocks):
            kb = j * sub_blocks + 1 + u
            r0 = pl.multiple_of(kb * tq, tq)
            ikb = ik_ref[pl.ds(r0, tq), :]
            s = jnp.zeros((tq, tq), F32)
            for hp in range(IDX_HEADS // 2):
                x = jnp.dot(ikb, iqp_ref[hp], preferred_element_type=F32)
                s = s + jnp.maximum(x[:, 0:tq], 0.0) * iw[2 * hp:2 * hp + 1, :]
                s = s + jnp.maximum(x[:, tq:2 * tq], 0.0) * iw[2 * hp + 1:2 * hp + 2, :]
            visible = jnp.logical_or(kb < i, jnp.logical_and(kb == i, k_loc <= q_loc))
            sc_ref[pl.ds(r0, tq), :] = jnp.where(visible, s, -jnp.inf)
            mx = jnp.maximum(mx, jnp.max(jnp.where(visible, s, -jnp.inf).reshape(tq // 8, 8, tq), axis=0))
            mn = jnp.minimum(mn, jnp.min(jnp.where(visible, s, jnp.inf).reshape(tq // 8, 8, tq), axis=0))
        return mn, mx

    mn, mx = lax.fori_loop(0, n_chunks, score_chunk,
                           (jnp.full((8, tq), jnp.inf, F32), jnp.full((8, tq), -jnp.inf, F32)))
    row_min = jnp.min(mn, axis=0, keepdims=True)
    row_max = jnp.max(mx, axis=0, keepdims=True)

    groups = KEY_CHUNK // 64

    def chunk_scores(j):
        r0 = pl.multiple_of(FRONT + j * KEY_CHUNK, FRONT)
        return sc_ref[pl.ds(r0, KEY_CHUNK), :].reshape(groups, 8, 8, tq)

    def count_ge(t):
        def body(j, cnt):
            kk = chunk_scores(j)
            for g in range(groups):
                cnt = jnp.where(kk[g] >= t, cnt + 1, cnt)
            return cnt
        cnt = lax.fori_loop(0, n_chunks, body, jnp.zeros((8, 8, tq), jnp.int32))
        return jnp.sum(jnp.sum(cnt, axis=0), axis=0, keepdims=True)

    lane = lax.broadcasted_iota(jnp.int32, (1, tq), 1)
    n_visible = (i - 1) * tq + lane + 1
    k_row = jnp.minimum(k_sel, n_visible)

    def probe(mid, movable, lo, hi, cnt_lo):
        c = count_ge(mid)
        up = jnp.logical_and(c >= k_row, movable)
        down = jnp.logical_and(jnp.logical_not(up), movable)
        return jnp.where(up, mid, lo), jnp.where(down, mid, hi), jnp.where(up, c, cnt_lo)

    def midpoint(lo, hi, cnt_lo):
        mid = lo + 0.5 * (hi - lo)
        movable = jnp.logical_and(cnt_lo != k_row, jnp.logical_and(mid > lo, mid < hi))
        return mid, movable

    def lane_flags(lo, hi, cnt_lo):
        moving = midpoint(lo, hi, cnt_lo)[1].astype(jnp.int32)
        return jnp.sum(moving + 256 * (cnt_lo > k_row).astype(jnp.int32))

    lo, hi, cnt_lo = probe(row_max, n_visible > k_row, row_min, row_max, n_visible)

    def bisect_cond(state):
        return jnp.logical_and(state[0] % 256 > 0, state[1] < BISECT_CAP)

    def bisect_steps(_, bracket):
        lo, hi, cnt_lo = bracket
        for _ in range(BISECT_STEPS):
            mid, movable = midpoint(lo, hi, cnt_lo)
            lo, hi, cnt_lo = probe(mid, movable, lo, hi, cnt_lo)
        return lo, hi, cnt_lo

    def bisect_body(state):
        lo, hi, cnt_lo = bisect_steps(0, state[2:])
        return lane_flags(lo, hi, cnt_lo), state[1] + 1, lo, hi, cnt_lo

    lo, hi, cnt_lo = lax.fori_loop(0, BISECT_BLIND, bisect_steps, (lo, hi, cnt_lo))
    state = (lane_flags(lo, hi, cnt_lo), jnp.int32(0), lo, hi, cnt_lo)
    flags, _, thr, _, cnt_thr = lax.while_loop(bisect_cond, bisect_body, state)

    tied = cnt_thr > k_row

    @pl.when(flags >= 256)
    def _():
        key_idx = lax.broadcasted_iota(jnp.int32, (groups, 8, 8, tq), 0) * 64 + (
            lax.broadcasted_iota(jnp.int32, (groups, 8, 8, tq), 1) * 8
            + lax.broadcasted_iota(jnp.int32, (groups, 8, 8, tq), 2))

        def count_where(pred):
            def body(j, cnt):
                hit = pred(chunk_scores(j), j * KEY_CHUNK + key_idx)
                return cnt + jnp.sum(hit.astype(jnp.int32), axis=0)
            cnt = lax.fori_loop(0, n_chunks, body, jnp.zeros((8, 8, tq), jnp.int32))
            return jnp.sum(jnp.sum(cnt, axis=0), axis=0, keepdims=True)

        wanted = k_row - count_where(lambda kk, idx: kk > thr)

        def index_step(_, bounds):
            below, cap = bounds
            mid = below + lax.shift_right_logical(cap - below, 1)
            enough = count_where(lambda kk, idx: jnp.logical_and(kk == thr, idx <= mid)) >= wanted
            return jnp.where(enough, below, mid), jnp.where(enough, mid, cap)

        n_keys = n_chunks * KEY_CHUNK
        steps = max(1, (sc_ref.shape[0] - 1).bit_length())
        _, cap = lax.fori_loop(0, steps, index_step,
                               (jnp.full((1, tq), -1, jnp.int32), jnp.full((1, tq), 1, jnp.int32) * (n_keys - 1)))

        def drop_chunk(j, carry):
            r0 = pl.multiple_of(FRONT + j * KEY_CHUNK, FRONT)
            kk = chunk_scores(j)
            extra = jnp.logical_and(jnp.logical_and(kk == thr, j * KEY_CHUNK + key_idx > cap), tied)
            sc_ref[pl.ds(r0, KEY_CHUNK), :] = jnp.where(extra, -jnp.inf, kk).reshape(KEY_CHUNK, tq)
            return carry

        lax.fori_loop(0, n_chunks, drop_chunk, 0)

    qlt = qlt_ref[...]
    head_cols = [slice(h * tq, (h + 1) * tq) for h in range(n_heads)]

    def masked_scores(rows, bias, s_ref):
        bias2 = jnp.concatenate([bias, bias], axis=1)
        cmax = []
        for hp in range(n_heads // 2):
            cols = slice(2 * hp * tq, (2 * hp + 2) * tq)
            sm = jnp.dot(c_ref[rows, :], qlt[:, cols], preferred_element_type=F32) + bias2
            s_ref[:, cols] = sm
            cmax.append(jnp.max(sm, axis=0, keepdims=True))
        return jnp.concatenate(cmax, axis=1)

    def accumulate(rows, s_ref, cmax):
        m_old = m_ref[...]
        m_new = jnp.maximum(m_old, cmax)
        m_ref[...] = m_new
        p = jnp.exp2(s_ref[...] - m_new).astype(BF16)
        acc_ref[...] = jnp.exp2(m_old - m_new) * acc_ref[...] + jnp.dot(
            ct_ref[:, rows], p, preferred_element_type=F32)

    m_ref[...] = jnp.full(m_ref.shape, NEG_BIG, F32)
    acc_ref[...] = jnp.zeros_like(acc_ref)

    q_row = i * tq + q_loc
    allowed0 = jnp.logical_and(jnp.logical_or(k_loc >= PAD_ROWS, k_loc == q_row), k_loc <= q_row)
    s0_ref = sa_ref.at[0:FRONT, :]
    cmax0 = masked_scores(slice(0, FRONT), jnp.where(allowed0, 0.0, NEG_BIG), s0_ref)
    accumulate(slice(0, FRONT), s0_ref, cmax0)

    def chunk_rows(j):
        return pl.ds(pl.multiple_of(FRONT + j * KEY_CHUNK, FRONT), KEY_CHUNK)

    def chunk_scores_masked(j, s_ref):
        rows = chunk_rows(j)
        return masked_scores(rows, jnp.where(sc_ref[rows, :] >= thr, 0.0, NEG_BIG), s_ref)

    @pl.when(i > 0)
    def _():
        def chunk_pair(t, cmax_a):
            j = 2 * t
            cmax_b = chunk_scores_masked(j + 1, sb_ref)
            accumulate(chunk_rows(j), sa_ref, cmax_a)
            cmax_a = chunk_scores_masked(j + 2, sa_ref)
            accumulate(chunk_rows(j + 1), sb_ref, cmax_b)
            return cmax_a

        n_pairs = (n_chunks - 1) // 2
        cmax_a = lax.fori_loop(0, n_pairs, chunk_pair, chunk_scores_masked(0, sa_ref))
        j = 2 * n_pairs

        @pl.when(n_chunks - j == 1)
        def _():
            accumulate(chunk_rows(j), sa_ref, cmax_a)

        @pl.when(n_chunks - j == 2)
        def _():
            cmax_b = chunk_scores_masked(j + 1, sb_ref)
            accumulate(chunk_rows(j), sa_ref, cmax_a)
            accumulate(chunk_rows(j + 1), sb_ref, cmax_b)

    o_lat = (acc_ref[0:KV_RANK, :] / acc_ref[KV_RANK:KV_RANK + 1, :]).astype(BF16)
    for h in range(n_heads):
        oh = jnp.dot(wuvt_ref[h], o_lat[:, head_cols[h]], preferred_element_type=F32)
        o_ref[:, h * ATT_HEAD_DIM:(h + 1) * ATT_HEAD_DIM] = oh.T.astype(BF16)


def _dsa(pt, c, ct, ik, wuk, wuvt, k_sel, q_row0, iq_row0, iw_row0):
    m = c.shape[0]
    tq = Q_BLOCK
    d_q = ATT_HEADS * ATT_HEAD_DIM
    d_iq = IDX_HEADS * IDX_DIM
    full2 = lambda shape: pl.BlockSpec(shape, lambda i: (0, 0))
    full3 = lambda shape: pl.BlockSpec(shape, lambda i: (0, 0, 0))
    return pl.pallas_call(
        functools.partial(_dsa_kernel, k_sel=k_sel),
        grid=(m // tq,),
        in_specs=[pl.BlockSpec((d_q, tq), lambda i: (q_row0 // d_q, i)),
                  pl.BlockSpec((d_iq, tq), lambda i: (iq_row0 // d_iq, i)),
                  pl.BlockSpec((IDX_HEADS, tq), lambda i: (iw_row0 // IDX_HEADS, i)),
                  full2(c.shape), full2(ct.shape), full2(ik.shape),
                  full3(wuk.shape), full3(wuvt.shape)],
        out_specs=pl.BlockSpec((tq, d_q), lambda i: (i, 0)),
        out_shape=jax.ShapeDtypeStruct((m, d_q), BF16),
        scratch_shapes=[pltpu.VMEM((m, tq), F32),
                        pltpu.VMEM((IDX_HEADS // 2, 2 * IDX_DIM, 2 * tq), BF16),
                        pltpu.VMEM((KV_RANK, ATT_HEADS * tq), BF16),
                        pltpu.VMEM((KV_RANK + ONES_ROWS, ATT_HEADS * tq), F32),
                        pltpu.VMEM((1, ATT_HEADS * tq), F32),
                        pltpu.VMEM((KEY_CHUNK, ATT_HEADS * tq), F32),
                        pltpu.VMEM((KEY_CHUNK, ATT_HEADS * tq), F32)],
        compiler_params=_params(("arbitrary",)),
        name="dsa",
    )(pt, pt, pt, c, ct, ik, wuk, wuvt)


def _pad_cols(w, n):
    return jnp.pad(w, ((0, 0), (0, n - w.shape[1])))


def kernel(x, meta_tokens, ln_mix_pre, ln_mix_post, ln_ffn_pre, ln_ffn_post, ffn_w1, ffn_w3, ffn_w2,
           ab_w_in, ab_w_out, hgrn_lb_logits, hgrn_out_norm, sconv_w,
           cd_w_in, cd_w_out, rg_conv_w, rg_conv_b, rg_w_a, rg_b_a, rg_w_i, rg_b_i, rg_lambda,
           mla_kv_norm, mla_w_uk, mla_w_uv):
    assert x.shape[0] == 1
    seq, d = x.shape[1], x.shape[2]
    assert seq % KEY_CHUNK == 0
    d_a = hgrn_lb_logits.shape[1]
    d_b = sconv_w.shape[2]
    d_c = rg_lambda.shape[1]
    d_d = ATT_HEADS * ATT_HEAD_DIM
    d_iq = IDX_HEADS * IDX_DIM
    k_sel = min(TOPK_MAX, seq // 4)

    x2 = x[0]
    front = jnp.concatenate([jnp.zeros((PAD_ROWS, d), F32), meta_tokens.astype(F32)], axis=0)
    w1, w3, w2 = ffn_w1.astype(BF16), ffn_w3.astype(BF16), ffn_w2.astype(BF16)

    p0 = _norm_mm(x2, ln_mix_pre[0], ab_w_in[0].astype(BF16), tn=512, front=front)
    og = _hgrn(p0, hgrn_lb_logits, hgrn_out_norm[0], lb_row=0)
    w_out = ab_w_out[0].astype(BF16)
    h = _out_proj_conv(og, p0, sconv_w[0], 4 * d_a, w_out[:d_a], w_out[d_a:], x2, front, ln_mix_post[0])
    h = _ffn(h, ln_ffn_pre[0], w1, w3, w2, ln_ffn_post[0], layer=0)

    w_in = cd_w_in[0]
    o_rx, o_ry, o_q, o_c = 0, d_c, 2 * d_c, 2 * d_c + d_d
    o_iq = o_c + KV_RANK
    o_ik = o_iq + d_iq
    o_iw = o_ik + IDX_DIM
    w_rows = jnp.concatenate([w_in[:, o_rx:o_q], w_in[:, o_c:o_iq], w_in[:, o_ik:o_iw]], axis=1)
    w_rows = _pad_cols(w_rows, -(-w_rows.shape[1] // 512) * 512).astype(BF16)
    w_cols = jnp.concatenate([w_in[:, o_q:o_c], w_in[:, o_iq:o_ik], w_in[:, o_c:o_iq], w_in[:, o_iw:]], axis=1)
    w_cols = _pad_cols(w_cols, -(-w_cols.shape[1] // 512) * 512).astype(BF16).T
    p1 = _norm_mm(h, ln_mix_pre[1], w_rows, tn=w_rows.shape[1] // 4)
    p1t = _norm_mm_t(h, ln_mix_pre[1], w_cols, tc=w_cols.shape[0] // 4)

    hc = _rglru(p1, rg_conv_w[0], rg_conv_b[0], rg_w_a[0].astype(BF16), rg_b_a[0],
                rg_w_i[0].astype(BF16), rg_b_i[0], rg_lambda[0], width=d_c)
    c, ct = _latent(p1, p1t, mla_kv_norm[0], col0=2 * d_c, row0=d_d + d_iq)
    ik = p1[:, 2 * d_c + KV_RANK:2 * d_c + KV_RANK + 2 * IDX_DIM].astype(BF16)
    wuk = jnp.transpose(mla_w_uk[0], (1, 0, 2)).astype(BF16)
    wuvt = jnp.transpose(mla_w_uv[0], (1, 2, 0)).astype(BF16)
    att = _dsa(p1t, c, ct, ik, wuk, wuvt, k_sel,
               q_row0=0, iq_row0=d_d, iw_row0=d_d + d_iq + KV_RANK)
    w_out = cd_w_out[0].astype(BF16)
    h = _out_proj(hc, att, w_out[:d_c], w_out[d_c:], h, ln_mix_post[1])
    return _ffn(h, ln_ffn_pre[1], w1, w3, w2, ln_ffn_post[1], layer=1, skip_front=True)[None]
```

```python
import functools
import math

import jax
import jax.numpy as jnp
from jax import lax
from jax.experimental import pallas as pl
from jax.experimental.pallas import tpu as pltpu

F32 = jnp.float32
BF16 = jnp.bfloat16

EPS = 1e-6
N_META = 16
FRONT = 128
PAD_ROWS = FRONT - N_META
HGRN_HEAD = 128
HGRN_CHUNK = 128
RG_BLOCK = 128
RG_C = 8.0
ATT_HEADS = 8
ATT_HEAD_DIM = 128
KV_RANK = 256
IDX_HEADS = 16
IDX_DIM = 64
TOPK_MAX = 256
Q_BLOCK = 128
KEY_CHUNK = 512
ONES_ROWS = 16
LOG2_E = 1.4426950408889634
BISECT_STEPS = 4
BISECT_BLIND = 3
BISECT_CAP = 64
NEG_BIG = -1e30
VMEM_LIMIT = 56 * 1024 * 1024
PROJ_ROWS = 1664


def _pick_tile(n, target, mult):
    best = None
    for t in range(mult, min(n, target) + 1, mult):
        if n % t == 0:
            best = t
    assert best is not None, (n, target, mult)
    return best


def _params(sem):
    return pltpu.CompilerParams(dimension_semantics=sem, vmem_limit_bytes=VMEM_LIMIT)


def _rms(x, g):
    return x * lax.rsqrt(jnp.mean(x * x, axis=-1, keepdims=True) + EPS) * g


def _token_specs(tm, d):
    align = math.gcd(tm, FRONT)
    x_map = lambda *a: (pl.multiple_of(jnp.maximum(a[0] * tm - FRONT, 0), align), 0)
    return [pl.BlockSpec((pl.Element(tm), pl.Element(d)), x_map),
            pl.BlockSpec((FRONT, d), lambda *a: (0, 0))]


def _fill_normed(x_ref, front_ref, g_ref, xn_ref):
    g = g_ref[...]
    first = 0 if front_ref is None else (pl.program_id(0) == 0).astype(jnp.int32)

    def piece(p, carry):
        src = pl.multiple_of(jnp.maximum(p - first, 0) * FRONT, FRONT)
        rows = x_ref[pl.ds(src, FRONT), :]
        if front_ref is not None:
            rows = jnp.where(jnp.logical_and(first == 1, p == 0), front_ref[...], rows)
        xn_ref[pl.ds(pl.multiple_of(p * FRONT, FRONT), FRONT), :] = _rms(rows, g).astype(BF16)
        return carry

    lax.fori_loop(0, xn_ref.shape[0] // FRONT, piece, 0)


def _norm_mm_kernel(x_ref, *rest, tokens):
    front_ref = rest[0] if tokens else None
    g_ref, w_ref, o_ref, xn_ref = rest[-4:]

    @pl.when(pl.program_id(1) == 0)
    def _():
        _fill_normed(x_ref, front_ref, g_ref, xn_ref)

    o_ref[...] = jnp.dot(xn_ref[...], w_ref[...], preferred_element_type=F32)


def _norm_mm(x, g, w, tn=512, front=None):
    d = x.shape[1]
    m = x.shape[0] + (0 if front is None else FRONT)
    n = w.shape[1]
    tm = _pick_tile(m, PROJ_ROWS, FRONT)
    if front is None:
        row_specs, rows = [pl.BlockSpec((tm, d), lambda i, j: (i, 0))], (x,)
    else:
        row_specs, rows = _token_specs(tm, d), (x, front)
    return pl.pallas_call(
        functools.partial(_norm_mm_kernel, tokens=front is not None),
        grid=(m // tm, n // tn),
        in_specs=row_specs + [pl.BlockSpec((1, d), lambda i, j: (0, 0)),
                              pl.BlockSpec((d, tn), lambda i, j: (0, j))],
        out_specs=pl.BlockSpec((tm, tn), lambda i, j: (i, j)),
        out_shape=jax.ShapeDtypeStruct((m, n), F32),
        scratch_shapes=[pltpu.VMEM((tm, d), BF16)],
        compiler_params=_params(("parallel", "arbitrary")),
        name="norm_proj",
    )(*rows, g.reshape(1, d), w)


def _norm_mm_t_kernel(x_ref, g_ref, wt_ref, o_ref, xn_ref):
    @pl.when(pl.program_id(1) == 0)
    def _():
        _fill_normed(x_ref, None, g_ref, xn_ref)

    o_ref[...] = lax.dot_general(wt_ref[...], xn_ref[...], (((1,), (1,)), ((), ())),
                                 preferred_element_type=F32)


def _norm_mm_t(x, g, wt, tc=512):
    m, d = x.shape
    n = wt.shape[0]
    tm = _pick_tile(m, PROJ_ROWS, FRONT)
    return pl.pallas_call(
        _norm_mm_t_kernel,
        grid=(m // tm, n // tc),
        in_specs=[pl.BlockSpec((tm, d), lambda i, j: (i, 0)),
                  pl.BlockSpec((1, d), lambda i, j: (0, 0)),
                  pl.BlockSpec((tc, d), lambda i, j: (j, 0))],
        out_specs=pl.BlockSpec((tc, tm), lambda i, j: (j, i)),
        out_shape=jax.ShapeDtypeStruct((n, m), F32),
        scratch_shapes=[pltpu.VMEM((tm, d), BF16)],
        compiler_params=_params(("parallel", "arbitrary")),
        name="norm_proj_t",
    )(x, g.reshape(1, d), wt)


def _out_proj_kernel(a_ref, b_ref, wa_ref, wb_ref, g_ref, h_ref, o_ref):
    y = jnp.dot(a_ref[...], wa_ref[...], preferred_element_type=F32)
    y = y + jnp.dot(b_ref[...], wb_ref[...], preferred_element_type=F32)
    o_ref[...] = h_ref[...] + _rms(y, g_ref[...])


def _out_proj(a, b, wa, wb, h, g):
    m, ka = a.shape
    kb = b.shape[1]
    d = wa.shape[1]
    tm = _pick_tile(m, 640, 128)
    return pl.pallas_call(
        _out_proj_kernel,
        grid=(m // tm,),
        in_specs=[pl.BlockSpec((tm, ka), lambda i: (i, 0)),
                  pl.BlockSpec((tm, kb), lambda i: (i, 0)),
                  pl.BlockSpec((ka, d), lambda i: (0, 0)),
                  pl.BlockSpec((kb, d), lambda i: (0, 0)),
                  pl.BlockSpec((1, d), lambda i: (0, 0)),
                  pl.BlockSpec((tm, d), lambda i: (i, 0))],
        out_specs=pl.BlockSpec((tm, d), lambda i: (i, 0)),
        out_shape=jax.ShapeDtypeStruct((m, d), F32),
        compiler_params=_params(("parallel",)),
        name="out_proj",
    )(a, b, wa, wb, g.reshape(1, d), h)


def _out_proj_conv_kernel(a_ref, sx_ref, sb_ref, sc_ref, sxp_ref, scp_ref, cw_ref, wa_ref, wb_ref, g_ref,
                          x_ref, front_ref, o_ref, ext_ref, *, taps):
    tm = a_ref.shape[0]
    i = pl.program_id(0)
    ext_ref[0:8, :] = jnp.where(i > 0, sxp_ref[...] * scp_ref[...], 0.0)
    ext_ref[8:8 + tm, :] = sx_ref[...] * sc_ref[...]
    conv = jnp.zeros(sx_ref.shape, F32)
    for j in range(taps):
        s = 8 - (taps - 1) + j
        conv = conv + cw_ref[j:j + 1, :] * ext_ref[s:s + tm, :]
    yb = (sb_ref[...] * conv).astype(BF16)

    y = jnp.dot(a_ref[...], wa_ref[...], preferred_element_type=F32)
    y = _rms(y + jnp.dot(yb, wb_ref[...], preferred_element_type=F32), g_ref[...])

    @pl.when(i == 0)
    def _():
        o_ref[0:FRONT, :] = front_ref[...] + y[0:FRONT, :]
        o_ref[FRONT:tm, :] = x_ref[0:tm - FRONT, :] + y[FRONT:tm, :]

    @pl.when(i > 0)
    def _():
        o_ref[...] = x_ref[...] + y


def _out_proj_conv(a, p, cw, col0, wa, wb, x, front, g):
    m, ka = a.shape
    taps, width = cw.shape
    d = wa.shape[1]
    tm = _pick_tile(m, 320, 64)
    cb = col0 // width
    prev = lambda off: (lambda i: (jnp.maximum(i * (tm // 8) - 1, 0), off))
    return pl.pallas_call(
        functools.partial(_out_proj_conv_kernel, taps=taps),
        grid=(m // tm,),
        in_specs=[pl.BlockSpec((tm, ka), lambda i: (i, 0)),
                  pl.BlockSpec((tm, width), lambda i: (i, cb)),
                  pl.BlockSpec((tm, width), lambda i: (i, cb + 1)),
                  pl.BlockSpec((tm, width), lambda i: (i, cb + 2)),
                  pl.BlockSpec((8, width), prev(cb)),
                  pl.BlockSpec((8, width), prev(cb + 2)),
                  pl.BlockSpec((taps, width), lambda i: (0, 0)),
                  pl.BlockSpec((ka, d), lambda i: (0, 0)),
                  pl.BlockSpec((width, d), lambda i: (0, 0)),
                  pl.BlockSpec((1, d), lambda i: (0, 0))] + _token_specs(tm, d),
        out_specs=pl.BlockSpec((tm, d), lambda i: (i, 0)),
        out_shape=jax.ShapeDtypeStruct((m, d), F32),
        scratch_shapes=[pltpu.VMEM((tm + 8, width), F32)],
        compiler_params=_params(("parallel",)),
        name="out_proj_conv",
    )(a, p, p, p, p, p, cw, wa, wb, g.reshape(1, d), x, front)


def _ffn_kernel(h_ref, gpre_ref, w1_ref, w3_ref, w2_ref, gpost_ref, o_ref, xn_ref, acc_ref):
    j = pl.program_id(1)

    @pl.when(j == 0)
    def _():
        xn_ref[...] = _rms(h_ref[...], gpre_ref[...]).astype(BF16)
        acc_ref[...] = jnp.zeros_like(acc_ref)

    xn = xn_ref[...]
    a = jnp.dot(xn, w1_ref[...], preferred_element_type=F32)
    b = jnp.dot(xn, w3_ref[...], preferred_element_type=F32)
    u = (a * jax.nn.sigmoid(a) * b).astype(BF16)
    acc_ref[...] += jnp.dot(u, w2_ref[...], preferred_element_type=F32)

    @pl.when(j == pl.num_programs(1) - 1)
    def _():
        o_ref[...] = h_ref[...] + _rms(acc_ref[...], gpost_ref[...])


def _ffn(h, gpre, w1, w3, w2, gpost, layer, tf=512, skip_front=False):
    d = h.shape[1]
    f = w1.shape[2]
    if skip_front:
        m = h.shape[0] - FRONT
        tm = _pick_tile(m, 640, 128)
        h_spec = pl.BlockSpec((pl.Element(tm), pl.Element(d)), lambda i, j: (pl.multiple_of(FRONT + i * tm, FRONT), 0))
    else:
        m = h.shape[0]
        tm = _pick_tile(m, 640, 128)
        h_spec = pl.BlockSpec((tm, d), lambda i, j: (i, 0))
    return pl.pallas_call(
        _ffn_kernel,
        grid=(m // tm, f // tf),
        in_specs=[h_spec,
                  pl.BlockSpec((1, d), lambda i, j: (0, 0)),
                  pl.BlockSpec((None, d, tf), lambda i, j: (layer, 0, j)),
                  pl.BlockSpec((None, d, tf), lambda i, j: (layer, 0, j)),
                  pl.BlockSpec((None, tf, d), lambda i, j: (layer, j, 0)),
                  pl.BlockSpec((1, d), lambda i, j: (0, 0))],
        out_specs=pl.BlockSpec((tm, d), lambda i, j: (i, 0)),
        out_shape=jax.ShapeDtypeStruct((m, d), F32),
        scratch_shapes=[pltpu.VMEM((tm, d), BF16), pltpu.VMEM((tm, d), F32)],
        compiler_params=_params(("parallel", "arbitrary")),
        name="ffn",
    )(h, gpre.reshape(1, d), w1, w3, w2, gpost.reshape(1, d))


def _cumsum_rows(tri, x):
    hi = x.astype(BF16)
    rest = x - hi.astype(F32)
    mid = rest.astype(BF16)
    lo = (rest - mid.astype(F32)).astype(BF16)
    return (jnp.dot(tri, hi, preferred_element_type=F32) + jnp.dot(tri, mid, preferred_element_type=F32)
            + jnp.dot(tri, lo, preferred_element_type=F32))


def _edge_rows(b, half):
    rows = b.shape[0]
    if half >= 8:
        parts = [jnp.broadcast_to(b[e:e + 1, :], (2 * half, b.shape[1]))
                 for e in range(half - 1, rows, 2 * half)]
        return parts[0] if len(parts) == 1 else jnp.concatenate(parts, axis=0)
    b3 = b.reshape(rows // 8, 8, b.shape[1])
    sub = lax.broadcasted_iota(jnp.int32, b3.shape, 1)
    pick = lambda r: jnp.broadcast_to(b3[:, r:r + 1, :], b3.shape)
    edge = pick(half - 1)
    for start in range(2 * half, 8, 2 * half):
        edge = jnp.where(sub >= start, pick(start + half - 1), edge)
    return edge.reshape(b.shape)


def _hgrn_kernel(q_ref, f_ref, v_ref, gate_ref, lbl_ref, gn_ref, o_ref, st_ref, *, n_heads, lb_row):
    c_rows = HGRN_CHUNK

    @pl.when(pl.program_id(1) == 0)
    def _():
        st_ref[...] = jnp.zeros_like(st_ref)

    logits = lbl_ref[...]
    ex = jnp.exp(logits - jnp.max(logits, axis=0, keepdims=True))
    lb_all = jnp.sum(ex[0:lb_row + 1, :], axis=0, keepdims=True) / jnp.sum(ex, axis=0, keepdims=True)

    r_i = lax.broadcasted_iota(jnp.int32, (c_rows, c_rows), 0)
    c_i = lax.broadcasted_iota(jnp.int32, (c_rows, c_rows), 1)
    tri = (r_i >= c_i).astype(BF16)
    levels = []
    half = c_rows // 2
    while half >= 1:
        shift = half.bit_length()
        upper = (lax.shift_right_logical(r_i, shift - 1) & 1) == 1
        same = lax.shift_right_logical(r_i, shift) == lax.shift_right_logical(c_i, shift)
        levels.append((half, upper, jnp.where(upper, 1.0, -1.0), same))
        half //= 2

    for hh in range(n_heads):
        cols = slice(hh * HGRN_HEAD, (hh + 1) * HGRN_HEAD)
        lb = lb_all[:, cols]
        q = q_ref[:, cols]
        v = v_ref[:, cols]
        f = lb + (1.0 - lb) * jax.nn.sigmoid(f_ref[:, cols])
        k = 1.0 - f
        b = _cumsum_rows(tri, jnp.log2(f))
        b_last = b[c_rows - 1:c_rows, :]

        st = st_ref[hh]
        inter = lax.dot_general((q * jnp.exp2(b)).astype(BF16), st.astype(BF16),
                                (((1,), (1,)), ((), ())), preferred_element_type=F32)
        kt = (k * jnp.exp2(b_last - b)).astype(BF16)
        st_ref[hh] = st * jnp.exp2(b_last) + lax.dot_general(
            v.astype(BF16), kt, (((0,), (0,)), ((), ())), preferred_element_type=F32)

        att = jnp.zeros((c_rows, c_rows), F32)
        for half, upper, sign, same in levels:
            decay = jnp.exp2((b - _edge_rows(b, half)) * sign)
            scaled = jnp.where(upper, q, k) * decay
            qt = jnp.where(upper, scaled, 0.0).astype(BF16)
            kl = jnp.where(upper, 0.0, scaled).astype(BF16)
            pair = lax.dot_general(qt, kl, (((1,), (1,)), ((), ())), preferred_element_type=F32)
            att = att + jnp.where(same, pair, 0.0)
        o = inter + jnp.dot(att.astype(BF16), v.astype(BF16), preferred_element_type=F32)
        o = o + jnp.sum(q * k, axis=-1, keepdims=True) * v

        gate = gate_ref[:, cols]
        o_ref[:, cols] = (_rms(o, gn_ref[...]) * (gate * jax.nn.sigmoid(gate))).astype(BF16)


def _hgrn(p, lb_logits, gn, lb_row):
    m = p.shape[0]
    d_a = lb_logits.shape[1]
    hp = 8
    width = hp * HGRN_HEAD
    groups = d_a // width
    tb = HGRN_CHUNK
    n_l = lb_logits.shape[0]
    col = lambda off: (lambda g, t: (t, off + g))
    kern = functools.partial(_hgrn_kernel, n_heads=hp, lb_row=lb_row)
    return pl.pallas_call(
        kern,
        grid=(groups, m // tb),
        in_specs=[pl.BlockSpec((tb, width), col(0)),
                  pl.BlockSpec((tb, width), col(groups)),
                  pl.BlockSpec((tb, width), col(2 * groups)),
                  pl.BlockSpec((tb, width), col(3 * groups)),
                  pl.BlockSpec((n_l, width), lambda g, t: (0, g)),
                  pl.BlockSpec((1, HGRN_HEAD), lambda g, t: (0, 0))],
        out_specs=pl.BlockSpec((tb, width), lambda g, t: (t, g)),
        out_shape=jax.ShapeDtypeStruct((m, d_a), BF16),
        scratch_shapes=[pltpu.VMEM((hp, HGRN_HEAD, HGRN_HEAD), F32)],
        compiler_params=_params(("parallel", "arbitrary")),
        name="hgrn2",
    )(p, p, p, p, lb_logits, gn.reshape(1, HGRN_HEAD))


def _rglru_kernel(rx_ref, ry_ref, rxp_ref, cw_ref, cb_ref, wa_ref, ba_ref, wi_ref, bi_ref, lam_ref,
                  o_ref, ext_ref, a_ref, x_ref, hs_ref, h_ref, *, taps):
    tm, width = rx_ref.shape
    i = pl.program_id(0)

    @pl.when(i == 0)
    def _():
        h_ref[...] = jnp.zeros_like(h_ref)

    ext_ref[0:8, :] = jnp.where(i > 0, rxp_ref[...], 0.0)
    ext_ref[8:8 + tm, :] = rx_ref[...]
    u = jnp.zeros((tm, width), F32) + cb_ref[...]
    for j in range(taps):
        s = 8 - (taps - 1) + j
        u = u + cw_ref[j:j + 1, :] * ext_ref[s:s + tm, :]

    u_b = u.astype(BF16)
    r_parts, i_parts = [], []
    for n in range(width // RG_BLOCK):
        blk = slice(n * RG_BLOCK, (n + 1) * RG_BLOCK)
        r_parts.append(jnp.dot(u_b[:, blk], wa_ref[n], preferred_element_type=F32))
        i_parts.append(jnp.dot(u_b[:, blk], wi_ref[n], preferred_element_type=F32))
    r = jax.nn.sigmoid(jnp.concatenate(r_parts, axis=1) + ba_ref[...])
    ig = jax.nn.sigmoid(jnp.concatenate(i_parts, axis=1) + bi_ref[...])

    neg_lam = -lam_ref[...]
    softplus = jnp.maximum(neg_lam, 0.0) + jnp.log1p(jnp.exp(-jnp.abs(neg_lam)))
    log_a = -RG_C * r * softplus
    row = i * tm + lax.broadcasted_iota(jnp.int32, (tm, 1), 0)
    a = jnp.exp(log_a)
    xin = jnp.sqrt(1.0 - a * a) * (ig * u)
    a_ref[...] = a
    x_ref[...] = jnp.where(row >= PAD_ROWS, xin, 0.0)

    sub = lax.broadcasted_iota(jnp.int32, (8, width), 0)

    def group(gidx, h):
        base = pl.multiple_of(gidx * 8, 8)
        a8 = a_ref[pl.ds(base, 8), :]
        x8 = x_ref[pl.ds(base, 8), :]
        for shift in (1, 2, 4):
            keep = sub >= shift
            a_up = jnp.where(keep, pltpu.roll(a8, shift, axis=0), 1.0)
            x_up = jnp.where(keep, pltpu.roll(x8, shift, axis=0), 0.0)
            x8 = a8 * x_up + x8
            a8 = a8 * a_up
        h8 = a8 * h + x8
        hs_ref[pl.ds(base, 8), :] = h8
        return h8[7:8, :]

    h_ref[...] = lax.fori_loop(0, tm // 8, group, h_ref[...])
    o_ref[...] = (hs_ref[...] * jax.nn.gelu(ry_ref[...])).astype(BF16)


def _rglru(p, cw, cb, wa, ba, wi, bi, lam, width):
    m = p.shape[0]
    taps = cw.shape[0]
    tm = _pick_tile(m, 640, 128)
    nb = width // RG_BLOCK
    row = lambda v: v.reshape(1, width)
    full2 = lambda shape: pl.BlockSpec(shape, lambda i: (0, 0))
    full3 = lambda shape: pl.BlockSpec(shape, lambda i: (0, 0, 0))
    return pl.pallas_call(
        functools.partial(_rglru_kernel, taps=taps),
        grid=(m // tm,),
        in_specs=[pl.BlockSpec((tm, width), lambda i: (i, 0)),
                  pl.BlockSpec((tm, width), lambda i: (i, 1)),
                  pl.BlockSpec((8, width), lambda i: (jnp.maximum(i * (tm // 8) - 1, 0), 0)),
                  full2((taps, width)), full2((1, width)),
                  full3((nb, RG_BLOCK, RG_BLOCK)), full2((1, width)),
                  full3((nb, RG_BLOCK, RG_BLOCK)), full2((1, width)),
                  full2((1, width))],
        out_specs=pl.BlockSpec((tm, width), lambda i: (i, 0)),
        out_shape=jax.ShapeDtypeStruct((m, width), BF16),
        scratch_shapes=[pltpu.VMEM((tm + 8, width), F32),
                        pltpu.VMEM((tm, width), F32),
                        pltpu.VMEM((tm, width), F32),
                        pltpu.VMEM((tm, width), F32),
                        pltpu.VMEM((1, width), F32)],
        compiler_params=_params(("arbitrary",)),
        name="rglru",
    )(p, p, p, cw, row(cb), wa, row(ba), wi, row(bi), row(lam))


def _latent_kernel(c_ref, ct_ref, g_ref, gt_ref, o_ref, ot_ref):
    o_ref[...] = _rms(c_ref[...], g_ref[...]).astype(BF16)
    ct = ct_ref[...]
    ot_ref[0:KV_RANK, :] = (ct * lax.rsqrt(jnp.mean(ct * ct, axis=0, keepdims=True) + EPS)
                            * gt_ref[...]).astype(BF16)
    first = lax.broadcasted_iota(jnp.int32, (ONES_ROWS, ct.shape[1]), 0) == 0
    ot_ref[KV_RANK:KV_RANK + ONES_ROWS, :] = jnp.where(first, 1.0, 0.0).astype(BF16)


def _latent(p, pt, g, col0, row0):
    m = p.shape[0]
    r = g.shape[0]
    tm = _pick_tile(m, 640, 128)
    return pl.pallas_call(
        _latent_kernel,
        grid=(m // tm,),
        in_specs=[pl.BlockSpec((tm, r), lambda i: (i, col0 // r)),
                  pl.BlockSpec((r, tm), lambda i: (row0 // r, i)),
                  pl.BlockSpec((1, r), lambda i: (0, 0)),
                  pl.BlockSpec((r, 1), lambda i: (0, 0))],
        out_specs=[pl.BlockSpec((tm, r), lambda i: (i, 0)),
                   pl.BlockSpec((r + ONES_ROWS, tm), lambda i: (0, i))],
        out_shape=[jax.ShapeDtypeStruct((m, r), BF16), jax.ShapeDtypeStruct((r + ONES_ROWS, m), BF16)],
        compiler_params=_params(("parallel",)),
        name="latent_norm",
    )(p, pt, g.reshape(1, r), g.reshape(r, 1))


def _dsa_kernel(qt_ref, iqt_ref, iwt_ref, c_ref, ct_ref, ik_ref, wuk_ref, wuvt_ref, o_ref,
                sc_ref, iqp_ref, qlt_ref, acc_ref, m_ref, sa_ref, sb_ref, *, k_sel):
    i = pl.program_id(0)
    tq = Q_BLOCK
    n_heads = ATT_HEADS
    sub_blocks = KEY_CHUNK // tq
    n_chunks = (i * tq + KEY_CHUNK - 1) // KEY_CHUNK

    scale = ATT_HEAD_DIM ** -0.5 * LOG2_E
    for h in range(n_heads):
        qh = qt_ref[h * ATT_HEAD_DIM:(h + 1) * ATT_HEAD_DIM, :].astype(BF16)
        ql = jnp.dot(wuk_ref[h], qh, preferred_element_type=F32) * scale
        qlt_ref[:, h * tq:(h + 1) * tq] = ql.astype(BF16)

    k_loc = lax.broadcasted_iota(jnp.int32, (tq, tq), 0)
    q_loc = lax.broadcasted_iota(jnp.int32, (tq, tq), 1)
    head_cols = [slice(h * tq, (h + 1) * tq) for h in range(n_heads)]

    def masked_scores(rows, bias, s_ref):
        bias2 = jnp.concatenate([bias, bias], axis=1)
        cmax = []
        for hp in range(n_heads // 2):
            cols = slice(2 * hp * tq, (2 * hp + 2) * tq)
            sm = jnp.dot(c_ref[rows, :], qlt_ref[:, cols], preferred_element_type=F32) + bias2
            s_ref[:, cols] = sm
            cmax.append(jnp.max(sm, axis=0, keepdims=True))
        return jnp.concatenate(cmax, axis=1)

    def accumulate(rows, s_ref, cmax):
        m_old = m_ref[...]
        m_new = jnp.maximum(m_old, cmax)
        m_ref[...] = m_new
        p = jnp.exp2(s_ref[...] - m_new).astype(BF16)
        acc_ref[...] = jnp.exp2(m_old - m_new) * acc_ref[...] + jnp.dot(
            ct_ref[:, rows], p, preferred_element_type=F32)

    m_ref[...] = jnp.full(m_ref.shape, NEG_BIG, F32)
    acc_ref[...] = jnp.zeros_like(acc_ref)

    q_row = i * tq + q_loc
    allowed0 = jnp.logical_and(jnp.logical_or(k_loc >= PAD_ROWS, k_loc == q_row), k_loc <= q_row)
    s0_ref = sa_ref.at[0:FRONT, :]
    cmax0 = masked_scores(slice(0, FRONT), jnp.where(allowed0, 0.0, NEG_BIG), s0_ref)
    accumulate(slice(0, FRONT), s0_ref, cmax0)

    iw = iwt_ref[...] * ((IDX_DIM ** -0.5) * (IDX_HEADS ** -0.5))

    for hp in range(IDX_HEADS // 2):
        pair = jnp.concatenate(
            [iqt_ref[(2 * hp) * IDX_DIM:(2 * hp + 1) * IDX_DIM, :],
             iqt_ref[(2 * hp + 1) * IDX_DIM:(2 * hp + 2) * IDX_DIM, :]], axis=1).astype(BF16)
        iqp_ref[hp, 0:IDX_DIM, :] = pair
        iqp_ref[hp, IDX_DIM:2 * IDX_DIM, :] = jnp.zeros_like(pair)

    def score_chunk(j, carry):
        mn, mx = carry
        for u in range(sub_blocks):
            kb = j * sub_blocks + 1 + u
            r0 = pl.multiple_of(kb * tq, tq)
            ikb = ik_ref[pl.ds(r0, tq), :]
            s = jnp.zeros((tq, tq), F32)
            for hp in range(IDX_HEADS // 2):
                x = jnp.dot(ikb, iqp_ref[hp], preferred_element_type=F32)
                s = s + jnp.maximum(x[:, 0:tq], 0.0) * iw[2 * hp:2 * hp + 1, :]
                s = s + jnp.maximum(x[:, tq:2 * tq], 0.0) * iw[2 * hp + 1:2 * hp + 2, :]
            visible = jnp.logical_or(kb < i, jnp.logical_and(kb == i, k_loc <= q_loc))
            sc_ref[pl.ds(r0, tq), :] = jnp.where(visible, s, -jnp.inf)
            mx = jnp.maximum(mx, jnp.max(jnp.where(visible, s, -jnp.inf).reshape(tq // 8, 8, tq), axis=0))
            mn = jnp.minimum(mn, jnp.min(jnp.where(visible, s, jnp.inf).reshape(tq // 8, 8, tq), axis=0))
        return mn, mx

    mn, mx = lax.fori_loop(0, n_chunks, score_chunk,
                           (jnp.full((8, tq), jnp.inf, F32), jnp.full((8, tq), -jnp.inf, F32)))
    row_min = jnp.min(mn, axis=0, keepdims=True)
    row_max = jnp.max(mx, axis=0, keepdims=True)

    groups = KEY_CHUNK // 64

    def chunk_scores(j):
        r0 = pl.multiple_of(FRONT + j * KEY_CHUNK, FRONT)
        return sc_ref[pl.ds(r0, KEY_CHUNK), :].reshape(groups, 8, 8, tq)

    def count_ge(t):
        def body(j, cnt):
            kk = chunk_scores(j)
            for g in range(groups):
                cnt = jnp.where(kk[g] >= t, cnt + 1, cnt)
            return cnt
        cnt = lax.fori_loop(0, n_chunks, body, jnp.zeros((8, 8, tq), jnp.int32))
        return jnp.sum(jnp.sum(cnt, axis=0), axis=0, keepdims=True)

    lane = lax.broadcasted_iota(jnp.int32, (1, tq), 1)
    n_visible = (i - 1) * tq + lane + 1
    k_row = jnp.minimum(k_sel, n_visible)

    def probe(mid, movable, lo, hi, cnt_lo):
        c = count_ge(mid)
        up = jnp.logical_and(c >= k_row, movable)
        down = jnp.logical_and(jnp.logical_not(up), movable)
        return jnp.where(up, mid, lo), jnp.where(down, mid, hi), jnp.where(up, c, cnt_lo)

    def midpoint(lo, hi, cnt_lo):
        mid = lo + 0.5 * (hi - lo)
        movable = jnp.logical_and(cnt_lo != k_row, jnp.logical_and(mid > lo, mid < hi))
        return mid, movable

    def lane_flags(lo, hi, cnt_lo):
        moving = midpoint(lo, hi, cnt_lo)[1].astype(jnp.int32)
        return jnp.sum(moving + 256 * (cnt_lo > k_row).astype(jnp.int32))

    lo, hi, cnt_lo = probe(row_max, n_visible > k_row, row_min, row_max, n_visible)

    def bisect_cond(state):
        return jnp.logical_and(state[0] % 256 > 0, state[1] < BISECT_CAP)

    def bisect_steps(_, bracket):
        lo, hi, cnt_lo = bracket
        for _ in range(BISECT_STEPS):
            mid, movable = midpoint(lo, hi, cnt_lo)
            lo, hi, cnt_lo = probe(mid, movable, lo, hi, cnt_lo)
        return lo, hi, cnt_lo

    def bisect_body(state):
        lo, hi, cnt_lo = bisect_steps(0, state[2:])
        return lane_flags(lo, hi, cnt_lo), state[1] + 1, lo, hi, cnt_lo

    lo, hi, cnt_lo = lax.fori_loop(0, BISECT_BLIND, bisect_steps, (lo, hi, cnt_lo))
    state = (lane_flags(lo, hi, cnt_lo), jnp.int32(0), lo, hi, cnt_lo)
    flags, _, thr, _, cnt_thr = lax.while_loop(bisect_cond, bisect_body, state)

    tied = cnt_thr > k_row

    @pl.when(flags >= 256)
    def _():
        key_idx = lax.broadcasted_iota(jnp.int32, (groups, 8, 8, tq), 0) * 64 + (
            lax.broadcasted_iota(jnp.int32, (groups, 8, 8, tq), 1) * 8
            + lax.broadcasted_iota(jnp.int32, (groups, 8, 8, tq), 2))

        def count_where(pred):
            def body(j, cnt):
                hit = pred(chunk_scores(j), j * KEY_CHUNK + key_idx)
                return cnt + jnp.sum(hit.astype(jnp.int32), axis=0)
            cnt = lax.fori_loop(0, n_chunks, body, jnp.zeros((8, 8, tq), jnp.int32))
            return jnp.sum(jnp.sum(cnt, axis=0), axis=0, keepdims=True)

        wanted = k_row - count_where(lambda kk, idx: kk > thr)

        def index_step(_, bounds):
            below, cap = bounds
            mid = below + lax.shift_right_logical(cap - below, 1)
            enough = count_where(lambda kk, idx: jnp.logical_and(kk == thr, idx <= mid)) >= wanted
            return jnp.where(enough, below, mid), jnp.where(enough, mid, cap)

        n_keys = n_chunks * KEY_CHUNK
        steps = max(1, (sc_ref.shape[0] - 1).bit_length())
        _, cap = lax.fori_loop(0, steps, index_step,
                               (jnp.full((1, tq), -1, jnp.int32), jnp.full((1, tq), 1, jnp.int32) * (n_keys - 1)))

        def drop_chunk(j, carry):
            r0 = pl.multiple_of(FRONT + j * KEY_CHUNK, FRONT)
            kk = chunk_scores(j)
            extra = jnp.logical_and(jnp.logical_and(kk == thr, j * KEY_CHUNK + key_idx > cap), tied)
            sc_ref[pl.ds(r0, KEY_CHUNK), :] = jnp.where(extra, -jnp.inf, kk).reshape(KEY_CHUNK, tq)
            return carry

        lax.fori_loop(0, n_chunks, drop_chunk, 0)

    def chunk_rows(j):
        return pl.ds(pl.multiple_of(FRONT + j * KEY_CHUNK, FRONT), KEY_CHUNK)

    def chunk_scores_masked(j, s_ref):
        rows = chunk_rows(j)
        return masked_scores(rows, jnp.where(sc_ref[rows, :] >= thr, 0.0, NEG_BIG), s_ref)

    @pl.when(i > 0)
    def _():
        def chunk_pair(t, cmax_a):
            j = 2 * t
            cmax_b = chunk_scores_masked(j + 1, sb_ref)
            accumulate(chunk_rows(j), sa_ref, cmax_a)
            cmax_a = chunk_scores_masked(j + 2, sa_ref)
            accumulate(chunk_rows(j + 1), sb_ref, cmax_b)
            return cmax_a

        n_pairs = (n_chunks - 1) // 2
        cmax_a = lax.fori_loop(0, n_pairs, chunk_pair, chunk_scores_masked(0, sa_ref))
        j = 2 * n_pairs

        @pl.when(n_chunks - j == 1)
        def _():
            accumulate(chunk_rows(j), sa_ref, cmax_a)

        @pl.when(n_chunks - j == 2)
        def _():
            cmax_b = chunk_scores_masked(j + 1, sb_ref)
            accumulate(chunk_rows(j), sa_ref, cmax_a)
            accumulate(chunk_rows(j + 1), sb_ref, cmax_b)

    o_lat = (acc_ref[0:KV_RANK, :] / acc_ref[KV_RANK:KV_RANK + 1, :]).astype(BF16)
    for h in range(n_heads):
        oh = jnp.dot(wuvt_ref[h], o_lat[:, head_cols[h]], preferred_element_type=F32)
        o_ref[:, h * ATT_HEAD_DIM:(h + 1) * ATT_HEAD_DIM] = oh.T.astype(BF16)


def _dsa(pt, c, ct, ik, wuk, wuvt, k_sel, q_row0, iq_row0, iw_row0):
    m = c.shape[0]
    tq = Q_BLOCK
    d_q = ATT_HEADS * ATT_HEAD_DIM
    d_iq = IDX_HEADS * IDX_DIM
    full2 = lambda shape: pl.BlockSpec(shape, lambda i: (0, 0))
    full3 = lambda shape: pl.BlockSpec(shape, lambda i: (0, 0, 0))
    return pl.pallas_call(
        functools.partial(_dsa_kernel, k_sel=k_sel),
        grid=(m // tq,),
        in_specs=[pl.BlockSpec((d_q, tq), lambda i: (q_row0 // d_q, i)),
                  pl.BlockSpec((d_iq, tq), lambda i: (iq_row0 // d_iq, i)),
                  pl.BlockSpec((IDX_HEADS, tq), lambda i: (iw_row0 // IDX_HEADS, i)),
                  full2(c.shape), full2(ct.shape), full2(ik.shape),
                  full3(wuk.shape), full3(wuvt.shape)],
        out_specs=pl.BlockSpec((tq, d_q), lambda i: (i, 0)),
        out_shape=jax.ShapeDtypeStruct((m, d_q), BF16),
        scratch_shapes=[pltpu.VMEM((m, tq), F32),
                        pltpu.VMEM((IDX_HEADS // 2, 2 * IDX_DIM, 2 * tq), BF16),
                        pltpu.VMEM((KV_RANK, ATT_HEADS * tq), BF16),
                        pltpu.VMEM((KV_RANK + ONES_ROWS, ATT_HEADS * tq), F32),
                        pltpu.VMEM((1, ATT_HEADS * tq), F32),
                        pltpu.VMEM((KEY_CHUNK, ATT_HEADS * tq), F32),
                        pltpu.VMEM((KEY_CHUNK, ATT_HEADS * tq), F32)],
        compiler_params=_params(("arbitrary",)),
        name="dsa",
    )(pt, pt, pt, c, ct, ik, wuk, wuvt)


def _pad_cols(w, n):
    return jnp.pad(w, ((0, 0), (0, n - w.shape[1])))


def kernel(x, meta_tokens, ln_mix_pre, ln_mix_post, ln_ffn_pre, ln_ffn_post, ffn_w1, ffn_w3, ffn_w2,
           ab_w_in, ab_w_out, hgrn_lb_logits, hgrn_out_norm, sconv_w,
           cd_w_in, cd_w_out, rg_conv_w, rg_conv_b, rg_w_a, rg_b_a, rg_w_i, rg_b_i, rg_lambda,
           mla_kv_norm, mla_w_uk, mla_w_uv):
    assert x.shape[0] == 1
    seq, d = x.shape[1], x.shape[2]
    assert seq % KEY_CHUNK == 0
    d_a = hgrn_lb_logits.shape[1]
    d_b = sconv_w.shape[2]
    d_c = rg_lambda.shape[1]
    d_d = ATT_HEADS * ATT_HEAD_DIM
    d_iq = IDX_HEADS * IDX_DIM
    k_sel = min(TOPK_MAX, seq // 4)

    x2 = x[0]
    front = jnp.concatenate([jnp.zeros((PAD_ROWS, d), F32), meta_tokens.astype(F32)], axis=0)
    w1, w3, w2 = ffn_w1.astype(BF16), ffn_w3.astype(BF16), ffn_w2.astype(BF16)

    p0 = _norm_mm(x2, ln_mix_pre[0], ab_w_in[0].astype(BF16), tn=512, front=front)
    og = _hgrn(p0, hgrn_lb_logits, hgrn_out_norm[0], lb_row=0)
    w_out = ab_w_out[0].astype(BF16)
    h = _out_proj_conv(og, p0, sconv_w[0], 4 * d_a, w_out[:d_a], w_out[d_a:], x2, front, ln_mix_post[0])
    h = _ffn(h, ln_ffn_pre[0], w1, w3, w2, ln_ffn_post[0], layer=0)

    w_in = cd_w_in[0]
    o_rx, o_ry, o_q, o_c = 0, d_c, 2 * d_c, 2 * d_c + d_d
    o_iq = o_c + KV_RANK
    o_ik = o_iq + d_iq
    o_iw = o_ik + IDX_DIM
    w_rows = jnp.concatenate([w_in[:, o_rx:o_q], w_in[:, o_c:o_iq], w_in[:, o_ik:o_iw]], axis=1)
    w_rows = _pad_cols(w_rows, -(-w_rows.shape[1] // 512) * 512).astype(BF16)
    w_cols = jnp.concatenate([w_in[:, o_q:o_c], w_in[:, o_iq:o_ik], w_in[:, o_c:o_iq], w_in[:, o_iw:]], axis=1)
    w_cols = _pad_cols(w_cols, -(-w_cols.shape[1] // 512) * 512).astype(BF16).T
    p1 = _norm_mm(h, ln_mix_pre[1], w_rows, tn=w_rows.shape[1] // 4)
    p1t = _norm_mm_t(h, ln_mix_pre[1], w_cols, tc=w_cols.shape[0] // 4)

    hc = _rglru(p1, rg_conv_w[0], rg_conv_b[0], rg_w_a[0].astype(BF16), rg_b_a[0],
                rg_w_i[0].astype(BF16), rg_b_i[0], rg_lambda[0], width=d_c)
    c, ct = _latent(p1, p1t, mla_kv_norm[0], col0=2 * d_c, row0=d_d + d_iq)
    ik = p1[:, 2 * d_c + KV_RANK:2 * d_c + KV_RANK + 2 * IDX_DIM].astype(BF16)
    wuk = jnp.transpose(mla_w_uk[0], (1, 0, 2)).astype(BF16)
    wuvt = jnp.transpose(mla_w_uv[0], (1, 2, 0)).astype(BF16)
    att = _dsa(p1t, c, ct, ik, wuk, wuvt, k_sel,
               q_row0=0, iq_row0=d_d, iw_row0=d_d + d_iq + KV_RANK)
    w_out = cd_w_out[0].astype(BF16)
    h = _out_proj(hc, att, w_out[:d_c], w_out[d_c:], h, ln_mix_post[1])
    return _ffn(h, ln_ffn_pre[1], w1, w3, w2, ln_ffn_post[1], layer=1, skip_front=True)[None]
```

```python
import functools
import math

import jax
import jax.numpy as jnp
from jax import lax
from jax.experimental import pallas as pl
from jax.experimental.pallas import tpu as pltpu

F32 = jnp.float32
BF16 = jnp.bfloat16

EPS = 1e-6
N_META = 16
FRONT = 128
PAD_ROWS = FRONT - N_META
HGRN_HEAD = 128
HGRN_CHUNK = 128
RG_BLOCK = 128
RG_C = 8.0
ATT_HEADS = 8
ATT_HEAD_DIM = 128
KV_RANK = 256
IDX_HEADS = 16
IDX_DIM = 64
TOPK_MAX = 256
Q_BLOCK = 128
KEY_CHUNK = 512
ONES_ROWS = 16
LOG2_E = 1.4426950408889634
BISECT_STEPS = 4
BISECT_BLIND = 3
BISECT_CAP = 64
NEG_BIG = -1e30
VMEM_LIMIT = 56 * 1024 * 1024
PROJ_ROWS = 1664
ROW_TILE = 640
CONV_ROW_TILE = 320


def _pick_tile(n, target, mult):
    best = None
    for t in range(mult, min(n, target) + 1, mult):
        if n % t == 0:
            best = t
    assert best is not None, (n, target, mult)
    return best


def _params(sem):
    return pltpu.CompilerParams(dimension_semantics=sem, vmem_limit_bytes=VMEM_LIMIT)


def _rms(x, g):
    return x * lax.rsqrt(jnp.mean(x * x, axis=-1, keepdims=True) + EPS) * g


def _token_specs(tm, d):
    align = math.gcd(tm, FRONT)
    x_map = lambda *a: (pl.multiple_of(jnp.maximum(a[0] * tm - FRONT, 0), align), 0)
    return [pl.BlockSpec((pl.Element(tm), pl.Element(d)), x_map),
            pl.BlockSpec((FRONT, d), lambda *a: (0, 0))]


def _fill_normed(x_ref, front_ref, g_ref, xn_ref):
    g = g_ref[...]
    first = 0 if front_ref is None else (pl.program_id(0) == 0).astype(jnp.int32)

    def piece(p, carry):
        src = pl.multiple_of(jnp.maximum(p - first, 0) * FRONT, FRONT)
        rows = x_ref[pl.ds(src, FRONT), :]
        if front_ref is not None:
            rows = jnp.where(jnp.logical_and(first == 1, p == 0), front_ref[...], rows)
        xn_ref[pl.ds(pl.multiple_of(p * FRONT, FRONT), FRONT), :] = _rms(rows, g).astype(BF16)
        return carry

    lax.fori_loop(0, xn_ref.shape[0] // FRONT, piece, 0)


def _norm_mm_kernel(x_ref, *rest, tokens):
    front_ref = rest[0] if tokens else None
    g_ref, w_ref, o_ref, xn_ref = rest[-4:]

    @pl.when(pl.program_id(1) == 0)
    def _():
        _fill_normed(x_ref, front_ref, g_ref, xn_ref)

    o_ref[...] = jnp.dot(xn_ref[...], w_ref[...], preferred_element_type=F32)


def _norm_mm(x, g, w, tn=512, front=None):
    d = x.shape[1]
    m = x.shape[0] + (0 if front is None else FRONT)
    n = w.shape[1]
    tm = _pick_tile(m, PROJ_ROWS, FRONT)
    if front is None:
        row_specs, rows = [pl.BlockSpec((tm, d), lambda i, j: (i, 0))], (x,)
    else:
        row_specs, rows = _token_specs(tm, d), (x, front)
    return pl.pallas_call(
        functools.partial(_norm_mm_kernel, tokens=front is not None),
        grid=(m // tm, n // tn),
        in_specs=row_specs + [pl.BlockSpec((1, d), lambda i, j: (0, 0)),
                              pl.BlockSpec((d, tn), lambda i, j: (0, j))],
        out_specs=pl.BlockSpec((tm, tn), lambda i, j: (i, j)),
        out_shape=jax.ShapeDtypeStruct((m, n), F32),
        scratch_shapes=[pltpu.VMEM((tm, d), BF16)],
        compiler_params=_params(("parallel", "arbitrary")),
        name="norm_proj",
    )(*rows, g.reshape(1, d), w)


def _norm_mm_t_kernel(x_ref, g_ref, wt_ref, o_ref, xn_ref):
    @pl.when(pl.program_id(1) == 0)
    def _():
        _fill_normed(x_ref, None, g_ref, xn_ref)

    o_ref[...] = lax.dot_general(wt_ref[...], xn_ref[...], (((1,), (1,)), ((), ())),
                                 preferred_element_type=F32)


def _norm_mm_t(x, g, wt, tc=512):
    m, d = x.shape
    n = wt.shape[0]
    tm = _pick_tile(m, PROJ_ROWS, FRONT)
    return pl.pallas_call(
        _norm_mm_t_kernel,
        grid=(m // tm, n // tc),
        in_specs=[pl.BlockSpec((tm, d), lambda i, j: (i, 0)),
                  pl.BlockSpec((1, d), lambda i, j: (0, 0)),
                  pl.BlockSpec((tc, d), lambda i, j: (j, 0))],
        out_specs=pl.BlockSpec((tc, tm), lambda i, j: (j, i)),
        out_shape=jax.ShapeDtypeStruct((n, m), F32),
        scratch_shapes=[pltpu.VMEM((tm, d), BF16)],
        compiler_params=_params(("parallel", "arbitrary")),
        name="norm_proj_t",
    )(x, g.reshape(1, d), wt)


def _out_proj_kernel(a_ref, b_ref, wa_ref, wb_ref, g_ref, h_ref, o_ref):
    y = jnp.dot(a_ref[...], wa_ref[...], preferred_element_type=F32)
    y = y + jnp.dot(b_ref[...], wb_ref[...], preferred_element_type=F32)
    o_ref[...] = h_ref[...] + _rms(y, g_ref[...])


def _out_proj(a, b, w, h, g):
    m, ka = a.shape
    kb = b.shape[1]
    d = w.shape[1]
    assert ka == kb and w.shape[0] == ka + kb
    tm = _pick_tile(m, ROW_TILE, FRONT)
    return pl.pallas_call(
        _out_proj_kernel,
        grid=(m // tm,),
        in_specs=[pl.BlockSpec((tm, ka), lambda i: (i, 0)),
                  pl.BlockSpec((tm, kb), lambda i: (i, 0)),
                  pl.BlockSpec((ka, d), lambda i: (0, 0)),
                  pl.BlockSpec((kb, d), lambda i: (1, 0)),
                  pl.BlockSpec((1, d), lambda i: (0, 0)),
                  pl.BlockSpec((tm, d), lambda i: (i, 0))],
        out_specs=pl.BlockSpec((tm, d), lambda i: (i, 0)),
        out_shape=jax.ShapeDtypeStruct((m, d), F32),
        compiler_params=_params(("parallel",)),
        name="out_proj",
    )(a, b, w, w, g.reshape(1, d), h)


def _out_proj_conv_kernel(a_ref, sx_ref, sb_ref, sc_ref, sxp_ref, scp_ref, cw_ref, wa_ref, wb_ref, g_ref,
                          x_ref, front_ref, o_ref, ext_ref, *, taps):
    tm = a_ref.shape[0]
    i = pl.program_id(0)
    ext_ref[0:8, :] = jnp.where(i > 0, sxp_ref[...] * scp_ref[...], 0.0)
    ext_ref[8:8 + tm, :] = sx_ref[...] * sc_ref[...]
    conv = jnp.zeros(sx_ref.shape, F32)
    for j in range(taps):
        s = 8 - (taps - 1) + j
        conv = conv + cw_ref[j:j + 1, :] * ext_ref[s:s + tm, :]
    yb = (sb_ref[...] * conv).astype(BF16)

    y = jnp.dot(a_ref[...], wa_ref[...], preferred_element_type=F32)
    y = _rms(y + jnp.dot(yb, wb_ref[...], preferred_element_type=F32), g_ref[...])

    @pl.when(i == 0)
    def _():
        o_ref[0:FRONT, :] = front_ref[...] + y[0:FRONT, :]
        o_ref[FRONT:tm, :] = x_ref[0:tm - FRONT, :] + y[FRONT:tm, :]

    @pl.when(i > 0)
    def _():
        o_ref[...] = x_ref[...] + y


def _out_proj_conv(a, p, cw, col0, w, x, front, g):
    m, ka = a.shape
    taps, width = cw.shape
    d = w.shape[1]
    assert ka == width and w.shape[0] == ka + width
    tm = _pick_tile(m, CONV_ROW_TILE, FRONT // 2)
    cb = col0 // width
    prev = lambda off: (lambda i: (jnp.maximum(i * (tm // 8) - 1, 0), off))
    return pl.pallas_call(
        functools.partial(_out_proj_conv_kernel, taps=taps),
        grid=(m // tm,),
        in_specs=[pl.BlockSpec((tm, ka), lambda i: (i, 0)),
                  pl.BlockSpec((tm, width), lambda i: (i, cb)),
                  pl.BlockSpec((tm, width), lambda i: (i, cb + 1)),
                  pl.BlockSpec((tm, width), lambda i: (i, cb + 2)),
                  pl.BlockSpec((8, width), prev(cb)),
                  pl.BlockSpec((8, width), prev(cb + 2)),
                  pl.BlockSpec((taps, width), lambda i: (0, 0)),
                  pl.BlockSpec((ka, d), lambda i: (0, 0)),
                  pl.BlockSpec((width, d), lambda i: (1, 0)),
                  pl.BlockSpec((1, d), lambda i: (0, 0))] + _token_specs(tm, d),
        out_specs=pl.BlockSpec((tm, d), lambda i: (i, 0)),
        out_shape=jax.ShapeDtypeStruct((m, d), F32),
        scratch_shapes=[pltpu.VMEM((tm + 8, width), F32)],
        compiler_params=_params(("parallel",)),
        name="out_proj_conv",
    )(a, p, p, p, p, p, cw, w, w, g.reshape(1, d), x, front)


def _ffn_kernel(h_ref, gpre_ref, w1_ref, w3_ref, w2_ref, gpost_ref, o_ref, xn_ref, acc_ref):
    j = pl.program_id(1)

    @pl.when(j == 0)
    def _():
        xn_ref[...] = _rms(h_ref[...], gpre_ref[...]).astype(BF16)
        acc_ref[...] = jnp.zeros_like(acc_ref)

    xn = xn_ref[...]
    a = jnp.dot(xn, w1_ref[...], preferred_element_type=F32)
    b = jnp.dot(xn, w3_ref[...], preferred_element_type=F32)
    u = (a * jax.nn.sigmoid(a) * b).astype(BF16)
    acc_ref[...] += jnp.dot(u, w2_ref[...], preferred_element_type=F32)

    @pl.when(j == pl.num_programs(1) - 1)
    def _():
        o_ref[...] = h_ref[...] + _rms(acc_ref[...], gpost_ref[...])


def _ffn(h, gpre, w1, w3, w2, gpost, layer, tf=512, skip_front=False):
    d = h.shape[1]
    f = w1.shape[2]
    if skip_front:
        m = h.shape[0] - FRONT
        tm = _pick_tile(m, ROW_TILE, FRONT)
        h_spec = pl.BlockSpec((pl.Element(tm), pl.Element(d)), lambda i, j: (pl.multiple_of(FRONT + i * tm, FRONT), 0))
    else:
        m = h.shape[0]
        tm = _pick_tile(m, ROW_TILE, FRONT)
        h_spec = pl.BlockSpec((tm, d), lambda i, j: (i, 0))
    return pl.pallas_call(
        _ffn_kernel,
        grid=(m // tm, f // tf),
        in_specs=[h_spec,
                  pl.BlockSpec((1, d), lambda i, j: (0, 0)),
                  pl.BlockSpec((None, d, tf), lambda i, j: (layer, 0, j)),
                  pl.BlockSpec((None, d, tf), lambda i, j: (layer, 0, j)),
                  pl.BlockSpec((None, tf, d), lambda i, j: (layer, j, 0)),
                  pl.BlockSpec((1, d), lambda i, j: (0, 0))],
        out_specs=pl.BlockSpec((tm, d), lambda i, j: (i, 0)),
        out_shape=jax.ShapeDtypeStruct((m, d), F32),
        scratch_shapes=[pltpu.VMEM((tm, d), BF16), pltpu.VMEM((tm, d), F32)],
        compiler_params=_params(("parallel", "arbitrary")),
        name="ffn",
    )(h, gpre.reshape(1, d), w1, w3, w2, gpost.reshape(1, d))


def _cumsum_rows(tri, x):
    hi = x.astype(BF16)
    rest = x - hi.astype(F32)
    mid = rest.astype(BF16)
    lo = (rest - mid.astype(F32)).astype(BF16)
    return (jnp.dot(tri, hi, preferred_element_type=F32) + jnp.dot(tri, mid, preferred_element_type=F32)
            + jnp.dot(tri, lo, preferred_element_type=F32))


def _edge_rows(b, half):
    rows = b.shape[0]
    if half >= 8:
        parts = [jnp.broadcast_to(b[e:e + 1, :], (2 * half, b.shape[1]))
                 for e in range(half - 1, rows, 2 * half)]
        return parts[0] if len(parts) == 1 else jnp.concatenate(parts, axis=0)
    b3 = b.reshape(rows // 8, 8, b.shape[1])
    sub = lax.broadcasted_iota(jnp.int32, b3.shape, 1)
    pick = lambda r: jnp.broadcast_to(b3[:, r:r + 1, :], b3.shape)
    edge = pick(half - 1)
    for start in range(2 * half, 8, 2 * half):
        edge = jnp.where(sub >= start, pick(start + half - 1), edge)
    return edge.reshape(b.shape)


def _hgrn_kernel(q_ref, f_ref, v_ref, gate_ref, lbl_ref, gn_ref, o_ref, st_ref, *, n_heads, lb_row):
    c_rows = HGRN_CHUNK

    @pl.when(pl.program_id(1) == 0)
    def _():
        st_ref[...] = jnp.zeros_like(st_ref)

    logits = lbl_ref[...]
    ex = jnp.exp(logits - jnp.max(logits, axis=0, keepdims=True))
    lb_all = jnp.sum(ex[0:lb_row + 1, :], axis=0, keepdims=True) / jnp.sum(ex, axis=0, keepdims=True)

    r_i = lax.broadcasted_iota(jnp.int32, (c_rows, c_rows), 0)
    c_i = lax.broadcasted_iota(jnp.int32, (c_rows, c_rows), 1)
    tri = (r_i >= c_i).astype(BF16)
    levels = []
    half = c_rows // 2
    while half >= 1:
        shift = half.bit_length()
        upper = (lax.shift_right_logical(r_i, shift - 1) & 1) == 1
        same = lax.shift_right_logical(r_i, shift) == lax.shift_right_logical(c_i, shift)
        levels.append((half, upper, jnp.where(upper, 1.0, -1.0), same))
        half //= 2

    for hh in range(n_heads):
        cols = slice(hh * HGRN_HEAD, (hh + 1) * HGRN_HEAD)
        lb = lb_all[:, cols]
        q = q_ref[:, cols]
        v = v_ref[:, cols]
        f = lb + (1.0 - lb) * jax.nn.sigmoid(f_ref[:, cols])
        k = 1.0 - f
        b = _cumsum_rows(tri, jnp.log2(f))
        b_last = b[c_rows - 1:c_rows, :]

        st = st_ref[hh]
        inter = lax.dot_general((q * jnp.exp2(b)).astype(BF16), st.astype(BF16),
                                (((1,), (1,)), ((), ())), preferred_element_type=F32)
        kt = (k * jnp.exp2(b_last - b)).astype(BF16)
        st_ref[hh] = st * jnp.exp2(b_last) + lax.dot_general(
            v.astype(BF16), kt, (((0,), (0,)), ((), ())), preferred_element_type=F32)

        att = jnp.zeros((c_rows, c_rows), F32)
        for half, upper, sign, same in levels:
            decay = jnp.exp2((b - _edge_rows(b, half)) * sign)
            scaled = jnp.where(upper, q, k) * decay
            qt = jnp.where(upper, scaled, 0.0).astype(BF16)
            kl = jnp.where(upper, 0.0, scaled).astype(BF16)
            pair = lax.dot_general(qt, kl, (((1,), (1,)), ((), ())), preferred_element_type=F32)
            att = att + jnp.where(same, pair, 0.0)
        o = inter + jnp.dot(att.astype(BF16), v.astype(BF16), preferred_element_type=F32)
        o = o + jnp.sum(q * k, axis=-1, keepdims=True) * v

        gate = gate_ref[:, cols]
        o_ref[:, cols] = (_rms(o, gn_ref[...]) * (gate * jax.nn.sigmoid(gate))).astype(BF16)


def _hgrn(p, lb_logits, gn, lb_row):
    m = p.shape[0]
    d_a = lb_logits.shape[1]
    hp = 8
    width = hp * HGRN_HEAD
    groups = d_a // width
    tb = HGRN_CHUNK
    n_l = lb_logits.shape[0]
    col = lambda off: (lambda g, t: (t, off + g))
    kern = functools.partial(_hgrn_kernel, n_heads=hp, lb_row=lb_row)
    return pl.pallas_call(
        kern,
        grid=(groups, m // tb),
        in_specs=[pl.BlockSpec((tb, width), col(0)),
                  pl.BlockSpec((tb, width), col(groups)),
                  pl.BlockSpec((tb, width), col(2 * groups)),
                  pl.BlockSpec((tb, width), col(3 * groups)),
                  pl.BlockSpec((n_l, width), lambda g, t: (0, g)),
                  pl.BlockSpec((1, HGRN_HEAD), lambda g, t: (0, 0))],
        out_specs=pl.BlockSpec((tb, width), lambda g, t: (t, g)),
        out_shape=jax.ShapeDtypeStruct((m, d_a), BF16),
        scratch_shapes=[pltpu.VMEM((hp, HGRN_HEAD, HGRN_HEAD), F32)],
        compiler_params=_params(("parallel", "arbitrary")),
        name="hgrn2",
    )(p, p, p, p, lb_logits, gn.reshape(1, HGRN_HEAD))


def _rglru_kernel(rx_ref, ry_ref, rxp_ref, cw_ref, cb_ref, wa_ref, ba_ref, wi_ref, bi_ref, lam_ref,
                  o_ref, ext_ref, a_ref, x_ref, hs_ref, h_ref, *, taps):
    tm, width = rx_ref.shape
    i = pl.program_id(0)

    @pl.when(i == 0)
    def _():
        h_ref[...] = jnp.zeros_like(h_ref)

    ext_ref[0:8, :] = jnp.where(i > 0, rxp_ref[...], 0.0)
    ext_ref[8:8 + tm, :] = rx_ref[...]
    u = jnp.zeros((tm, width), F32) + cb_ref[...]
    for j in range(taps):
        s = 8 - (taps - 1) + j
        u = u + cw_ref[j:j + 1, :] * ext_ref[s:s + tm, :]

    u_b = u.astype(BF16)
    r_parts, i_parts = [], []
    for n in range(width // RG_BLOCK):
        blk = slice(n * RG_BLOCK, (n + 1) * RG_BLOCK)
        r_parts.append(jnp.dot(u_b[:, blk], wa_ref[n], preferred_element_type=F32))
        i_parts.append(jnp.dot(u_b[:, blk], wi_ref[n], preferred_element_type=F32))
    r = jax.nn.sigmoid(jnp.concatenate(r_parts, axis=1) + ba_ref[...])
    ig = jax.nn.sigmoid(jnp.concatenate(i_parts, axis=1) + bi_ref[...])

    neg_lam = -lam_ref[...]
    softplus = jnp.maximum(neg_lam, 0.0) + jnp.log1p(jnp.exp(-jnp.abs(neg_lam)))
    log_a = -RG_C * r * softplus
    row = i * tm + lax.broadcasted_iota(jnp.int32, (tm, 1), 0)
    a = jnp.exp(log_a)
    xin = jnp.sqrt(1.0 - a * a) * (ig * u)
    a_ref[...] = a
    x_ref[...] = jnp.where(row >= PAD_ROWS, xin, 0.0)

    sub = lax.broadcasted_iota(jnp.int32, (8, width), 0)

    def group(gidx, h):
        base = pl.multiple_of(gidx * 8, 8)
        a8 = a_ref[pl.ds(base, 8), :]
        x8 = x_ref[pl.ds(base, 8), :]
        for shift in (1, 2, 4):
            keep = sub >= shift
            a_up = jnp.where(keep, pltpu.roll(a8, shift, axis=0), 1.0)
            x_up = jnp.where(keep, pltpu.roll(x8, shift, axis=0), 0.0)
            x8 = a8 * x_up + x8
            a8 = a8 * a_up
        h8 = a8 * h + x8
        hs_ref[pl.ds(base, 8), :] = h8
        return h8[7:8, :]

    h_ref[...] = lax.fori_loop(0, tm // 8, group, h_ref[...])
    o_ref[...] = (hs_ref[...] * jax.nn.gelu(ry_ref[...])).astype(BF16)


def _rglru(p, cw, cb, wa, ba, wi, bi, lam, width):
    m = p.shape[0]
    taps = cw.shape[0]
    tm = _pick_tile(m, ROW_TILE, FRONT)
    nb = width // RG_BLOCK
    row = lambda v: v.reshape(1, width)
    full2 = lambda shape: pl.BlockSpec(shape, lambda i: (0, 0))
    full3 = lambda shape: pl.BlockSpec(shape, lambda i: (0, 0, 0))
    return pl.pallas_call(
        functools.partial(_rglru_kernel, taps=taps),
        grid=(m // tm,),
        in_specs=[pl.BlockSpec((tm, width), lambda i: (i, 0)),
                  pl.BlockSpec((tm, width), lambda i: (i, 1)),
                  pl.BlockSpec((8, width), lambda i: (jnp.maximum(i * (tm // 8) - 1, 0), 0)),
                  full2((taps, width)), full2((1, width)),
                  full3((nb, RG_BLOCK, RG_BLOCK)), full2((1, width)),
                  full3((nb, RG_BLOCK, RG_BLOCK)), full2((1, width)),
                  full2((1, width))],
        out_specs=pl.BlockSpec((tm, width), lambda i: (i, 0)),
        out_shape=jax.ShapeDtypeStruct((m, width), BF16),
        scratch_shapes=[pltpu.VMEM((tm + 8, width), F32),
                        pltpu.VMEM((tm, width), F32),
                        pltpu.VMEM((tm, width), F32),
                        pltpu.VMEM((tm, width), F32),
                        pltpu.VMEM((1, width), F32)],
        compiler_params=_params(("arbitrary",)),
        name="rglru",
    )(p, p, p, cw, row(cb), wa, row(ba), wi, row(bi), row(lam))


def _latent_kernel(c_ref, ct_ref, g_ref, gt_ref, o_ref, ot_ref):
    o_ref[...] = _rms(c_ref[...], g_ref[...]).astype(BF16)
    ct = ct_ref[...]
    ot_ref[0:KV_RANK, :] = (ct * lax.rsqrt(jnp.mean(ct * ct, axis=0, keepdims=True) + EPS)
                            * gt_ref[...]).astype(BF16)
    first = lax.broadcasted_iota(jnp.int32, (ONES_ROWS, ct.shape[1]), 0) == 0
    ot_ref[KV_RANK:KV_RANK + ONES_ROWS, :] = jnp.where(first, 1.0, 0.0).astype(BF16)


def _latent(p, pt, g, col0, row0):
    m = p.shape[0]
    r = g.shape[0]
    tm = _pick_tile(m, ROW_TILE, FRONT)
    return pl.pallas_call(
        _latent_kernel,
        grid=(m // tm,),
        in_specs=[pl.BlockSpec((tm, r), lambda i: (i, col0 // r)),
                  pl.BlockSpec((r, tm), lambda i: (row0 // r, i)),
                  pl.BlockSpec((1, r), lambda i: (0, 0)),
                  pl.BlockSpec((r, 1), lambda i: (0, 0))],
        out_specs=[pl.BlockSpec((tm, r), lambda i: (i, 0)),
                   pl.BlockSpec((r + ONES_ROWS, tm), lambda i: (0, i))],
        out_shape=[jax.ShapeDtypeStruct((m, r), BF16), jax.ShapeDtypeStruct((r + ONES_ROWS, m), BF16)],
        compiler_params=_params(("parallel",)),
        name="latent_norm",
    )(p, pt, g.reshape(1, r), g.reshape(r, 1))


def _dsa_kernel(qt_ref, iqt_ref, iwt_ref, c_ref, ct_ref, ik_ref, wuk_ref, wuvt_ref, o_ref,
                sc_ref, iqp_ref, qlt_ref, acc_ref, m_ref, sa_ref, sb_ref, *, k_sel):
    i = pl.program_id(0)
    tq = Q_BLOCK
    n_heads = ATT_HEADS
    sub_blocks = KEY_CHUNK // tq
    n_chunks = (i * tq + KEY_CHUNK - 1) // KEY_CHUNK

    scale = ATT_HEAD_DIM ** -0.5 * LOG2_E
    for h in range(n_heads):
        qh = qt_ref[h * ATT_HEAD_DIM:(h + 1) * ATT_HEAD_DIM, :].astype(BF16)
        ql = jnp.dot(wuk_ref[h], qh, preferred_element_type=F32) * scale
        qlt_ref[:, h * tq:(h + 1) * tq] = ql.astype(BF16)

    k_loc = lax.broadcasted_iota(jnp.int32, (tq, tq), 0)
    q_loc = lax.broadcasted_iota(jnp.int32, (tq, tq), 1)
    head_cols = [slice(h * tq, (h + 1) * tq) for h in range(n_heads)]

    def masked_scores(rows, bias, s_ref):
        bias2 = jnp.concatenate([bias, bias], axis=1)
        cmax = []
        for hp in range(n_heads // 2):
            cols = slice(2 * hp * tq, (2 * hp + 2) * tq)
            sm = jnp.dot(c_ref[rows, :], qlt_ref[:, cols], preferred_element_type=F32) + bias2
            s_ref[:, cols] = sm
            cmax.append(jnp.max(sm, axis=0, keepdims=True))
        return jnp.concatenate(cmax, axis=1)

    def accumulate(rows, s_ref, cmax):
        m_old = m_ref[...]
        m_new = jnp.maximum(m_old, cmax)
        m_ref[...] = m_new
        p = jnp.exp2(s_ref[...] - m_new).astype(BF16)
        acc_ref[...] = jnp.exp2(m_old - m_new) * acc_ref[...] + jnp.dot(
            ct_ref[:, rows], p, preferred_element_type=F32)

    m_ref[...] = jnp.full(m_ref.shape, NEG_BIG, F32)
    acc_ref[...] = jnp.zeros_like(acc_ref)

    q_row = i * tq + q_loc
    allowed0 = jnp.logical_and(jnp.logical_or(k_loc >= PAD_ROWS, k_loc == q_row), k_loc <= q_row)
    s0_ref = sa_ref.at[0:FRONT, :]
    cmax0 = masked_scores(slice(0, FRONT), jnp.where(allowed0, 0.0, NEG_BIG), s0_ref)
    accumulate(slice(0, FRONT), s0_ref, cmax0)

    iw = iwt_ref[...] * ((IDX_DIM ** -0.5) * (IDX_HEADS ** -0.5))

    for hp in range(IDX_HEADS // 2):
        pair = jnp.concatenate(
            [iqt_ref[(2 * hp) * IDX_DIM:(2 * hp + 1) * IDX_DIM, :],
             iqt_ref[(2 * hp + 1) * IDX_DIM:(2 * hp + 2) * IDX_DIM, :]], axis=1).astype(BF16)
        iqp_ref[hp, 0:IDX_DIM, :] = pair
        iqp_ref[hp, IDX_DIM:2 * IDX_DIM, :] = jnp.zeros_like(pair)

    def score_chunk(j, carry):
        mn, mx = carry
        for u in range(sub_blocks):
            kb = j * sub_blocks + 1 + u
            r0 = pl.multiple_of(kb * tq, tq)
            ikb = ik_ref[pl.ds(r0, tq), :]
            s = jnp.zeros((tq, tq), F32)
            for hp in range(IDX_HEADS // 2):
                x = jnp.dot(ikb, iqp_ref[hp], preferred_element_type=F32)
                s = s + jnp.maximum(x[:, 0:tq], 0.0) * iw[2 * hp:2 * hp + 1, :]
                s = s + jnp.maximum(x[:, tq:2 * tq], 0.0) * iw[2 * hp + 1:2 * hp + 2, :]
            visible = jnp.logical_or(kb < i, jnp.logical_and(kb == i, k_loc <= q_loc))
            sc_ref[pl.ds(r0, tq), :] = jnp.where(visible, s, -jnp.inf)
            mx = jnp.maximum(mx, jnp.max(jnp.where(visible, s, -jnp.inf).reshape(tq // 8, 8, tq), axis=0))
            mn = jnp.minimum(mn, jnp.min(jnp.where(visible, s, jnp.inf).reshape(tq // 8, 8, tq), axis=0))
        return mn, mx

    mn, mx = lax.fori_loop(0, n_chunks, score_chunk,
                           (jnp.full((8, tq), jnp.inf, F32), jnp.full((8, tq), -jnp.inf, F32)))
    row_min = jnp.min(mn, axis=0, keepdims=True)
    row_max = jnp.max(mx, axis=0, keepdims=True)

    groups = KEY_CHUNK // 64

    def chunk_scores(j):
        r0 = pl.multiple_of(FRONT + j * KEY_CHUNK, FRONT)
        return sc_ref[pl.ds(r0, KEY_CHUNK), :].reshape(groups, 8, 8, tq)

    def count_ge(t):
        def body(j, cnt):
            kk = chunk_scores(j)
            for g in range(groups):
                cnt = jnp.where(kk[g] >= t, cnt + 1, cnt)
            return cnt
        cnt = lax.fori_loop(0, n_chunks, body, jnp.zeros((8, 8, tq), jnp.int32))
        return jnp.sum(jnp.sum(cnt, axis=0), axis=0, keepdims=True)

    lane = lax.broadcasted_iota(jnp.int32, (1, tq), 1)
    n_visible = (i - 1) * tq + lane + 1
    k_row = jnp.minimum(k_sel, n_visible)

    def probe(mid, movable, lo, hi, cnt_lo):
        c = count_ge(mid)
        up = jnp.logical_and(c >= k_row, movable)
        down = jnp.logical_and(jnp.logical_not(up), movable)
        return jnp.where(up, mid, lo), jnp.where(down, mid, hi), jnp.where(up, c, cnt_lo)

    def midpoint(lo, hi, cnt_lo):
        mid = lo + 0.5 * (hi - lo)
        movable = jnp.logical_and(cnt_lo != k_row, jnp.logical_and(mid > lo, mid < hi))
        return mid, movable

    def lane_flags(lo, hi, cnt_lo):
        moving = midpoint(lo, hi, cnt_lo)[1].astype(jnp.int32)
        return jnp.sum(moving + 256 * (cnt_lo > k_row).astype(jnp.int32))

    lo, hi, cnt_lo = probe(row_max, n_visible > k_row, row_min, row_max, n_visible)

    def bisect_cond(state):
        return jnp.logical_and(state[0] % 256 > 0, state[1] < BISECT_CAP)

    def bisect_steps(_, bracket):
        lo, hi, cnt_lo = bracket
        for _ in range(BISECT_STEPS):
            mid, movable = midpoint(lo, hi, cnt_lo)
            lo, hi, cnt_lo = probe(mid, movable, lo, hi, cnt_lo)
        return lo, hi, cnt_lo

    def bisect_body(state):
        lo, hi, cnt_lo = bisect_steps(0, state[2:])
        return lane_flags(lo, hi, cnt_lo), state[1] + 1, lo, hi, cnt_lo

    lo, hi, cnt_lo = lax.fori_loop(0, BISECT_BLIND, bisect_steps, (lo, hi, cnt_lo))
    state = (lane_flags(lo, hi, cnt_lo), jnp.int32(0), lo, hi, cnt_lo)
    flags, _, thr, _, cnt_thr = lax.while_loop(bisect_cond, bisect_body, state)

    tied = cnt_thr > k_row

    @pl.when(flags >= 256)
    def _():
        key_idx = lax.broadcasted_iota(jnp.int32, (groups, 8, 8, tq), 0) * 64 + (
            lax.broadcasted_iota(jnp.int32, (groups, 8, 8, tq), 1) * 8
            + lax.broadcasted_iota(jnp.int32, (groups, 8, 8, tq), 2))

        def count_where(pred):
            def body(j, cnt):
                hit = pred(chunk_scores(j), j * KEY_CHUNK + key_idx)
                return cnt + jnp.sum(hit.astype(jnp.int32), axis=0)
            cnt = lax.fori_loop(0, n_chunks, body, jnp.zeros((8, 8, tq), jnp.int32))
            return jnp.sum(jnp.sum(cnt, axis=0), axis=0, keepdims=True)

        wanted = k_row - count_where(lambda kk, idx: kk > thr)

        def index_step(_, bounds):
            below, cap = bounds
            mid = below + lax.shift_right_logical(cap - below, 1)
            enough = count_where(lambda kk, idx: jnp.logical_and(kk == thr, idx <= mid)) >= wanted
            return jnp.where(enough, below, mid), jnp.where(enough, mid, cap)

        n_keys = n_chunks * KEY_CHUNK
        steps = max(1, (sc_ref.shape[0] - 1).bit_length())
        _, cap = lax.fori_loop(0, steps, index_step,
                               (jnp.full((1, tq), -1, jnp.int32), jnp.full((1, tq), 1, jnp.int32) * (n_keys - 1)))

        def drop_chunk(j, carry):
            r0 = pl.multiple_of(FRONT + j * KEY_CHUNK, FRONT)
            kk = chunk_scores(j)
            extra = jnp.logical_and(jnp.logical_and(kk == thr, j * KEY_CHUNK + key_idx > cap), tied)
            sc_ref[pl.ds(r0, KEY_CHUNK), :] = jnp.where(extra, -jnp.inf, kk).reshape(KEY_CHUNK, tq)
            return carry

        lax.fori_loop(0, n_chunks, drop_chunk, 0)

    def chunk_rows(j):
        return pl.ds(pl.multiple_of(FRONT + j * KEY_CHUNK, FRONT), KEY_CHUNK)

    def chunk_scores_masked(j, s_ref):
        rows = chunk_rows(j)
        return masked_scores(rows, jnp.where(sc_ref[rows, :] >= thr, 0.0, NEG_BIG), s_ref)

    @pl.when(i > 0)
    def _():
        def chunk_pair(t, cmax_a):
            j = 2 * t
            cmax_b = chunk_scores_masked(j + 1, sb_ref)
            accumulate(chunk_rows(j), sa_ref, cmax_a)
            cmax_a = chunk_scores_masked(j + 2, sa_ref)
            accumulate(chunk_rows(j + 1), sb_ref, cmax_b)
            return cmax_a

        n_pairs = (n_chunks - 1) // 2
        cmax_a = lax.fori_loop(0, n_pairs, chunk_pair, chunk_scores_masked(0, sa_ref))
        j = 2 * n_pairs

        @pl.when(n_chunks - j == 1)
        def _():
            accumulate(chunk_rows(j), sa_ref, cmax_a)

        @pl.when(n_chunks - j == 2)
        def _():
            cmax_b = chunk_scores_masked(j + 1, sb_ref)
            accumulate(chunk_rows(j), sa_ref, cmax_a)
            accumulate(chunk_rows(j + 1), sb_ref, cmax_b)

    o_lat = (acc_ref[0:KV_RANK, :] / acc_ref[KV_RANK:KV_RANK + 1, :]).astype(BF16)
    for h in range(n_heads):
        oh = jnp.dot(wuvt_ref[h], o_lat[:, head_cols[h]], preferred_element_type=F32)
        o_ref[:, h * ATT_HEAD_DIM:(h + 1) * ATT_HEAD_DIM] = oh.T.astype(BF16)


def _dsa(pt, c, ct, ik, wuk, wuvt, k_sel, q_row0, iq_row0, iw_row0):
    m = c.shape[0]
    tq = Q_BLOCK
    d_q = ATT_HEADS * ATT_HEAD_DIM
    d_iq = IDX_HEADS * IDX_DIM
    full2 = lambda shape: pl.BlockSpec(shape, lambda i: (0, 0))
    full3 = lambda shape: pl.BlockSpec(shape, lambda i: (0, 0, 0))
    return pl.pallas_call(
        functools.partial(_dsa_kernel, k_sel=k_sel),
        grid=(m // tq,),
        in_specs=[pl.BlockSpec((d_q, tq), lambda i: (q_row0 // d_q, i)),
                  pl.BlockSpec((d_iq, tq), lambda i: (iq_row0 // d_iq, i)),
                  pl.BlockSpec((IDX_HEADS, tq), lambda i: (iw_row0 // IDX_HEADS, i)),
                  full2(c.shape), full2(ct.shape), full2(ik.shape),
                  full3(wuk.shape), full3(wuvt.shape)],
        out_specs=pl.BlockSpec((tq, d_q), lambda i: (i, 0)),
        out_shape=jax.ShapeDtypeStruct((m, d_q), BF16),
        scratch_shapes=[pltpu.VMEM((m, tq), F32),
                        pltpu.VMEM((IDX_HEADS // 2, 2 * IDX_DIM, 2 * tq), BF16),
                        pltpu.VMEM((KV_RANK, ATT_HEADS * tq), BF16),
                        pltpu.VMEM((KV_RANK + ONES_ROWS, ATT_HEADS * tq), F32),
                        pltpu.VMEM((1, ATT_HEADS * tq), F32),
                        pltpu.VMEM((KEY_CHUNK, ATT_HEADS * tq), F32),
                        pltpu.VMEM((KEY_CHUNK, ATT_HEADS * tq), F32)],
        compiler_params=_params(("arbitrary",)),
        name="dsa",
    )(pt, pt, pt, c, ct, ik, wuk, wuvt)


def _pad_cols(w, n):
    return jnp.pad(w, ((0, 0), (0, n - w.shape[1])))


def kernel(x, meta_tokens, ln_mix_pre, ln_mix_post, ln_ffn_pre, ln_ffn_post, ffn_w1, ffn_w3, ffn_w2,
           ab_w_in, ab_w_out, hgrn_lb_logits, hgrn_out_norm, sconv_w,
           cd_w_in, cd_w_out, rg_conv_w, rg_conv_b, rg_w_a, rg_b_a, rg_w_i, rg_b_i, rg_lambda,
           mla_kv_norm, mla_w_uk, mla_w_uv):
    assert x.shape[0] == 1
    seq, d = x.shape[1], x.shape[2]
    assert seq % KEY_CHUNK == 0
    d_a = hgrn_lb_logits.shape[1]
    d_c = rg_lambda.shape[1]
    d_d = ATT_HEADS * ATT_HEAD_DIM
    d_iq = IDX_HEADS * IDX_DIM
    k_sel = min(TOPK_MAX, seq // 4)

    x2 = x[0]
    front = jnp.concatenate([jnp.zeros((PAD_ROWS, d), F32), meta_tokens.astype(F32)], axis=0)
    w1, w3, w2 = ffn_w1.astype(BF16), ffn_w3.astype(BF16), ffn_w2.astype(BF16)

    p0 = _norm_mm(x2, ln_mix_pre[0], ab_w_in[0].astype(BF16), tn=512, front=front)
    og = _hgrn(p0, hgrn_lb_logits, hgrn_out_norm[0], lb_row=0)
    h = _out_proj_conv(og, p0, sconv_w[0], 4 * d_a, ab_w_out[0].astype(BF16), x2, front, ln_mix_post[0])
    h = _ffn(h, ln_ffn_pre[0], w1, w3, w2, ln_ffn_post[0], layer=0)

    w_in = cd_w_in[0]
    o_rx, o_q, o_c = 0, 2 * d_c, 2 * d_c + d_d
    o_iq = o_c + KV_RANK
    o_ik = o_iq + d_iq
    o_iw = o_ik + IDX_DIM
    w_rows = jnp.concatenate([w_in[:, o_rx:o_q], w_in[:, o_c:o_iq], w_in[:, o_ik:o_iw]], axis=1)
    w_rows = _pad_cols(w_rows, -(-w_rows.shape[1] // 512) * 512).astype(BF16)
    w_cols = jnp.concatenate([w_in[:, o_q:o_c], w_in[:, o_iq:o_ik], w_in[:, o_c:o_iq], w_in[:, o_iw:]], axis=1)
    w_cols = _pad_cols(w_cols, -(-w_cols.shape[1] // 512) * 512).astype(BF16).T
    p1 = _norm_mm(h, ln_mix_pre[1], w_rows, tn=w_rows.shape[1] // 4)
    p1t = _norm_mm_t(h, ln_mix_pre[1], w_cols, tc=w_cols.shape[0] // 4)

    hc = _rglru(p1, rg_conv_w[0], rg_conv_b[0], rg_w_a[0].astype(BF16), rg_b_a[0],
                rg_w_i[0].astype(BF16), rg_b_i[0], rg_lambda[0], width=d_c)
    c, ct = _latent(p1, p1t, mla_kv_norm[0], col0=2 * d_c, row0=d_d + d_iq)
    ik = p1[:, 2 * d_c + KV_RANK:2 * d_c + KV_RANK + 2 * IDX_DIM].astype(BF16)
    wuk = jnp.transpose(mla_w_uk[0], (1, 0, 2)).astype(BF16)
    wuvt = jnp.transpose(mla_w_uv[0], (1, 2, 0)).astype(BF16)
    att = _dsa(p1t, c, ct, ik, wuk, wuvt, k_sel,
               q_row0=0, iq_row0=d_d, iw_row0=d_d + d_iq + KV_RANK)
    h = _out_proj(hc, att, cd_w_out[0].astype(BF16), h, ln_mix_post[1])
    return _ffn(h, ln_ffn_pre[1], w1, w3, w2, ln_ffn_post[1], layer=1, skip_front=True)[None]
```

```python
import functools
import math

import jax
import jax.numpy as jnp
from jax import lax
from jax.experimental import pallas as pl
from jax.experimental.pallas import tpu as pltpu

F32 = jnp.float32
BF16 = jnp.bfloat16

EPS = 1e-6
N_META = 16
FRONT = 128
PAD_ROWS = FRONT - N_META
HGRN_HEAD = 128
HGRN_CHUNK = 128
RG_BLOCK = 128
RG_C = 8.0
ATT_HEADS = 8
ATT_HEAD_DIM = 128
KV_RANK = 256
IDX_HEADS = 16
IDX_DIM = 64
TOPK_MAX = 256
Q_BLOCK = 128
KEY_CHUNK = 512
ONES_ROWS = 16
LOG2_E = 1.4426950408889634
BISECT_STEPS = 4
BISECT_BLIND = 4
BISECT_CAP = 64
NEG_BIG = -1e30
VMEM_LIMIT = 56 * 1024 * 1024
PROJ_ROWS = 1664
ROW_TILE = 640
CONV_ROW_TILE = 320


def _pick_tile(n, target, mult):
    best = None
    for t in range(mult, min(n, target) + 1, mult):
        if n % t == 0:
            best = t
    assert best is not None, (n, target, mult)
    return best


def _params(sem):
    return pltpu.CompilerParams(dimension_semantics=sem, vmem_limit_bytes=VMEM_LIMIT)


def _rms(x, g):
    return x * lax.rsqrt(jnp.mean(x * x, axis=-1, keepdims=True) + EPS) * g


def _token_specs(tm, d):
    align = math.gcd(tm, FRONT)
    x_map = lambda *a: (pl.multiple_of(jnp.maximum(a[0] * tm - FRONT, 0), align), 0)
    return [pl.BlockSpec((pl.Element(tm), pl.Element(d)), x_map),
            pl.BlockSpec((FRONT, d), lambda *a: (0, 0))]


def _fill_normed(x_ref, front_ref, g_ref, xn_ref):
    g = g_ref[...]
    first = 0 if front_ref is None else (pl.program_id(0) == 0).astype(jnp.int32)

    def piece(p, carry):
        src = pl.multiple_of(jnp.maximum(p - first, 0) * FRONT, FRONT)
        rows = x_ref[pl.ds(src, FRONT), :]
        if front_ref is not None:
            rows = jnp.where(jnp.logical_and(first == 1, p == 0), front_ref[...], rows)
        xn_ref[pl.ds(pl.multiple_of(p * FRONT, FRONT), FRONT), :] = _rms(rows, g).astype(BF16)
        return carry

    lax.fori_loop(0, xn_ref.shape[0] // FRONT, piece, 0)


def _norm_mm_kernel(x_ref, *rest, tokens):
    front_ref = rest[0] if tokens else None
    g_ref, w_ref, o_ref, xn_ref = rest[-4:]

    @pl.when(pl.program_id(1) == 0)
    def _():
        _fill_normed(x_ref, front_ref, g_ref, xn_ref)

    o_ref[...] = jnp.dot(xn_ref[...], w_ref[...], preferred_element_type=F32)


def _norm_mm(x, g, w, tn=512, front=None):
    d = x.shape[1]
    m = x.shape[0] + (0 if front is None else FRONT)
    n = w.shape[1]
    tm = _pick_tile(m, PROJ_ROWS, FRONT)
    if front is None:
        row_specs, rows = [pl.BlockSpec((tm, d), lambda i, j: (i, 0))], (x,)
    else:
        row_specs, rows = _token_specs(tm, d), (x, front)
    return pl.pallas_call(
        functools.partial(_norm_mm_kernel, tokens=front is not None),
        grid=(m // tm, n // tn),
        in_specs=row_specs + [pl.BlockSpec((1, d), lambda i, j: (0, 0)),
                              pl.BlockSpec((d, tn), lambda i, j: (0, j))],
        out_specs=pl.BlockSpec((tm, tn), lambda i, j: (i, j)),
        out_shape=jax.ShapeDtypeStruct((m, n), F32),
        scratch_shapes=[pltpu.VMEM((tm, d), BF16)],
        compiler_params=_params(("parallel", "arbitrary")),
        name="norm_proj",
    )(*rows, g.reshape(1, d), w)


def _norm_mm_t_kernel(x_ref, g_ref, wt_ref, o_ref, xn_ref):
    @pl.when(pl.program_id(1) == 0)
    def _():
        _fill_normed(x_ref, None, g_ref, xn_ref)

    o_ref[...] = lax.dot_general(wt_ref[...], xn_ref[...], (((1,), (1,)), ((), ())),
                                 preferred_element_type=F32)


def _norm_mm_t(x, g, wt, tc=512):
    m, d = x.shape
    n = wt.shape[0]
    tm = _pick_tile(m, PROJ_ROWS, FRONT)
    return pl.pallas_call(
        _norm_mm_t_kernel,
        grid=(m // tm, n // tc),
        in_specs=[pl.BlockSpec((tm, d), lambda i, j: (i, 0)),
                  pl.BlockSpec((1, d), lambda i, j: (0, 0)),
                  pl.BlockSpec((tc, d), lambda i, j: (j, 0))],
        out_specs=pl.BlockSpec((tc, tm), lambda i, j: (j, i)),
        out_shape=jax.ShapeDtypeStruct((n, m), F32),
        scratch_shapes=[pltpu.VMEM((tm, d), BF16)],
        compiler_params=_params(("parallel", "arbitrary")),
        name="norm_proj_t",
    )(x, g.reshape(1, d), wt)


def _out_proj_kernel(a_ref, b_ref, wa_ref, wb_ref, g_ref, h_ref, o_ref):
    y = jnp.dot(a_ref[...], wa_ref[...], preferred_element_type=F32)
    y = y + jnp.dot(b_ref[...], wb_ref[...], preferred_element_type=F32)
    o_ref[...] = h_ref[...] + _rms(y, g_ref[...])


def _out_proj(a, b, w, h, g):
    m, ka = a.shape
    kb = b.shape[1]
    d = w.shape[1]
    assert ka == kb and w.shape[0] == ka + kb
    tm = _pick_tile(m, ROW_TILE, FRONT)
    return pl.pallas_call(
        _out_proj_kernel,
        grid=(m // tm,),
        in_specs=[pl.BlockSpec((tm, ka), lambda i: (i, 0)),
                  pl.BlockSpec((tm, kb), lambda i: (i, 0)),
                  pl.BlockSpec((ka, d), lambda i: (0, 0)),
                  pl.BlockSpec((kb, d), lambda i: (1, 0)),
                  pl.BlockSpec((1, d), lambda i: (0, 0)),
                  pl.BlockSpec((tm, d), lambda i: (i, 0))],
        out_specs=pl.BlockSpec((tm, d), lambda i: (i, 0)),
        out_shape=jax.ShapeDtypeStruct((m, d), F32),
        compiler_params=_params(("parallel",)),
        name="out_proj",
    )(a, b, w, w, g.reshape(1, d), h)


def _out_proj_conv_kernel(a_ref, sx_ref, sb_ref, sc_ref, sxp_ref, scp_ref, cw_ref, wa_ref, wb_ref, g_ref,
                          x_ref, front_ref, o_ref, ext_ref, *, taps):
    tm = a_ref.shape[0]
    i = pl.program_id(0)
    ext_ref[0:8, :] = jnp.where(i > 0, sxp_ref[...] * scp_ref[...], 0.0)
    ext_ref[8:8 + tm, :] = sx_ref[...] * sc_ref[...]
    conv = jnp.zeros(sx_ref.shape, F32)
    for j in range(taps):
        s = 8 - (taps - 1) + j
        conv = conv + cw_ref[j:j + 1, :] * ext_ref[s:s + tm, :]
    yb = (sb_ref[...] * conv).astype(BF16)

    y = jnp.dot(a_ref[...], wa_ref[...], preferred_element_type=F32)
    y = _rms(y + jnp.dot(yb, wb_ref[...], preferred_element_type=F32), g_ref[...])

    @pl.when(i == 0)
    def _():
        o_ref[0:FRONT, :] = front_ref[...] + y[0:FRONT, :]
        o_ref[FRONT:tm, :] = x_ref[0:tm - FRONT, :] + y[FRONT:tm, :]

    @pl.when(i > 0)
    def _():
        o_ref[...] = x_ref[...] + y


def _out_proj_conv(a, p, cw, col0, w, x, front, g):
    m, ka = a.shape
    taps, width = cw.shape
    d = w.shape[1]
    assert ka == width and w.shape[0] == ka + width
    tm = _pick_tile(m, CONV_ROW_TILE, FRONT // 2)
    cb = col0 // width
    prev = lambda off: (lambda i: (jnp.maximum(i * (tm // 8) - 1, 0), off))
    return pl.pallas_call(
        functools.partial(_out_proj_conv_kernel, taps=taps),
        grid=(m // tm,),
        in_specs=[pl.BlockSpec((tm, ka), lambda i: (i, 0)),
                  pl.BlockSpec((tm, width), lambda i: (i, cb)),
                  pl.BlockSpec((tm, width), lambda i: (i, cb + 1)),
                  pl.BlockSpec((tm, width), lambda i: (i, cb + 2)),
                  pl.BlockSpec((8, width), prev(cb)),
                  pl.BlockSpec((8, width), prev(cb + 2)),
                  pl.BlockSpec((taps, width), lambda i: (0, 0)),
                  pl.BlockSpec((ka, d), lambda i: (0, 0)),
                  pl.BlockSpec((width, d), lambda i: (1, 0)),
                  pl.BlockSpec((1, d), lambda i: (0, 0))] + _token_specs(tm, d),
        out_specs=pl.BlockSpec((tm, d), lambda i: (i, 0)),
        out_shape=jax.ShapeDtypeStruct((m, d), F32),
        scratch_shapes=[pltpu.VMEM((tm + 8, width), F32)],
        compiler_params=_params(("parallel",)),
        name="out_proj_conv",
    )(a, p, p, p, p, p, cw, w, w, g.reshape(1, d), x, front)


def _ffn_kernel(h_ref, gpre_ref, w1_ref, w3_ref, w2_ref, gpost_ref, o_ref, xn_ref, acc_ref):
    j = pl.program_id(1)

    @pl.when(j == 0)
    def _():
        xn_ref[...] = _rms(h_ref[...], gpre_ref[...]).astype(BF16)
        acc_ref[...] = jnp.zeros_like(acc_ref)

    xn = xn_ref[...]
    a = jnp.dot(xn, w1_ref[...], preferred_element_type=F32)
    b = jnp.dot(xn, w3_ref[...], preferred_element_type=F32)
    u = (a * jax.nn.sigmoid(a) * b).astype(BF16)
    acc_ref[...] += jnp.dot(u, w2_ref[...], preferred_element_type=F32)

    @pl.when(j == pl.num_programs(1) - 1)
    def _():
        o_ref[...] = h_ref[...] + _rms(acc_ref[...], gpost_ref[...])


def _ffn(h, gpre, w1, w3, w2, gpost, layer, tf=512, skip_front=False):
    d = h.shape[1]
    f = w1.shape[2]
    if skip_front:
        m = h.shape[0] - FRONT
        tm = _pick_tile(m, ROW_TILE, FRONT)
        h_spec = pl.BlockSpec((pl.Element(tm), pl.Element(d)), lambda i, j: (pl.multiple_of(FRONT + i * tm, FRONT), 0))
    else:
        m = h.shape[0]
        tm = _pick_tile(m, ROW_TILE, FRONT)
        h_spec = pl.BlockSpec((tm, d), lambda i, j: (i, 0))
    return pl.pallas_call(
        _ffn_kernel,
        grid=(m // tm, f // tf),
        in_specs=[h_spec,
                  pl.BlockSpec((1, d), lambda i, j: (0, 0)),
                  pl.BlockSpec((None, d, tf), lambda i, j: (layer, 0, j)),
                  pl.BlockSpec((None, d, tf), lambda i, j: (layer, 0, j)),
                  pl.BlockSpec((None, tf, d), lambda i, j: (layer, j, 0)),
                  pl.BlockSpec((1, d), lambda i, j: (0, 0))],
        out_specs=pl.BlockSpec((tm, d), lambda i, j: (i, 0)),
        out_shape=jax.ShapeDtypeStruct((m, d), F32),
        scratch_shapes=[pltpu.VMEM((tm, d), BF16), pltpu.VMEM((tm, d), F32)],
        compiler_params=_params(("parallel", "arbitrary")),
        name="ffn",
    )(h, gpre.reshape(1, d), w1, w3, w2, gpost.reshape(1, d))


def _cumsum_rows(tri, x):
    hi = x.astype(BF16)
    rest = x - hi.astype(F32)
    mid = rest.astype(BF16)
    lo = (rest - mid.astype(F32)).astype(BF16)
    return (jnp.dot(tri, hi, preferred_element_type=F32) + jnp.dot(tri, mid, preferred_element_type=F32)
            + jnp.dot(tri, lo, preferred_element_type=F32))


def _edge_rows(b, half):
    rows = b.shape[0]
    if half >= 8:
        parts = [jnp.broadcast_to(b[e:e + 1, :], (2 * half, b.shape[1]))
                 for e in range(half - 1, rows, 2 * half)]
        return parts[0] if len(parts) == 1 else jnp.concatenate(parts, axis=0)
    b3 = b.reshape(rows // 8, 8, b.shape[1])
    sub = lax.broadcasted_iota(jnp.int32, b3.shape, 1)
    pick = lambda r: jnp.broadcast_to(b3[:, r:r + 1, :], b3.shape)
    edge = pick(half - 1)
    for start in range(2 * half, 8, 2 * half):
        edge = jnp.where(sub >= start, pick(start + half - 1), edge)
    return edge.reshape(b.shape)


def _hgrn_kernel(q_ref, f_ref, v_ref, gate_ref, lbl_ref, gn_ref, o_ref, st_ref, *, n_heads, lb_row):
    c_rows = HGRN_CHUNK

    @pl.when(pl.program_id(1) == 0)
    def _():
        st_ref[...] = jnp.zeros_like(st_ref)

    logits = lbl_ref[...]
    ex = jnp.exp(logits - jnp.max(logits, axis=0, keepdims=True))
    lb_all = jnp.sum(ex[0:lb_row + 1, :], axis=0, keepdims=True) / jnp.sum(ex, axis=0, keepdims=True)

    r_i = lax.broadcasted_iota(jnp.int32, (c_rows, c_rows), 0)
    c_i = lax.broadcasted_iota(jnp.int32, (c_rows, c_rows), 1)
    tri = (r_i >= c_i).astype(BF16)
    levels = []
    half = c_rows // 2
    while half >= 1:
        shift = half.bit_length()
        upper = (lax.shift_right_logical(r_i, shift - 1) & 1) == 1
        same = lax.shift_right_logical(r_i, shift) == lax.shift_right_logical(c_i, shift)
        levels.append((half, upper, jnp.where(upper, 1.0, -1.0), same))
        half //= 2

    for hh in range(n_heads):
        cols = slice(hh * HGRN_HEAD, (hh + 1) * HGRN_HEAD)
        lb = lb_all[:, cols]
        q = q_ref[:, cols]
        v = v_ref[:, cols]
        f = lb + (1.0 - lb) * jax.nn.sigmoid(f_ref[:, cols])
        k = 1.0 - f
        b = _cumsum_rows(tri, jnp.log2(f))
        b_last = b[c_rows - 1:c_rows, :]

        st = st_ref[hh]
        inter = lax.dot_general((q * jnp.exp2(b)).astype(BF16), st.astype(BF16),
                                (((1,), (1,)), ((), ())), preferred_element_type=F32)
        kt = (k * jnp.exp2(b_last - b)).astype(BF16)
        st_ref[hh] = st * jnp.exp2(b_last) + lax.dot_general(
            v.astype(BF16), kt, (((0,), (0,)), ((), ())), preferred_element_type=F32)

        att = jnp.zeros((c_rows, c_rows), F32)
        for half, upper, sign, same in levels:
            decay = jnp.exp2((b - _edge_rows(b, half)) * sign)
            scaled = jnp.where(upper, q, k) * decay
            qt = jnp.where(upper, scaled, 0.0).astype(BF16)
            kl = jnp.where(upper, 0.0, scaled).astype(BF16)
            pair = lax.dot_general(qt, kl, (((1,), (1,)), ((), ())), preferred_element_type=F32)
            att = att + jnp.where(same, pair, 0.0)
        o = inter + jnp.dot(att.astype(BF16), v.astype(BF16), preferred_element_type=F32)
        o = o + jnp.sum(q * k, axis=-1, keepdims=True) * v

        gate = gate_ref[:, cols]
        o_ref[:, cols] = (_rms(o, gn_ref[...]) * (gate * jax.nn.sigmoid(gate))).astype(BF16)


def _hgrn(p, lb_logits, gn, lb_row):
    m = p.shape[0]
    d_a = lb_logits.shape[1]
    hp = 8
    width = hp * HGRN_HEAD
    groups = d_a // width
    tb = HGRN_CHUNK
    n_l = lb_logits.shape[0]
    col = lambda off: (lambda g, t: (t, off + g))
    kern = functools.partial(_hgrn_kernel, n_heads=hp, lb_row=lb_row)
    return pl.pallas_call(
        kern,
        grid=(groups, m // tb),
        in_specs=[pl.BlockSpec((tb, width), col(0)),
                  pl.BlockSpec((tb, width), col(groups)),
                  pl.BlockSpec((tb, width), col(2 * groups)),
                  pl.BlockSpec((tb, width), col(3 * groups)),
                  pl.BlockSpec((n_l, width), lambda g, t: (0, g)),
                  pl.BlockSpec((1, HGRN_HEAD), lambda g, t: (0, 0))],
        out_specs=pl.BlockSpec((tb, width), lambda g, t: (t, g)),
        out_shape=jax.ShapeDtypeStruct((m, d_a), BF16),
        scratch_shapes=[pltpu.VMEM((hp, HGRN_HEAD, HGRN_HEAD), F32)],
        compiler_params=_params(("parallel", "arbitrary")),
        name="hgrn2",
    )(p, p, p, p, lb_logits, gn.reshape(1, HGRN_HEAD))


def _rglru_kernel(rx_ref, ry_ref, rxp_ref, cw_ref, cb_ref, wa_ref, ba_ref, wi_ref, bi_ref, lam_ref,
                  o_ref, ext_ref, a_ref, x_ref, hs_ref, h_ref, *, taps):
    tm, width = rx_ref.shape
    i = pl.program_id(0)

    @pl.when(i == 0)
    def _():
        h_ref[...] = jnp.zeros_like(h_ref)

    ext_ref[0:8, :] = jnp.where(i > 0, rxp_ref[...], 0.0)
    ext_ref[8:8 + tm, :] = rx_ref[...]
    u = jnp.zeros((tm, width), F32) + cb_ref[...]
    for j in range(taps):
        s = 8 - (taps - 1) + j
        u = u + cw_ref[j:j + 1, :] * ext_ref[s:s + tm, :]

    u_b = u.astype(BF16)
    r_parts, i_parts = [], []
    for n in range(width // RG_BLOCK):
        blk = slice(n * RG_BLOCK, (n + 1) * RG_BLOCK)
        r_parts.append(jnp.dot(u_b[:, blk], wa_ref[n], preferred_element_type=F32))
        i_parts.append(jnp.dot(u_b[:, blk], wi_ref[n], preferred_element_type=F32))
    r = jax.nn.sigmoid(jnp.concatenate(r_parts, axis=1) + ba_ref[...])
    ig = jax.nn.sigmoid(jnp.concatenate(i_parts, axis=1) + bi_ref[...])

    neg_lam = -lam_ref[...]
    softplus = jnp.maximum(neg_lam, 0.0) + jnp.log1p(jnp.exp(-jnp.abs(neg_lam)))
    log_a = -RG_C * r * softplus
    row = i * tm + lax.broadcasted_iota(jnp.int32, (tm, 1), 0)
    a = jnp.exp(log_a)
    xin = jnp.sqrt(1.0 - a * a) * (ig * u)
    a_ref[...] = a
    x_ref[...] = jnp.where(row >= PAD_ROWS, xin, 0.0)

    sub = lax.broadcasted_iota(jnp.int32, (8, width), 0)

    def group(gidx, h):
        base = pl.multiple_of(gidx * 8, 8)
        a8 = a_ref[pl.ds(base, 8), :]
        x8 = x_ref[pl.ds(base, 8), :]
        for shift in (1, 2, 4):
            keep = sub >= shift
            a_up = jnp.where(keep, pltpu.roll(a8, shift, axis=0), 1.0)
            x_up = jnp.where(keep, pltpu.roll(x8, shift, axis=0), 0.0)
            x8 = a8 * x_up + x8
            a8 = a8 * a_up
        h8 = a8 * h + x8
        hs_ref[pl.ds(base, 8), :] = h8
        return h8[7:8, :]

    h_ref[...] = lax.fori_loop(0, tm // 8, group, h_ref[...])
    o_ref[...] = (hs_ref[...] * jax.nn.gelu(ry_ref[...])).astype(BF16)


def _rglru(p, cw, cb, wa, ba, wi, bi, lam, width):
    m = p.shape[0]
    taps = cw.shape[0]
    tm = _pick_tile(m, ROW_TILE, FRONT)
    nb = width // RG_BLOCK
    row = lambda v: v.reshape(1, width)
    full2 = lambda shape: pl.BlockSpec(shape, lambda i: (0, 0))
    full3 = lambda shape: pl.BlockSpec(shape, lambda i: (0, 0, 0))
    return pl.pallas_call(
        functools.partial(_rglru_kernel, taps=taps),
        grid=(m // tm,),
        in_specs=[pl.BlockSpec((tm, width), lambda i: (i, 0)),
                  pl.BlockSpec((tm, width), lambda i: (i, 1)),
                  pl.BlockSpec((8, width), lambda i: (jnp.maximum(i * (tm // 8) - 1, 0), 0)),
                  full2((taps, width)), full2((1, width)),
                  full3((nb, RG_BLOCK, RG_BLOCK)), full2((1, width)),
                  full3((nb, RG_BLOCK, RG_BLOCK)), full2((1, width)),
                  full2((1, width))],
        out_specs=pl.BlockSpec((tm, width), lambda i: (i, 0)),
        out_shape=jax.ShapeDtypeStruct((m, width), BF16),
        scratch_shapes=[pltpu.VMEM((tm + 8, width), F32),
                        pltpu.VMEM((tm, width), F32),
                        pltpu.VMEM((tm, width), F32),
                        pltpu.VMEM((tm, width), F32),
                        pltpu.VMEM((1, width), F32)],
        compiler_params=_params(("arbitrary",)),
        name="rglru",
    )(p, p, p, cw, row(cb), wa, row(ba), wi, row(bi), row(lam))


def _latent_kernel(c_ref, ct_ref, g_ref, gt_ref, o_ref, ot_ref):
    o_ref[...] = _rms(c_ref[...], g_ref[...]).astype(BF16)
    ct = ct_ref[...]
    ot_ref[0:KV_RANK, :] = (ct * lax.rsqrt(jnp.mean(ct * ct, axis=0, keepdims=True) + EPS)
                            * gt_ref[...]).astype(BF16)
    first = lax.broadcasted_iota(jnp.int32, (ONES_ROWS, ct.shape[1]), 0) == 0
    ot_ref[KV_RANK:KV_RANK + ONES_ROWS, :] = jnp.where(first, 1.0, 0.0).astype(BF16)


def _latent(p, pt, g, col0, row0):
    m = p.shape[0]
    r = g.shape[0]
    tm = _pick_tile(m, ROW_TILE, FRONT)
    return pl.pallas_call(
        _latent_kernel,
        grid=(m // tm,),
        in_specs=[pl.BlockSpec((tm, r), lambda i: (i, col0 // r)),
                  pl.BlockSpec((r, tm), lambda i: (row0 // r, i)),
                  pl.BlockSpec((1, r), lambda i: (0, 0)),
                  pl.BlockSpec((r, 1), lambda i: (0, 0))],
        out_specs=[pl.BlockSpec((tm, r), lambda i: (i, 0)),
                   pl.BlockSpec((r + ONES_ROWS, tm), lambda i: (0, i))],
        out_shape=[jax.ShapeDtypeStruct((m, r), BF16), jax.ShapeDtypeStruct((r + ONES_ROWS, m), BF16)],
        compiler_params=_params(("parallel",)),
        name="latent_norm",
    )(p, pt, g.reshape(1, r), g.reshape(r, 1))


def _dsa_kernel(qt_ref, iqt_ref, iwt_ref, c_ref, ct_ref, ik_ref, wuk_ref, wuvt_ref, o_ref,
                sc_ref, iqp_ref, qlt_ref, acc_ref, m_ref, sa_ref, sb_ref, *, k_sel):
    i = pl.program_id(0)
    tq = Q_BLOCK
    n_heads = ATT_HEADS
    sub_blocks = KEY_CHUNK // tq
    n_chunks = (i * tq + KEY_CHUNK - 1) // KEY_CHUNK

    scale = ATT_HEAD_DIM ** -0.5 * LOG2_E
    for h in range(n_heads):
        qh = qt_ref[h * ATT_HEAD_DIM:(h + 1) * ATT_HEAD_DIM, :].astype(BF16)
        ql = jnp.dot(wuk_ref[h], qh, preferred_element_type=F32) * scale
        qlt_ref[:, h * tq:(h + 1) * tq] = ql.astype(BF16)

    k_loc = lax.broadcasted_iota(jnp.int32, (tq, tq), 0)
    q_loc = lax.broadcasted_iota(jnp.int32, (tq, tq), 1)
    head_cols = [slice(h * tq, (h + 1) * tq) for h in range(n_heads)]

    def masked_scores(rows, bias, s_ref):
        bias2 = jnp.concatenate([bias, bias], axis=1)
        cmax = []
        for hp in range(n_heads // 2):
            cols = slice(2 * hp * tq, (2 * hp + 2) * tq)
            sm = jnp.dot(c_ref[rows, :], qlt_ref[:, cols], preferred_element_type=F32) + bias2
            s_ref[:, cols] = sm
            cmax.append(jnp.max(sm, axis=0, keepdims=True))
        return jnp.concatenate(cmax, axis=1)

    def accumulate(rows, s_ref, cmax):
        m_old = m_ref[...]
        m_new = jnp.maximum(m_old, cmax)
        m_ref[...] = m_new
        p = jnp.exp2(s_ref[...] - m_new).astype(BF16)
        acc_ref[...] = jnp.exp2(m_old - m_new) * acc_ref[...] + jnp.dot(
            ct_ref[:, rows], p, preferred_element_type=F32)

    m_ref[...] = jnp.full(m_ref.shape, NEG_BIG, F32)
    acc_ref[...] = jnp.zeros_like(acc_ref)

    q_row = i * tq + q_loc
    allowed0 = jnp.logical_and(jnp.logical_or(k_loc >= PAD_ROWS, k_loc == q_row), k_loc <= q_row)
    s0_ref = sa_ref.at[0:FRONT, :]
    cmax0 = masked_scores(slice(0, FRONT), jnp.where(allowed0, 0.0, NEG_BIG), s0_ref)
    accumulate(slice(0, FRONT), s0_ref, cmax0)

    iw = iwt_ref[...] * ((IDX_DIM ** -0.5) * (IDX_HEADS ** -0.5))

    for hp in range(IDX_HEADS // 2):
        pair = jnp.concatenate(
            [iqt_ref[(2 * hp) * IDX_DIM:(2 * hp + 1) * IDX_DIM, :],
             iqt_ref[(2 * hp + 1) * IDX_DIM:(2 * hp + 2) * IDX_DIM, :]], axis=1).astype(BF16)
        iqp_ref[hp, 0:IDX_DIM, :] = pair
        iqp_ref[hp, IDX_DIM:2 * IDX_DIM, :] = jnp.zeros_like(pair)

    def score_chunk(j, carry):
        mn, mx = carry
        for u in range(sub_blocks):
            kb = j * sub_blocks + 1 + u
            r0 = pl.multiple_of(kb * tq, tq)
            ikb = ik_ref[pl.ds(r0, tq), :]
            s = jnp.zeros((tq, tq), F32)
            for hp in range(IDX_HEADS // 2):
                x = jnp.dot(ikb, iqp_ref[hp], preferred_element_type=F32)
                s = s + jnp.maximum(x[:, 0:tq], 0.0) * iw[2 * hp:2 * hp + 1, :]
                s = s + jnp.maximum(x[:, tq:2 * tq], 0.0) * iw[2 * hp + 1:2 * hp + 2, :]
            visible = jnp.logical_or(kb < i, jnp.logical_and(kb == i, k_loc <= q_loc))
            sc_ref[pl.ds(r0, tq), :] = jnp.where(visible, s, -jnp.inf)
            mx = jnp.maximum(mx, jnp.max(jnp.where(visible, s, -jnp.inf).reshape(tq // 8, 8, tq), axis=0))
            mn = jnp.minimum(mn, jnp.min(jnp.where(visible, s, jnp.inf).reshape(tq // 8, 8, tq), axis=0))
        return mn, mx

    mn, mx = lax.fori_loop(0, n_chunks, score_chunk,
                           (jnp.full((8, tq), jnp.inf, F32), jnp.full((8, tq), -jnp.inf, F32)))
    row_min = jnp.min(mn, axis=0, keepdims=True)
    row_max = jnp.max(mx, axis=0, keepdims=True)

    groups = KEY_CHUNK // 64

    def chunk_scores(j):
        r0 = pl.multiple_of(FRONT + j * KEY_CHUNK, FRONT)
        return sc_ref[pl.ds(r0, KEY_CHUNK), :].reshape(groups, 8, 8, tq)

    def count_ge(t):
        def body(j, cnt):
            kk = chunk_scores(j)
            for g in range(groups):
                cnt = jnp.where(kk[g] >= t, cnt + 1, cnt)
            return cnt
        cnt = lax.fori_loop(0, n_chunks, body, jnp.zeros((8, 8, tq), jnp.int32))
        return jnp.sum(jnp.sum(cnt, axis=0), axis=0, keepdims=True)

    lane = lax.broadcasted_iota(jnp.int32, (1, tq), 1)
    n_visible = (i - 1) * tq + lane + 1
    k_row = jnp.minimum(k_sel, n_visible)

    def probe(mid, movable, lo, hi, cnt_lo):
        c = count_ge(mid)
        up = jnp.logical_and(c >= k_row, movable)
        down = jnp.logical_and(jnp.logical_not(up), movable)
        return jnp.where(up, mid, lo), jnp.where(down, mid, hi), jnp.where(up, c, cnt_lo)

    def midpoint(lo, hi, cnt_lo):
        mid = lo + 0.5 * (hi - lo)
        movable = jnp.logical_and(cnt_lo != k_row, jnp.logical_and(mid > lo, mid < hi))
        return mid, movable

    def lane_flags(lo, hi, cnt_lo):
        moving = midpoint(lo, hi, cnt_lo)[1].astype(jnp.int32)
        return jnp.sum(moving + 256 * (cnt_lo > k_row).astype(jnp.int32))

    lo, hi, cnt_lo = probe(row_max, n_visible > k_row, row_min, row_max, n_visible)

    def bisect_cond(state):
        return jnp.logical_and(state[0] % 256 > 0, state[1] < BISECT_CAP)

    def bisect_steps(_, bracket):
        lo, hi, cnt_lo = bracket
        for _ in range(BISECT_STEPS):
            mid, movable = midpoint(lo, hi, cnt_lo)
            lo, hi, cnt_lo = probe(mid, movable, lo, hi, cnt_lo)
        return lo, hi, cnt_lo

    def bisect_body(state):
        lo, hi, cnt_lo = bisect_steps(0, state[2:])
        return lane_flags(lo, hi, cnt_lo), state[1] + 1, lo, hi, cnt_lo

    lo, hi, cnt_lo = lax.fori_loop(0, BISECT_BLIND, bisect_steps, (lo, hi, cnt_lo))
    state = (lane_flags(lo, hi, cnt_lo), jnp.int32(0), lo, hi, cnt_lo)
    flags, _, thr, _, cnt_thr = lax.while_loop(bisect_cond, bisect_body, state)

    tied = cnt_thr > k_row

    @pl.when(flags >= 256)
    def _():
        key_idx = lax.broadcasted_iota(jnp.int32, (groups, 8, 8, tq), 0) * 64 + (
            lax.broadcasted_iota(jnp.int32, (groups, 8, 8, tq), 1) * 8
            + lax.broadcasted_iota(jnp.int32, (groups, 8, 8, tq), 2))

        def count_where(pred):
            def body(j, cnt):
                hit = pred(chunk_scores(j), j * KEY_CHUNK + key_idx)
                return cnt + jnp.sum(hit.astype(jnp.int32), axis=0)
            cnt = lax.fori_loop(0, n_chunks, body, jnp.zeros((8, 8, tq), jnp.int32))
            return jnp.sum(jnp.sum(cnt, axis=0), axis=0, keepdims=True)

        wanted = k_row - count_where(lambda kk, idx: kk > thr)

        def index_step(_, bounds):
            below, cap = bounds
            mid = below + lax.shift_right_logical(cap - below, 1)
            enough = count_where(lambda kk, idx: jnp.logical_and(kk == thr, idx <= mid)) >= wanted
            return jnp.where(enough, below, mid), jnp.where(enough, mid, cap)

        n_keys = n_chunks * KEY_CHUNK
        steps = max(1, (sc_ref.shape[0] - 1).bit_length())
        _, cap = lax.fori_loop(0, steps, index_step,
                               (jnp.full((1, tq), -1, jnp.int32), jnp.full((1, tq), 1, jnp.int32) * (n_keys - 1)))

        def drop_chunk(j, carry):
            r0 = pl.multiple_of(FRONT + j * KEY_CHUNK, FRONT)
            kk = chunk_scores(j)
            extra = jnp.logical_and(jnp.logical_and(kk == thr, j * KEY_CHUNK + key_idx > cap), tied)
            sc_ref[pl.ds(r0, KEY_CHUNK), :] = jnp.where(extra, -jnp.inf, kk).reshape(KEY_CHUNK, tq)
            return carry

        lax.fori_loop(0, n_chunks, drop_chunk, 0)

    def chunk_rows(j):
        return pl.ds(pl.multiple_of(FRONT + j * KEY_CHUNK, FRONT), KEY_CHUNK)

    def chunk_scores_masked(j, s_ref):
        rows = chunk_rows(j)
        return masked_scores(rows, jnp.where(sc_ref[rows, :] >= thr, 0.0, NEG_BIG), s_ref)

    @pl.when(i > 0)
    def _():
        def chunk_pair(t, cmax_a):
            j = 2 * t
            cmax_b = chunk_scores_masked(j + 1, sb_ref)
            accumulate(chunk_rows(j), sa_ref, cmax_a)
            cmax_a = chunk_scores_masked(j + 2, sa_ref)
            accumulate(chunk_rows(j + 1), sb_ref, cmax_b)
            return cmax_a

        n_pairs = (n_chunks - 1) // 2
        cmax_a = lax.fori_loop(0, n_pairs, chunk_pair, chunk_scores_masked(0, sa_ref))
        j = 2 * n_pairs

        @pl.when(n_chunks - j == 1)
        def _():
            accumulate(chunk_rows(j), sa_ref, cmax_a)

        @pl.when(n_chunks - j == 2)
        def _():
            cmax_b = chunk_scores_masked(j + 1, sb_ref)
            accumulate(chunk_rows(j), sa_ref, cmax_a)
            accumulate(chunk_rows(j + 1), sb_ref, cmax_b)

    o_lat = (acc_ref[0:KV_RANK, :] / acc_ref[KV_RANK:KV_RANK + 1, :]).astype(BF16)
    for h in range(n_heads):
        oh = jnp.dot(wuvt_ref[h], o_lat[:, head_cols[h]], preferred_element_type=F32)
        o_ref[:, h * ATT_HEAD_DIM:(h + 1) * ATT_HEAD_DIM] = oh.T.astype(BF16)


def _dsa(pt, c, ct, ik, wuk, wuvt, k_sel, q_row0, iq_row0, iw_row0):
    m = c.shape[0]
    tq = Q_BLOCK
    d_q = ATT_HEADS * ATT_HEAD_DIM
    d_iq = IDX_HEADS * IDX_DIM
    full2 = lambda shape: pl.BlockSpec(shape, lambda i: (0, 0))
    full3 = lambda shape: pl.BlockSpec(shape, lambda i: (0, 0, 0))
    return pl.pallas_call(
        functools.partial(_dsa_kernel, k_sel=k_sel),
        grid=(m // tq,),
        in_specs=[pl.BlockSpec((d_q, tq), lambda i: (q_row0 // d_q, i)),
                  pl.BlockSpec((d_iq, tq), lambda i: (iq_row0 // d_iq, i)),
                  pl.BlockSpec((IDX_HEADS, tq), lambda i: (iw_row0 // IDX_HEADS, i)),
                  full2(c.shape), full2(ct.shape), full2(ik.shape),
                  full3(wuk.shape), full3(wuvt.shape)],
        out_specs=pl.BlockSpec((tq, d_q), lambda i: (i, 0)),
        out_shape=jax.ShapeDtypeStruct((m, d_q), BF16),
        scratch_shapes=[pltpu.VMEM((m, tq), F32),
                        pltpu.VMEM((IDX_HEADS // 2, 2 * IDX_DIM, 2 * tq), BF16),
                        pltpu.VMEM((KV_RANK, ATT_HEADS * tq), BF16),
                        pltpu.VMEM((KV_RANK + ONES_ROWS, ATT_HEADS * tq), F32),
                        pltpu.VMEM((1, ATT_HEADS * tq), F32),
                        pltpu.VMEM((KEY_CHUNK, ATT_HEADS * tq), F32),
                        pltpu.VMEM((KEY_CHUNK, ATT_HEADS * tq), F32)],
        compiler_params=_params(("arbitrary",)),
        name="dsa",
    )(pt, pt, pt, c, ct, ik, wuk, wuvt)


def _pad_cols(w, n):
    return jnp.pad(w, ((0, 0), (0, n - w.shape[1])))


def kernel(x, meta_tokens, ln_mix_pre, ln_mix_post, ln_ffn_pre, ln_ffn_post, ffn_w1, ffn_w3, ffn_w2,
           ab_w_in, ab_w_out, hgrn_lb_logits, hgrn_out_norm, sconv_w,
           cd_w_in, cd_w_out, rg_conv_w, rg_conv_b, rg_w_a, rg_b_a, rg_w_i, rg_b_i, rg_lambda,
           mla_kv_norm, mla_w_uk, mla_w_uv):
    assert x.shape[0] == 1
    seq, d = x.shape[1], x.shape[2]
    assert seq % KEY_CHUNK == 0
    d_a = hgrn_lb_logits.shape[1]
    d_c = rg_lambda.shape[1]
    d_d = ATT_HEADS * ATT_HEAD_DIM
    d_iq = IDX_HEADS * IDX_DIM
    k_sel = min(TOPK_MAX, seq // 4)

    x2 = x[0]
    front = jnp.concatenate([jnp.zeros((PAD_ROWS, d), F32), meta_tokens.astype(F32)], axis=0)
    w1, w3, w2 = ffn_w1.astype(BF16), ffn_w3.astype(BF16), ffn_w2.astype(BF16)

    p0 = _norm_mm(x2, ln_mix_pre[0], ab_w_in[0].astype(BF16), tn=512, front=front)
    og = _hgrn(p0, hgrn_lb_logits, hgrn_out_norm[0], lb_row=0)
    h = _out_proj_conv(og, p0, sconv_w[0], 4 * d_a, ab_w_out[0].astype(BF16), x2, front, ln_mix_post[0])
    h = _ffn(h, ln_ffn_pre[0], w1, w3, w2, ln_ffn_post[0], layer=0)

    w_in = cd_w_in[0]
    o_rx, o_q, o_c = 0, 2 * d_c, 2 * d_c + d_d
    o_iq = o_c + KV_RANK
    o_ik = o_iq + d_iq
    o_iw = o_ik + IDX_DIM
    w_rows = jnp.concatenate([w_in[:, o_rx:o_q], w_in[:, o_c:o_iq], w_in[:, o_ik:o_iw]], axis=1)
    w_rows = _pad_cols(w_rows, -(-w_rows.shape[1] // 512) * 512).astype(BF16)
    w_cols = jnp.concatenate([w_in[:, o_q:o_c], w_in[:, o_iq:o_ik], w_in[:, o_c:o_iq], w_in[:, o_iw:]], axis=1)
    w_cols = _pad_cols(w_cols, -(-w_cols.shape[1] // 512) * 512).astype(BF16).T
    p1 = _norm_mm(h, ln_mix_pre[1], w_rows, tn=w_rows.shape[1] // 4)
    p1t = _norm_mm_t(h, ln_mix_pre[1], w_cols, tc=w_cols.shape[0] // 4)

    hc = _rglru(p1, rg_conv_w[0], rg_conv_b[0], rg_w_a[0].astype(BF16), rg_b_a[0],
                rg_w_i[0].astype(BF16), rg_b_i[0], rg_lambda[0], width=d_c)
    c, ct = _latent(p1, p1t, mla_kv_norm[0], col0=2 * d_c, row0=d_d + d_iq)
    ik = p1[:, 2 * d_c + KV_RANK:2 * d_c + KV_RANK + 2 * IDX_DIM].astype(BF16)
    wuk = jnp.transpose(mla_w_uk[0], (1, 0, 2)).astype(BF16)
    wuvt = jnp.transpose(mla_w_uv[0], (1, 2, 0)).astype(BF16)
    att = _dsa(p1t, c, ct, ik, wuk, wuvt, k_sel,
               q_row0=0, iq_row0=d_d, iw_row0=d_d + d_iq + KV_RANK)
    h = _out_proj(hc, att, cd_w_out[0].astype(BF16), h, ln_mix_post[1])
    return _ffn(h, ln_ffn_pre[1], w1, w3, w2, ln_ffn_post[1], layer=1, skip_front=True)[None]
```

```python
import functools
import math

import jax
import jax.numpy as jnp
from jax import lax
from jax.experimental import pallas as pl
from jax.experimental.pallas import tpu as pltpu

F32 = jnp.float32
BF16 = jnp.bfloat16

EPS = 1e-6
N_META = 16
FRONT = 128
PAD_ROWS = FRONT - N_META
HGRN_HEAD = 128
HGRN_CHUNK = 128
RG_BLOCK = 128
RG_C = 8.0
ATT_HEADS = 8
ATT_HEAD_DIM = 128
KV_RANK = 256
IDX_HEADS = 16
IDX_DIM = 64
TOPK_MAX = 256
Q_BLOCK = 128
KEY_CHUNK = 512
ONES_ROWS = 16
LOG2_E = 1.4426950408889634
BISECT_STEPS = 4
BISECT_BLIND = 4
BISECT_CAP = 64
NEG_BIG = -1e30
VMEM_LIMIT = 56 * 1024 * 1024
PROJ_ROWS = 1664
ROW_TILE = 640
CONV_ROW_TILE = 320


def _pick_tile(n, target, mult):
    best = None
    for t in range(mult, min(n, target) + 1, mult):
        if n % t == 0:
            best = t
    assert best is not None, (n, target, mult)
    return best


def _params(sem):
    return pltpu.CompilerParams(dimension_semantics=sem, vmem_limit_bytes=VMEM_LIMIT)


def _rms(x, g):
    return x * lax.rsqrt(jnp.mean(x * x, axis=-1, keepdims=True) + EPS) * g


def _token_specs(tm, d):
    align = math.gcd(tm, FRONT)
    x_map = lambda *a: (pl.multiple_of(jnp.maximum(a[0] * tm - FRONT, 0), align), 0)
    return [pl.BlockSpec((pl.Element(tm), pl.Element(d)), x_map),
            pl.BlockSpec((FRONT, d), lambda *a: (0, 0))]


def _fill_normed(x_ref, front_ref, g_ref, xn_ref):
    g = g_ref[...]
    first = 0 if front_ref is None else (pl.program_id(0) == 0).astype(jnp.int32)

    def piece(p, carry):
        src = pl.multiple_of(jnp.maximum(p - first, 0) * FRONT, FRONT)
        rows = x_ref[pl.ds(src, FRONT), :]
        if front_ref is not None:
            rows = jnp.where(jnp.logical_and(first == 1, p == 0), front_ref[...], rows)
        xn_ref[pl.ds(pl.multiple_of(p * FRONT, FRONT), FRONT), :] = _rms(rows, g).astype(BF16)
        return carry

    lax.fori_loop(0, xn_ref.shape[0] // FRONT, piece, 0)


def _norm_mm_kernel(x_ref, *rest, tokens):
    front_ref = rest[0] if tokens else None
    g_ref, w_ref, o_ref, xn_ref = rest[-4:]

    @pl.when(pl.program_id(1) == 0)
    def _():
        _fill_normed(x_ref, front_ref, g_ref, xn_ref)

    o_ref[...] = jnp.dot(xn_ref[...], w_ref[...], preferred_element_type=F32)


def _norm_mm(x, g, w, tn=512, front=None):
    d = x.shape[1]
    m = x.shape[0] + (0 if front is None else FRONT)
    n = w.shape[1]
    tm = _pick_tile(m, PROJ_ROWS, FRONT)
    if front is None:
        row_specs, rows = [pl.BlockSpec((tm, d), lambda i, j: (i, 0))], (x,)
    else:
        row_specs, rows = _token_specs(tm, d), (x, front)
    return pl.pallas_call(
        functools.partial(_norm_mm_kernel, tokens=front is not None),
        grid=(m // tm, n // tn),
        in_specs=row_specs + [pl.BlockSpec((1, d), lambda i, j: (0, 0)),
                              pl.BlockSpec((d, tn), lambda i, j: (0, j))],
        out_specs=pl.BlockSpec((tm, tn), lambda i, j: (i, j)),
        out_shape=jax.ShapeDtypeStruct((m, n), F32),
        scratch_shapes=[pltpu.VMEM((tm, d), BF16)],
        compiler_params=_params(("parallel", "arbitrary")),
        name="norm_proj",
    )(*rows, g.reshape(1, d), w)


def _norm_mm_both_kernel(x_ref, g_ref, w_ref, wt_ref, o_ref, ot_ref, xn_ref):
    @pl.when(pl.program_id(1) == 0)
    def _():
        _fill_normed(x_ref, None, g_ref, xn_ref)

    xn = xn_ref[...]
    o_ref[...] = jnp.dot(xn, w_ref[...], preferred_element_type=F32)
    ot_ref[...] = lax.dot_general(wt_ref[...], xn, (((1,), (1,)), ((), ())), preferred_element_type=F32)


def _norm_mm_both(x, g, w, wt, tn):
    m, d = x.shape
    n = w.shape[1]
    assert wt.shape[0] == n
    tm = _pick_tile(m, ROW_TILE, FRONT)
    return pl.pallas_call(
        _norm_mm_both_kernel,
        grid=(m // tm, n // tn),
        in_specs=[pl.BlockSpec((tm, d), lambda i, j: (i, 0)),
                  pl.BlockSpec((1, d), lambda i, j: (0, 0)),
                  pl.BlockSpec((d, tn), lambda i, j: (0, j)),
                  pl.BlockSpec((tn, d), lambda i, j: (j, 0))],
        out_specs=[pl.BlockSpec((tm, tn), lambda i, j: (i, j)),
                   pl.BlockSpec((tn, tm), lambda i, j: (j, i))],
        out_shape=[jax.ShapeDtypeStruct((m, n), F32), jax.ShapeDtypeStruct((n, m), F32)],
        scratch_shapes=[pltpu.VMEM((tm, d), BF16)],
        compiler_params=_params(("parallel", "arbitrary")),
        name="norm_proj_both",
    )(x, g.reshape(1, d), w, wt)


def _out_proj_kernel(a_ref, b_ref, wa_ref, wb_ref, g_ref, h_ref, o_ref):
    y = jnp.dot(a_ref[...], wa_ref[...], preferred_element_type=F32)
    y = y + jnp.dot(b_ref[...], wb_ref[...], preferred_element_type=F32)
    o_ref[...] = h_ref[...] + _rms(y, g_ref[...])


def _out_proj(a, b, w, h, g):
    m, ka = a.shape
    kb = b.shape[1]
    d = w.shape[1]
    assert ka == kb and w.shape[0] == ka + kb
    tm = _pick_tile(m, ROW_TILE, FRONT)
    return pl.pallas_call(
        _out_proj_kernel,
        grid=(m // tm,),
        in_specs=[pl.BlockSpec((tm, ka), lambda i: (i, 0)),
                  pl.BlockSpec((tm, kb), lambda i: (i, 0)),
                  pl.BlockSpec((ka, d), lambda i: (0, 0)),
                  pl.BlockSpec((kb, d), lambda i: (1, 0)),
                  pl.BlockSpec((1, d), lambda i: (0, 0)),
                  pl.BlockSpec((tm, d), lambda i: (i, 0))],
        out_specs=pl.BlockSpec((tm, d), lambda i: (i, 0)),
        out_shape=jax.ShapeDtypeStruct((m, d), F32),
        compiler_params=_params(("parallel",)),
        name="out_proj",
    )(a, b, w, w, g.reshape(1, d), h)


def _out_proj_conv_kernel(a_ref, sx_ref, sb_ref, sc_ref, sxp_ref, scp_ref, cw_ref, wa_ref, wb_ref, g_ref,
                          x_ref, front_ref, o_ref, ext_ref, *, taps):
    tm = a_ref.shape[0]
    i = pl.program_id(0)
    ext_ref[0:8, :] = jnp.where(i > 0, sxp_ref[...] * scp_ref[...], 0.0)
    ext_ref[8:8 + tm, :] = sx_ref[...] * sc_ref[...]
    conv = jnp.zeros(sx_ref.shape, F32)
    for j in range(taps):
        s = 8 - (taps - 1) + j
        conv = conv + cw_ref[j:j + 1, :] * ext_ref[s:s + tm, :]
    yb = (sb_ref[...] * conv).astype(BF16)

    y = jnp.dot(a_ref[...], wa_ref[...], preferred_element_type=F32)
    y = _rms(y + jnp.dot(yb, wb_ref[...], preferred_element_type=F32), g_ref[...])

    @pl.when(i == 0)
    def _():
        o_ref[0:FRONT, :] = front_ref[...] + y[0:FRONT, :]
        o_ref[FRONT:tm, :] = x_ref[0:tm - FRONT, :] + y[FRONT:tm, :]

    @pl.when(i > 0)
    def _():
        o_ref[...] = x_ref[...] + y


def _out_proj_conv(a, p, cw, col0, w, x, front, g):
    m, ka = a.shape
    taps, width = cw.shape
    d = w.shape[1]
    assert ka == width and w.shape[0] == ka + width
    tm = _pick_tile(m, CONV_ROW_TILE, FRONT // 2)
    cb = col0 // width
    prev = lambda off: (lambda i: (jnp.maximum(i * (tm // 8) - 1, 0), off))
    return pl.pallas_call(
        functools.partial(_out_proj_conv_kernel, taps=taps),
        grid=(m // tm,),
        in_specs=[pl.BlockSpec((tm, ka), lambda i: (i, 0)),
                  pl.BlockSpec((tm, width), lambda i: (i, cb)),
                  pl.BlockSpec((tm, width), lambda i: (i, cb + 1)),
                  pl.BlockSpec((tm, width), lambda i: (i, cb + 2)),
                  pl.BlockSpec((8, width), prev(cb)),
                  pl.BlockSpec((8, width), prev(cb + 2)),
                  pl.BlockSpec((taps, width), lambda i: (0, 0)),
                  pl.BlockSpec((ka, d), lambda i: (0, 0)),
                  pl.BlockSpec((width, d), lambda i: (1, 0)),
                  pl.BlockSpec((1, d), lambda i: (0, 0))] + _token_specs(tm, d),
        out_specs=pl.BlockSpec((tm, d), lambda i: (i, 0)),
        out_shape=jax.ShapeDtypeStruct((m, d), F32),
        scratch_shapes=[pltpu.VMEM((tm + 8, width), F32)],
        compiler_params=_params(("parallel",)),
        name="out_proj_conv",
    )(a, p, p, p, p, p, cw, w, w, g.reshape(1, d), x, front)


def _ffn_kernel(h_ref, gpre_ref, w1_ref, w3_ref, w2_ref, gpost_ref, o_ref, xn_ref, acc_ref):
    j = pl.program_id(1)

    @pl.when(j == 0)
    def _():
        xn_ref[...] = _rms(h_ref[...], gpre_ref[...]).astype(BF16)
        acc_ref[...] = jnp.zeros_like(acc_ref)

    xn = xn_ref[...]
    a = jnp.dot(xn, w1_ref[...], preferred_element_type=F32)
    b = jnp.dot(xn, w3_ref[...], preferred_element_type=F32)
    u = (a * jax.nn.sigmoid(a) * b).astype(BF16)
    acc_ref[...] += jnp.dot(u, w2_ref[...], preferred_element_type=F32)

    @pl.when(j == pl.num_programs(1) - 1)
    def _():
        o_ref[...] = h_ref[...] + _rms(acc_ref[...], gpost_ref[...])


def _ffn(h, gpre, w1, w3, w2, gpost, layer, tf=512, skip_front=False):
    d = h.shape[1]
    f = w1.shape[2]
    if skip_front:
        m = h.shape[0] - FRONT
        tm = _pick_tile(m, ROW_TILE, FRONT)
        h_spec = pl.BlockSpec((pl.Element(tm), pl.Element(d)), lambda i, j: (pl.multiple_of(FRONT + i * tm, FRONT), 0))
    else:
        m = h.shape[0]
        tm = _pick_tile(m, ROW_TILE, FRONT)
        h_spec = pl.BlockSpec((tm, d), lambda i, j: (i, 0))
    return pl.pallas_call(
        _ffn_kernel,
        grid=(m // tm, f // tf),
        in_specs=[h_spec,
                  pl.BlockSpec((1, d), lambda i, j: (0, 0)),
                  pl.BlockSpec((None, d, tf), lambda i, j: (layer, 0, j)),
                  pl.BlockSpec((None, d, tf), lambda i, j: (layer, 0, j)),
                  pl.BlockSpec((None, tf, d), lambda i, j: (layer, j, 0)),
                  pl.BlockSpec((1, d), lambda i, j: (0, 0))],
        out_specs=pl.BlockSpec((tm, d), lambda i, j: (i, 0)),
        out_shape=jax.ShapeDtypeStruct((m, d), F32),
        scratch_shapes=[pltpu.VMEM((tm, d), BF16), pltpu.VMEM((tm, d), F32)],
        compiler_params=_params(("parallel", "arbitrary")),
        name="ffn",
    )(h, gpre.reshape(1, d), w1, w3, w2, gpost.reshape(1, d))


def _cumsum_rows(tri, x):
    hi = x.astype(BF16)
    rest = x - hi.astype(F32)
    mid = rest.astype(BF16)
    lo = (rest - mid.astype(F32)).astype(BF16)
    return (jnp.dot(tri, hi, preferred_element_type=F32) + jnp.dot(tri, mid, preferred_element_type=F32)
            + jnp.dot(tri, lo, preferred_element_type=F32))


def _edge_rows(b, half):
    rows = b.shape[0]
    if half >= 8:
        parts = [jnp.broadcast_to(b[e:e + 1, :], (2 * half, b.shape[1]))
                 for e in range(half - 1, rows, 2 * half)]
        return parts[0] if len(parts) == 1 else jnp.concatenate(parts, axis=0)
    b3 = b.reshape(rows // 8, 8, b.shape[1])
    sub = lax.broadcasted_iota(jnp.int32, b3.shape, 1)
    pick = lambda r: jnp.broadcast_to(b3[:, r:r + 1, :], b3.shape)
    edge = pick(half - 1)
    for start in range(2 * half, 8, 2 * half):
        edge = jnp.where(sub >= start, pick(start + half - 1), edge)
    return edge.reshape(b.shape)


def _hgrn_kernel(q_ref, f_ref, v_ref, gate_ref, lbl_ref, gn_ref, o_ref, st_ref, *, n_heads, lb_row):
    c_rows = HGRN_CHUNK

    @pl.when(pl.program_id(1) == 0)
    def _():
        st_ref[...] = jnp.zeros_like(st_ref)

    logits = lbl_ref[...]
    ex = jnp.exp(logits - jnp.max(logits, axis=0, keepdims=True))
    lb_all = jnp.sum(ex[0:lb_row + 1, :], axis=0, keepdims=True) / jnp.sum(ex, axis=0, keepdims=True)

    r_i = lax.broadcasted_iota(jnp.int32, (c_rows, c_rows), 0)
    c_i = lax.broadcasted_iota(jnp.int32, (c_rows, c_rows), 1)
    tri = (r_i >= c_i).astype(BF16)
    levels = []
    half = c_rows // 2
    while half >= 1:
        shift = half.bit_length()
        upper = (lax.shift_right_logical(r_i, shift - 1) & 1) == 1
        same = lax.shift_right_logical(r_i, shift) == lax.shift_right_logical(c_i, shift)
        levels.append((half, upper, jnp.where(upper, 1.0, -1.0), same))
        half //= 2

    for hh in range(n_heads):
        cols = slice(hh * HGRN_HEAD, (hh + 1) * HGRN_HEAD)
        lb = lb_all[:, cols]
        q = q_ref[:, cols]
        v = v_ref[:, cols]
        f = lb + (1.0 - lb) * jax.nn.sigmoid(f_ref[:, cols])
        k = 1.0 - f
        b = _cumsum_rows(tri, jnp.log2(f))
        b_last = b[c_rows - 1:c_rows, :]

        st = st_ref[hh]
        inter = lax.dot_general((q * jnp.exp2(b)).astype(BF16), st.astype(BF16),
                                (((1,), (1,)), ((), ())), preferred_element_type=F32)
        kt = (k * jnp.exp2(b_last - b)).astype(BF16)
        st_ref[hh] = st * jnp.exp2(b_last) + lax.dot_general(
            v.astype(BF16), kt, (((0,), (0,)), ((), ())), preferred_element_type=F32)

        att = jnp.zeros((c_rows, c_rows), F32)
        for half, upper, sign, same in levels:
            decay = jnp.exp2((b - _edge_rows(b, half)) * sign)
            scaled = jnp.where(upper, q, k) * decay
            qt = jnp.where(upper, scaled, 0.0).astype(BF16)
            kl = jnp.where(upper, 0.0, scaled).astype(BF16)
            pair = lax.dot_general(qt, kl, (((1,), (1,)), ((), ())), preferred_element_type=F32)
            att = att + jnp.where(same, pair, 0.0)
        o = inter + jnp.dot(att.astype(BF16), v.astype(BF16), preferred_element_type=F32)
        o = o + jnp.sum(q * k, axis=-1, keepdims=True) * v

        gate = gate_ref[:, cols]
        o_ref[:, cols] = (_rms(o, gn_ref[...]) * (gate * jax.nn.sigmoid(gate))).astype(BF16)


def _hgrn(p, lb_logits, gn, lb_row):
    m = p.shape[0]
    d_a = lb_logits.shape[1]
    hp = 8
    width = hp * HGRN_HEAD
    groups = d_a // width
    tb = HGRN_CHUNK
    n_l = lb_logits.shape[0]
    col = lambda off: (lambda g, t: (t, off + g))
    kern = functools.partial(_hgrn_kernel, n_heads=hp, lb_row=lb_row)
    return pl.pallas_call(
        kern,
        grid=(groups, m // tb),
        in_specs=[pl.BlockSpec((tb, width), col(0)),
                  pl.BlockSpec((tb, width), col(groups)),
                  pl.BlockSpec((tb, width), col(2 * groups)),
                  pl.BlockSpec((tb, width), col(3 * groups)),
                  pl.BlockSpec((n_l, width), lambda g, t: (0, g)),
                  pl.BlockSpec((1, HGRN_HEAD), lambda g, t: (0, 0))],
        out_specs=pl.BlockSpec((tb, width), lambda g, t: (t, g)),
        out_shape=jax.ShapeDtypeStruct((m, d_a), BF16),
        scratch_shapes=[pltpu.VMEM((hp, HGRN_HEAD, HGRN_HEAD), F32)],
        compiler_params=_params(("parallel", "arbitrary")),
        name="hgrn2",
    )(p, p, p, p, lb_logits, gn.reshape(1, HGRN_HEAD))


def _rglru_kernel(rx_ref, ry_ref, rxp_ref, cw_ref, cb_ref, wa_ref, ba_ref, wi_ref, bi_ref, lam_ref,
                  o_ref, ext_ref, a_ref, x_ref, hs_ref, h_ref, *, taps):
    tm, width = rx_ref.shape
    i = pl.program_id(0)

    @pl.when(i == 0)
    def _():
        h_ref[...] = jnp.zeros_like(h_ref)

    ext_ref[0:8, :] = jnp.where(i > 0, rxp_ref[...], 0.0)
    ext_ref[8:8 + tm, :] = rx_ref[...]
    u = jnp.zeros((tm, width), F32) + cb_ref[...]
    for j in range(taps):
        s = 8 - (taps - 1) + j
        u = u + cw_ref[j:j + 1, :] * ext_ref[s:s + tm, :]

    u_b = u.astype(BF16)
    r_parts, i_parts = [], []
    for n in range(width // RG_BLOCK):
        blk = slice(n * RG_BLOCK, (n + 1) * RG_BLOCK)
        r_parts.append(jnp.dot(u_b[:, blk], wa_ref[n], preferred_element_type=F32))
        i_parts.append(jnp.dot(u_b[:, blk], wi_ref[n], preferred_element_type=F32))
    r = jax.nn.sigmoid(jnp.concatenate(r_parts, axis=1) + ba_ref[...])
    ig = jax.nn.sigmoid(jnp.concatenate(i_parts, axis=1) + bi_ref[...])

    neg_lam = -lam_ref[...]
    softplus = jnp.maximum(neg_lam, 0.0) + jnp.log1p(jnp.exp(-jnp.abs(neg_lam)))
    log_a = -RG_C * r * softplus
    row = i * tm + lax.broadcasted_iota(jnp.int32, (tm, 1), 0)
    a = jnp.exp(log_a)
    xin = jnp.sqrt(1.0 - a * a) * (ig * u)
    a_ref[...] = a
    x_ref[...] = jnp.where(row >= PAD_ROWS, xin, 0.0)

    sub = lax.broadcasted_iota(jnp.int32, (8, width), 0)

    def group(gidx, h):
        base = pl.multiple_of(gidx * 8, 8)
        a8 = a_ref[pl.ds(base, 8), :]
        x8 = x_ref[pl.ds(base, 8), :]
        for shift in (1, 2, 4):
            keep = sub >= shift
            a_up = jnp.where(keep, pltpu.roll(a8, shift, axis=0), 1.0)
            x_up = jnp.where(keep, pltpu.roll(x8, shift, axis=0), 0.0)
            x8 = a8 * x_up + x8
            a8 = a8 * a_up
        h8 = a8 * h + x8
        hs_ref[pl.ds(base, 8), :] = h8
        return h8[7:8, :]

    h_ref[...] = lax.fori_loop(0, tm // 8, group, h_ref[...])
    o_ref[...] = (hs_ref[...] * jax.nn.gelu(ry_ref[...])).astype(BF16)


def _rglru(p, cw, cb, wa, ba, wi, bi, lam, width):
    m = p.shape[0]
    taps = cw.shape[0]
    tm = _pick_tile(m, ROW_TILE, FRONT)
    nb = width // RG_BLOCK
    row = lambda v: v.reshape(1, width)
    full2 = lambda shape: pl.BlockSpec(shape, lambda i: (0, 0))
    full3 = lambda shape: pl.BlockSpec(shape, lambda i: (0, 0, 0))
    return pl.pallas_call(
        functools.partial(_rglru_kernel, taps=taps),
        grid=(m // tm,),
        in_specs=[pl.BlockSpec((tm, width), lambda i: (i, 0)),
                  pl.BlockSpec((tm, width), lambda i: (i, 1)),
                  pl.BlockSpec((8, width), lambda i: (jnp.maximum(i * (tm // 8) - 1, 0), 0)),
                  full2((taps, width)), full2((1, width)),
                  full3((nb, RG_BLOCK, RG_BLOCK)), full2((1, width)),
                  full3((nb, RG_BLOCK, RG_BLOCK)), full2((1, width)),
                  full2((1, width))],
        out_specs=pl.BlockSpec((tm, width), lambda i: (i, 0)),
        out_shape=jax.ShapeDtypeStruct((m, width), BF16),
        scratch_shapes=[pltpu.VMEM((tm + 8, width), F32),
                        pltpu.VMEM((tm, width), F32),
                        pltpu.VMEM((tm, width), F32),
                        pltpu.VMEM((tm, width), F32),
                        pltpu.VMEM((1, width), F32)],
        compiler_params=_params(("arbitrary",)),
        name="rglru",
    )(p, p, p, cw, row(cb), wa, row(ba), wi, row(bi), row(lam))


def _latent_kernel(c_ref, ct_ref, g_ref, gt_ref, o_ref, ot_ref):
    o_ref[...] = _rms(c_ref[...], g_ref[...]).astype(BF16)
    ct = ct_ref[...]
    ot_ref[0:KV_RANK, :] = (ct * lax.rsqrt(jnp.mean(ct * ct, axis=0, keepdims=True) + EPS)
                            * gt_ref[...]).astype(BF16)
    first = lax.broadcasted_iota(jnp.int32, (ONES_ROWS, ct.shape[1]), 0) == 0
    ot_ref[KV_RANK:KV_RANK + ONES_ROWS, :] = jnp.where(first, 1.0, 0.0).astype(BF16)


def _latent(p, pt, g, col0, row0):
    m = p.shape[0]
    r = g.shape[0]
    tm = _pick_tile(m, ROW_TILE, FRONT)
    return pl.pallas_call(
        _latent_kernel,
        grid=(m // tm,),
        in_specs=[pl.BlockSpec((tm, r), lambda i: (i, col0 // r)),
                  pl.BlockSpec((r, tm), lambda i: (row0 // r, i)),
                  pl.BlockSpec((1, r), lambda i: (0, 0)),
                  pl.BlockSpec((r, 1), lambda i: (0, 0))],
        out_specs=[pl.BlockSpec((tm, r), lambda i: (i, 0)),
                   pl.BlockSpec((r + ONES_ROWS, tm), lambda i: (0, i))],
        out_shape=[jax.ShapeDtypeStruct((m, r), BF16), jax.ShapeDtypeStruct((r + ONES_ROWS, m), BF16)],
        compiler_params=_params(("parallel",)),
        name="latent_norm",
    )(p, pt, g.reshape(1, r), g.reshape(r, 1))


def _dsa_kernel(qt_ref, iqt_ref, iwt_ref, c_ref, ct_ref, ik_ref, wuk_ref, wuvt_ref, o_ref,
                sc_ref, iqp_ref, qlt_ref, acc_ref, m_ref, sa_ref, sb_ref, *, k_sel):
    i = pl.program_id(0)
    tq = Q_BLOCK
    n_heads = ATT_HEADS
    sub_blocks = KEY_CHUNK // tq
    n_chunks = (i * tq + KEY_CHUNK - 1) // KEY_CHUNK

    scale = ATT_HEAD_DIM ** -0.5 * LOG2_E
    for h in range(n_heads):
        qh = qt_ref[h * ATT_HEAD_DIM:(h + 1) * ATT_HEAD_DIM, :].astype(BF16)
        ql = jnp.dot(wuk_ref[h], qh, preferred_element_type=F32) * scale
        qlt_ref[:, h * tq:(h + 1) * tq] = ql.astype(BF16)

    k_loc = lax.broadcasted_iota(jnp.int32, (tq, tq), 0)
    q_loc = lax.broadcasted_iota(jnp.int32, (tq, tq), 1)
    head_cols = [slice(h * tq, (h + 1) * tq) for h in range(n_heads)]

    def masked_scores(rows, bias, s_ref):
        bias2 = jnp.concatenate([bias, bias], axis=1)
        cmax = []
        for hp in range(n_heads // 2):
            cols = slice(2 * hp * tq, (2 * hp + 2) * tq)
            sm = jnp.dot(c_ref[rows, :], qlt_ref[:, cols], preferred_element_type=F32) + bias2
            s_ref[:, cols] = sm
            cmax.append(jnp.max(sm, axis=0, keepdims=True))
        return jnp.concatenate(cmax, axis=1)

    def accumulate(rows, s_ref, cmax):
        m_old = m_ref[...]
        m_new = jnp.maximum(m_old, cmax)
        m_ref[...] = m_new
        p = jnp.exp2(s_ref[...] - m_new).astype(BF16)
        acc_ref[...] = jnp.exp2(m_old - m_new) * acc_ref[...] + jnp.dot(
            ct_ref[:, rows], p, preferred_element_type=F32)

    m_ref[...] = jnp.full(m_ref.shape, NEG_BIG, F32)
    acc_ref[...] = jnp.zeros_like(acc_ref)

    q_row = i * tq + q_loc
    allowed0 = jnp.logical_and(jnp.logical_or(k_loc >= PAD_ROWS, k_loc == q_row), k_loc <= q_row)
    s0_ref = sa_ref.at[0:FRONT, :]
    cmax0 = masked_scores(slice(0, FRONT), jnp.where(allowed0, 0.0, NEG_BIG), s0_ref)
    accumulate(slice(0, FRONT), s0_ref, cmax0)

    iw = iwt_ref[...] * ((IDX_DIM ** -0.5) * (IDX_HEADS ** -0.5))

    for hp in range(IDX_HEADS // 2):
        pair = jnp.concatenate(
            [iqt_ref[(2 * hp) * IDX_DIM:(2 * hp + 1) * IDX_DIM, :],
             iqt_ref[(2 * hp + 1) * IDX_DIM:(2 * hp + 2) * IDX_DIM, :]], axis=1).astype(BF16)
        iqp_ref[hp, 0:IDX_DIM, :] = pair
        iqp_ref[hp, IDX_DIM:2 * IDX_DIM, :] = jnp.zeros_like(pair)

    def score_chunk(j, carry):
        mn, mx = carry
        for u in range(sub_blocks):
            kb = j * sub_blocks + 1 + u
            r0 = pl.multiple_of(kb * tq, tq)
            ikb = ik_ref[pl.ds(r0, tq), :]
            s = jnp.zeros((tq, tq), F32)
            for hp in range(IDX_HEADS // 2):
                x = jnp.dot(ikb, iqp_ref[hp], preferred_element_type=F32)
                s = s + jnp.maximum(x[:, 0:tq], 0.0) * iw[2 * hp:2 * hp + 1, :]
                s = s + jnp.maximum(x[:, tq:2 * tq], 0.0) * iw[2 * hp + 1:2 * hp + 2, :]
            visible = jnp.logical_or(kb < i, jnp.logical_and(kb == i, k_loc <= q_loc))
            sc_ref[pl.ds(r0, tq), :] = jnp.where(visible, s, -jnp.inf)
            mx = jnp.maximum(mx, jnp.max(jnp.where(visible, s, -jnp.inf).reshape(tq // 8, 8, tq), axis=0))
            mn = jnp.minimum(mn, jnp.min(jnp.where(visible, s, jnp.inf).reshape(tq // 8, 8, tq), axis=0))
        return mn, mx

    mn, mx = lax.fori_loop(0, n_chunks, score_chunk,
                           (jnp.full((8, tq), jnp.inf, F32), jnp.full((8, tq), -jnp.inf, F32)))
    row_min = jnp.min(mn, axis=0, keepdims=True)
    row_max = jnp.max(mx, axis=0, keepdims=True)

    groups = KEY_CHUNK // 64

    def chunk_scores(j):
        r0 = pl.multiple_of(FRONT + j * KEY_CHUNK, FRONT)
        return sc_ref[pl.ds(r0, KEY_CHUNK), :].reshape(groups, 8, 8, tq)

    def count_ge(t):
        def body(j, cnt):
            kk = chunk_scores(j)
            for g in range(groups):
                cnt = jnp.where(kk[g] >= t, cnt + 1, cnt)
            return cnt
        cnt = lax.fori_loop(0, n_chunks, body, jnp.zeros((8, 8, tq), jnp.int32))
        return jnp.sum(jnp.sum(cnt, axis=0), axis=0, keepdims=True)

    lane = lax.broadcasted_iota(jnp.int32, (1, tq), 1)
    n_visible = (i - 1) * tq + lane + 1
    k_row = jnp.minimum(k_sel, n_visible)

    def probe(mid, movable, lo, hi, cnt_lo):
        c = count_ge(mid)
        up = jnp.logical_and(c >= k_row, movable)
        down = jnp.logical_and(jnp.logical_not(up), movable)
        return jnp.where(up, mid, lo), jnp.where(down, mid, hi), jnp.where(up, c, cnt_lo)

    def midpoint(lo, hi, cnt_lo):
        mid = lo + 0.5 * (hi - lo)
        movable = jnp.logical_and(cnt_lo != k_row, jnp.logical_and(mid > lo, mid < hi))
        return mid, movable

    def lane_flags(lo, hi, cnt_lo):
        moving = midpoint(lo, hi, cnt_lo)[1].astype(jnp.int32)
        return jnp.sum(moving + 256 * (cnt_lo > k_row).astype(jnp.int32))

    lo, hi, cnt_lo = probe(row_max, n_visible > k_row, row_min, row_max, n_visible)

    def bisect_cond(state):
        return jnp.logical_and(state[0] % 256 > 0, state[1] < BISECT_CAP)

    def bisect_steps(_, bracket):
        lo, hi, cnt_lo = bracket
        for _ in range(BISECT_STEPS):
            mid, movable = midpoint(lo, hi, cnt_lo)
            lo, hi, cnt_lo = probe(mid, movable, lo, hi, cnt_lo)
        return lo, hi, cnt_lo

    def bisect_body(state):
        lo, hi, cnt_lo = bisect_steps(0, state[2:])
        return lane_flags(lo, hi, cnt_lo), state[1] + 1, lo, hi, cnt_lo

    lo, hi, cnt_lo = lax.fori_loop(0, BISECT_BLIND, bisect_steps, (lo, hi, cnt_lo))
    state = (lane_flags(lo, hi, cnt_lo), jnp.int32(0), lo, hi, cnt_lo)
    flags, _, thr, _, cnt_thr = lax.while_loop(bisect_cond, bisect_body, state)

    tied = cnt_thr > k_row

    @pl.when(flags >= 256)
    def _():
        key_idx = lax.broadcasted_iota(jnp.int32, (groups, 8, 8, tq), 0) * 64 + (
            lax.broadcasted_iota(jnp.int32, (groups, 8, 8, tq), 1) * 8
            + lax.broadcasted_iota(jnp.int32, (groups, 8, 8, tq), 2))

        def count_where(pred):
            def body(j, cnt):
                hit = pred(chunk_scores(j), j * KEY_CHUNK + key_idx)
                return cnt + jnp.sum(hit.astype(jnp.int32), axis=0)
            cnt = lax.fori_loop(0, n_chunks, body, jnp.zeros((8, 8, tq), jnp.int32))
            return jnp.sum(jnp.sum(cnt, axis=0), axis=0, keepdims=True)

        wanted = k_row - count_where(lambda kk, idx: kk > thr)

        def index_step(_, bounds):
            below, cap = bounds
            mid = below + lax.shift_right_logical(cap - below, 1)
            enough = count_where(lambda kk, idx: jnp.logical_and(kk == thr, idx <= mid)) >= wanted
            return jnp.where(enough, below, mid), jnp.where(enough, mid, cap)

        n_keys = n_chunks * KEY_CHUNK
        steps = max(1, (sc_ref.shape[0] - 1).bit_length())
        _, cap = lax.fori_loop(0, steps, index_step,
                               (jnp.full((1, tq), -1, jnp.int32), jnp.full((1, tq), 1, jnp.int32) * (n_keys - 1)))

        def drop_chunk(j, carry):
            r0 = pl.multiple_of(FRONT + j * KEY_CHUNK, FRONT)
            kk = chunk_scores(j)
            extra = jnp.logical_and(jnp.logical_and(kk == thr, j * KEY_CHUNK + key_idx > cap), tied)
            sc_ref[pl.ds(r0, KEY_CHUNK), :] = jnp.where(extra, -jnp.inf, kk).reshape(KEY_CHUNK, tq)
            return carry

        lax.fori_loop(0, n_chunks, drop_chunk, 0)

    def chunk_rows(j):
        return pl.ds(pl.multiple_of(FRONT + j * KEY_CHUNK, FRONT), KEY_CHUNK)

    def chunk_scores_masked(j, s_ref):
        rows = chunk_rows(j)
        return masked_scores(rows, jnp.where(sc_ref[rows, :] >= thr, 0.0, NEG_BIG), s_ref)

    @pl.when(i > 0)
    def _():
        def chunk_pair(t, cmax_a):
            j = 2 * t
            cmax_b = chunk_scores_masked(j + 1, sb_ref)
            accumulate(chunk_rows(j), sa_ref, cmax_a)
            cmax_a = chunk_scores_masked(j + 2, sa_ref)
            accumulate(chunk_rows(j + 1), sb_ref, cmax_b)
            return cmax_a

        n_pairs = (n_chunks - 1) // 2
        cmax_a = lax.fori_loop(0, n_pairs, chunk_pair, chunk_scores_masked(0, sa_ref))
        j = 2 * n_pairs

        @pl.when(n_chunks - j == 1)
        def _():
            accumulate(chunk_rows(j), sa_ref, cmax_a)

        @pl.when(n_chunks - j == 2)
        def _():
            cmax_b = chunk_scores_masked(j + 1, sb_ref)
            accumulate(chunk_rows(j), sa_ref, cmax_a)
            accumulate(chunk_rows(j + 1), sb_ref, cmax_b)

    o_lat = (acc_ref[0:KV_RANK, :] / acc_ref[KV_RANK:KV_RANK + 1, :]).astype(BF16)
    for h in range(n_heads):
        oh = jnp.dot(wuvt_ref[h], o_lat[:, head_cols[h]], preferred_element_type=F32)
        o_ref[:, h * ATT_HEAD_DIM:(h + 1) * ATT_HEAD_DIM] = oh.T.astype(BF16)


def _dsa(pt, c, ct, ik, wuk, wuvt, k_sel, q_row0, iq_row0, iw_row0):
    m = c.shape[0]
    tq = Q_BLOCK
    d_q = ATT_HEADS * ATT_HEAD_DIM
    d_iq = IDX_HEADS * IDX_DIM
    full2 = lambda shape: pl.BlockSpec(shape, lambda i: (0, 0))
    full3 = lambda shape: pl.BlockSpec(shape, lambda i: (0, 0, 0))
    return pl.pallas_call(
        functools.partial(_dsa_kernel, k_sel=k_sel),
        grid=(m // tq,),
        in_specs=[pl.BlockSpec((d_q, tq), lambda i: (q_row0 // d_q, i)),
                  pl.BlockSpec((d_iq, tq), lambda i: (iq_row0 // d_iq, i)),
                  pl.BlockSpec((IDX_HEADS, tq), lambda i: (iw_row0 // IDX_HEADS, i)),
                  full2(c.shape), full2(ct.shape), full2(ik.shape),
                  full3(wuk.shape), full3(wuvt.shape)],
        out_specs=pl.BlockSpec((tq, d_q), lambda i: (i, 0)),
        out_shape=jax.ShapeDtypeStruct((m, d_q), BF16),
        scratch_shapes=[pltpu.VMEM((m, tq), F32),
                        pltpu.VMEM((IDX_HEADS // 2, 2 * IDX_DIM, 2 * tq), BF16),
                        pltpu.VMEM((KV_RANK, ATT_HEADS * tq), BF16),
                        pltpu.VMEM((KV_RANK + ONES_ROWS, ATT_HEADS * tq), F32),
                        pltpu.VMEM((1, ATT_HEADS * tq), F32),
                        pltpu.VMEM((KEY_CHUNK, ATT_HEADS * tq), F32),
                        pltpu.VMEM((KEY_CHUNK, ATT_HEADS * tq), F32)],
        compiler_params=_params(("arbitrary",)),
        name="dsa",
    )(pt, pt, pt, c, ct, ik, wuk, wuvt)


def _pad_cols(w, n):
    return jnp.pad(w, ((0, 0), (0, n - w.shape[1])))


def kernel(x, meta_tokens, ln_mix_pre, ln_mix_post, ln_ffn_pre, ln_ffn_post, ffn_w1, ffn_w3, ffn_w2,
           ab_w_in, ab_w_out, hgrn_lb_logits, hgrn_out_norm, sconv_w,
           cd_w_in, cd_w_out, rg_conv_w, rg_conv_b, rg_w_a, rg_b_a, rg_w_i, rg_b_i, rg_lambda,
           mla_kv_norm, mla_w_uk, mla_w_uv):
    assert x.shape[0] == 1
    seq, d = x.shape[1], x.shape[2]
    assert seq % KEY_CHUNK == 0
    d_a = hgrn_lb_logits.shape[1]
    d_c = rg_lambda.shape[1]
    d_d = ATT_HEADS * ATT_HEAD_DIM
    d_iq = IDX_HEADS * IDX_DIM
    k_sel = min(TOPK_MAX, seq // 4)

    x2 = x[0]
    front = jnp.concatenate([jnp.zeros((PAD_ROWS, d), F32), meta_tokens.astype(F32)], axis=0)
    w1, w3, w2 = ffn_w1.astype(BF16), ffn_w3.astype(BF16), ffn_w2.astype(BF16)

    p0 = _norm_mm(x2, ln_mix_pre[0], ab_w_in[0].astype(BF16), tn=512, front=front)
    og = _hgrn(p0, hgrn_lb_logits, hgrn_out_norm[0], lb_row=0)
    h = _out_proj_conv(og, p0, sconv_w[0], 4 * d_a, ab_w_out[0].astype(BF16), x2, front, ln_mix_post[0])
    h = _ffn(h, ln_ffn_pre[0], w1, w3, w2, ln_ffn_post[0], layer=0)

    w_in = cd_w_in[0]
    o_rx, o_q, o_c = 0, 2 * d_c, 2 * d_c + d_d
    o_iq = o_c + KV_RANK
    o_ik = o_iq + d_iq
    o_iw = o_ik + IDX_DIM
    w_rows = jnp.concatenate([w_in[:, o_rx:o_q], w_in[:, o_c:o_iq], w_in[:, o_ik:o_iw]], axis=1)
    w_rows = _pad_cols(w_rows, -(-w_rows.shape[1] // 512) * 512).astype(BF16)
    w_cols = jnp.concatenate([w_in[:, o_q:o_c], w_in[:, o_iq:o_ik], w_in[:, o_c:o_iq], w_in[:, o_iw:]], axis=1)
    w_cols = _pad_cols(w_cols, -(-w_cols.shape[1] // 512) * 512).astype(BF16).T
    p1, p1t = _norm_mm_both(h, ln_mix_pre[1], w_rows, w_cols, tn=w_rows.shape[1] // 4)

    hc = _rglru(p1, rg_conv_w[0], rg_conv_b[0], rg_w_a[0].astype(BF16), rg_b_a[0],
                rg_w_i[0].astype(BF16), rg_b_i[0], rg_lambda[0], width=d_c)
    c, ct = _latent(p1, p1t, mla_kv_norm[0], col0=2 * d_c, row0=d_d + d_iq)
    ik = p1[:, 2 * d_c + KV_RANK:2 * d_c + KV_RANK + 2 * IDX_DIM].astype(BF16)
    wuk = jnp.transpose(mla_w_uk[0], (1, 0, 2)).astype(BF16)
    wuvt = jnp.transpose(mla_w_uv[0], (1, 2, 0)).astype(BF16)
    att = _dsa(p1t, c, ct, ik, wuk, wuvt, k_sel,
               q_row0=0, iq_row0=d_d, iw_row0=d_d + d_iq + KV_RANK)
    h = _out_proj(hc, att, cd_w_out[0].astype(BF16), h, ln_mix_post[1])
    return _ffn(h, ln_ffn_pre[1], w1, w3, w2, ln_ffn_post[1], layer=1, skip_front=True)[None]
```

```python
import functools
import math

import jax
import jax.numpy as jnp
from jax import lax
from jax.experimental import pallas as pl
from jax.experimental.pallas import tpu as pltpu

F32 = jnp.float32
BF16 = jnp.bfloat16

EPS = 1e-6
N_META = 16
FRONT = 128
PAD_ROWS = FRONT - N_META
HGRN_HEAD = 128
HGRN_CHUNK = 128
RG_BLOCK = 128
RG_C = 8.0
ATT_HEADS = 8
ATT_HEAD_DIM = 128
KV_RANK = 256
IDX_HEADS = 16
IDX_DIM = 64
TOPK_MAX = 256
Q_BLOCK = 128
KEY_CHUNK = 512
ONES_ROWS = 16
LOG2_E = 1.4426950408889634
BISECT_STEPS = 4
BISECT_BLIND = 4
BISECT_CAP = 64
NEG_BIG = -1e30
VMEM_LIMIT = 56 * 1024 * 1024
PROJ_ROWS = 1664
ROW_TILE = 640
CONV_ROW_TILE = 320


def _pick_tile(n, target, mult):
    best = None
    for t in range(mult, min(n, target) + 1, mult):
        if n % t == 0:
            best = t
    assert best is not None, (n, target, mult)
    return best


def _params(sem):
    return pltpu.CompilerParams(dimension_semantics=sem, vmem_limit_bytes=VMEM_LIMIT)


def _rms(x, g):
    return x * lax.rsqrt(jnp.mean(x * x, axis=-1, keepdims=True) + EPS) * g


def _token_specs(tm, d):
    align = math.gcd(tm, FRONT)
    x_map = lambda *a: (pl.multiple_of(jnp.maximum(a[0] * tm - FRONT, 0), align), 0)
    return [pl.BlockSpec((pl.Element(tm), pl.Element(d)), x_map),
            pl.BlockSpec((FRONT, d), lambda *a: (0, 0))]


def _fill_normed(x_ref, front_ref, g_ref, xn_ref):
    g = g_ref[...]
    first = 0 if front_ref is None else (pl.program_id(0) == 0).astype(jnp.int32)

    def piece(p, carry):
        src = pl.multiple_of(jnp.maximum(p - first, 0) * FRONT, FRONT)
        rows = x_ref[pl.ds(src, FRONT), :]
        if front_ref is not None:
            rows = jnp.where(jnp.logical_and(first == 1, p == 0), front_ref[...], rows)
        xn_ref[pl.ds(pl.multiple_of(p * FRONT, FRONT), FRONT), :] = _rms(rows, g).astype(BF16)
        return carry

    lax.fori_loop(0, xn_ref.shape[0] // FRONT, piece, 0)


def _norm_mm_kernel(x_ref, *rest, tokens):
    front_ref = rest[0] if tokens else None
    g_ref, w_ref, o_ref, xn_ref = rest[-4:]

    @pl.when(pl.program_id(1) == 0)
    def _():
        _fill_normed(x_ref, front_ref, g_ref, xn_ref)

    o_ref[...] = jnp.dot(xn_ref[...], w_ref[...], preferred_element_type=F32)


def _norm_mm(x, g, w, tn=512, front=None):
    d = x.shape[1]
    m = x.shape[0] + (0 if front is None else FRONT)
    n = w.shape[1]
    tm = _pick_tile(m, PROJ_ROWS, FRONT)
    if front is None:
        row_specs, rows = [pl.BlockSpec((tm, d), lambda i, j: (i, 0))], (x,)
    else:
        row_specs, rows = _token_specs(tm, d), (x, front)
    return pl.pallas_call(
        functools.partial(_norm_mm_kernel, tokens=front is not None),
        grid=(m // tm, n // tn),
        in_specs=row_specs + [pl.BlockSpec((1, d), lambda i, j: (0, 0)),
                              pl.BlockSpec((d, tn), lambda i, j: (0, j))],
        out_specs=pl.BlockSpec((tm, tn), lambda i, j: (i, j)),
        out_shape=jax.ShapeDtypeStruct((m, n), F32),
        scratch_shapes=[pltpu.VMEM((tm, d), BF16)],
        compiler_params=_params(("parallel", "arbitrary")),
        name="norm_proj",
    )(*rows, g.reshape(1, d), w)


def _norm_mm_both_kernel(x_ref, g_ref, w_ref, wt_ref, o_ref, ot_ref, xn_ref):
    @pl.when(pl.program_id(1) == 0)
    def _():
        _fill_normed(x_ref, None, g_ref, xn_ref)

    xn = xn_ref[...]
    o_ref[...] = jnp.dot(xn, w_ref[...], preferred_element_type=F32)
    ot_ref[...] = lax.dot_general(wt_ref[...], xn, (((1,), (1,)), ((), ())), preferred_element_type=F32)


def _norm_mm_both(x, g, w, wt, tn):
    m, d = x.shape
    n = w.shape[1]
    assert wt.shape[0] == n
    tm = _pick_tile(m, ROW_TILE, FRONT)
    return pl.pallas_call(
        _norm_mm_both_kernel,
        grid=(m // tm, n // tn),
        in_specs=[pl.BlockSpec((tm, d), lambda i, j: (i, 0)),
                  pl.BlockSpec((1, d), lambda i, j: (0, 0)),
                  pl.BlockSpec((d, tn), lambda i, j: (0, j)),
                  pl.BlockSpec((tn, d), lambda i, j: (j, 0))],
        out_specs=[pl.BlockSpec((tm, tn), lambda i, j: (i, j)),
                   pl.BlockSpec((tn, tm), lambda i, j: (j, i))],
        out_shape=[jax.ShapeDtypeStruct((m, n), F32), jax.ShapeDtypeStruct((n, m), F32)],
        scratch_shapes=[pltpu.VMEM((tm, d), BF16)],
        compiler_params=_params(("parallel", "arbitrary")),
        name="norm_proj_both",
    )(x, g.reshape(1, d), w, wt)


def _out_proj_kernel(a_ref, b_ref, wa_ref, wb_ref, g_ref, h_ref, o_ref):
    y = jnp.dot(a_ref[...], wa_ref[...], preferred_element_type=F32)
    y = y + jnp.dot(b_ref[...], wb_ref[...], preferred_element_type=F32)
    o_ref[...] = h_ref[...] + _rms(y, g_ref[...])


def _out_proj(a, b, w, h, g):
    m, ka = a.shape
    kb = b.shape[1]
    d = w.shape[1]
    assert ka == kb and w.shape[0] == ka + kb
    tm = _pick_tile(m, ROW_TILE, FRONT)
    return pl.pallas_call(
        _out_proj_kernel,
        grid=(m // tm,),
        in_specs=[pl.BlockSpec((tm, ka), lambda i: (i, 0)),
                  pl.BlockSpec((tm, kb), lambda i: (i, 0)),
                  pl.BlockSpec((ka, d), lambda i: (0, 0)),
                  pl.BlockSpec((kb, d), lambda i: (1, 0)),
                  pl.BlockSpec((1, d), lambda i: (0, 0)),
                  pl.BlockSpec((tm, d), lambda i: (i, 0))],
        out_specs=pl.BlockSpec((tm, d), lambda i: (i, 0)),
        out_shape=jax.ShapeDtypeStruct((m, d), F32),
        compiler_params=_params(("parallel",)),
        name="out_proj",
    )(a, b, w, w, g.reshape(1, d), h)


def _out_proj_conv_kernel(a_ref, sx_ref, sb_ref, sc_ref, sxp_ref, scp_ref, cw_ref, wa_ref, wb_ref, g_ref,
                          x_ref, front_ref, o_ref, ext_ref, *, taps):
    tm = a_ref.shape[0]
    i = pl.program_id(0)
    ext_ref[0:8, :] = jnp.where(i > 0, sxp_ref[...] * scp_ref[...], 0.0)
    ext_ref[8:8 + tm, :] = sx_ref[...] * sc_ref[...]
    conv = jnp.zeros(sx_ref.shape, F32)
    for j in range(taps):
        s = 8 - (taps - 1) + j
        conv = conv + cw_ref[j:j + 1, :] * ext_ref[s:s + tm, :]
    yb = (sb_ref[...] * conv).astype(BF16)

    y = jnp.dot(a_ref[...], wa_ref[...], preferred_element_type=F32)
    y = _rms(y + jnp.dot(yb, wb_ref[...], preferred_element_type=F32), g_ref[...])

    @pl.when(i == 0)
    def _():
        o_ref[0:FRONT, :] = front_ref[...] + y[0:FRONT, :]
        o_ref[FRONT:tm, :] = x_ref[0:tm - FRONT, :] + y[FRONT:tm, :]

    @pl.when(i > 0)
    def _():
        o_ref[...] = x_ref[...] + y


def _out_proj_conv(a, p, cw, col0, w, x, front, g):
    m, ka = a.shape
    taps, width = cw.shape
    d = w.shape[1]
    assert ka == width and w.shape[0] == ka + width
    tm = _pick_tile(m, CONV_ROW_TILE, FRONT // 2)
    cb = col0 // width
    prev = lambda off: (lambda i: (jnp.maximum(i * (tm // 8) - 1, 0), off))
    return pl.pallas_call(
        functools.partial(_out_proj_conv_kernel, taps=taps),
        grid=(m // tm,),
        in_specs=[pl.BlockSpec((tm, ka), lambda i: (i, 0)),
                  pl.BlockSpec((tm, width), lambda i: (i, cb)),
                  pl.BlockSpec((tm, width), lambda i: (i, cb + 1)),
                  pl.BlockSpec((tm, width), lambda i: (i, cb + 2)),
                  pl.BlockSpec((8, width), prev(cb)),
                  pl.BlockSpec((8, width), prev(cb + 2)),
                  pl.BlockSpec((taps, width), lambda i: (0, 0)),
                  pl.BlockSpec((ka, d), lambda i: (0, 0)),
                  pl.BlockSpec((width, d), lambda i: (1, 0)),
                  pl.BlockSpec((1, d), lambda i: (0, 0))] + _token_specs(tm, d),
        out_specs=pl.BlockSpec((tm, d), lambda i: (i, 0)),
        out_shape=jax.ShapeDtypeStruct((m, d), F32),
        scratch_shapes=[pltpu.VMEM((tm + 8, width), F32)],
        compiler_params=_params(("parallel",)),
        name="out_proj_conv",
    )(a, p, p, p, p, p, cw, w, w, g.reshape(1, d), x, front)


def _ffn_kernel(h_ref, gpre_ref, w1_ref, w3_ref, w2_ref, gpost_ref, o_ref, xn_ref, acc_ref):
    j = pl.program_id(1)

    @pl.when(j == 0)
    def _():
        xn_ref[...] = _rms(h_ref[...], gpre_ref[...]).astype(BF16)
        acc_ref[...] = jnp.zeros_like(acc_ref)

    xn = xn_ref[...]
    a = jnp.dot(xn, w1_ref[...], preferred_element_type=F32)
    b = jnp.dot(xn, w3_ref[...], preferred_element_type=F32)
    u = (a * jax.nn.sigmoid(a) * b).astype(BF16)
    acc_ref[...] += jnp.dot(u, w2_ref[...], preferred_element_type=F32)

    @pl.when(j == pl.num_programs(1) - 1)
    def _():
        o_ref[...] = h_ref[...] + _rms(acc_ref[...], gpost_ref[...])


def _ffn(h, gpre, w1, w3, w2, gpost, layer, tf=512, skip_front=False):
    d = h.shape[1]
    f = w1.shape[2]
    if skip_front:
        m = h.shape[0] - FRONT
        tm = _pick_tile(m, ROW_TILE, FRONT)
        h_spec = pl.BlockSpec((pl.Element(tm), pl.Element(d)), lambda i, j: (pl.multiple_of(FRONT + i * tm, FRONT), 0))
    else:
        m = h.shape[0]
        tm = _pick_tile(m, ROW_TILE, FRONT)
        h_spec = pl.BlockSpec((tm, d), lambda i, j: (i, 0))
    return pl.pallas_call(
        _ffn_kernel,
        grid=(m // tm, f // tf),
        in_specs=[h_spec,
                  pl.BlockSpec((1, d), lambda i, j: (0, 0)),
                  pl.BlockSpec((None, d, tf), lambda i, j: (layer, 0, j)),
                  pl.BlockSpec((None, d, tf), lambda i, j: (layer, 0, j)),
                  pl.BlockSpec((None, tf, d), lambda i, j: (layer, j, 0)),
                  pl.BlockSpec((1, d), lambda i, j: (0, 0))],
        out_specs=pl.BlockSpec((tm, d), lambda i, j: (i, 0)),
        out_shape=jax.ShapeDtypeStruct((m, d), F32),
        scratch_shapes=[pltpu.VMEM((tm, d), BF16), pltpu.VMEM((tm, d), F32)],
        compiler_params=_params(("parallel", "arbitrary")),
        name="ffn",
    )(h, gpre.reshape(1, d), w1, w3, w2, gpost.reshape(1, d))


def _cumsum_rows(tri, x):
    hi = x.astype(BF16)
    rest = x - hi.astype(F32)
    mid = rest.astype(BF16)
    lo = (rest - mid.astype(F32)).astype(BF16)
    return (jnp.dot(tri, hi, preferred_element_type=F32) + jnp.dot(tri, mid, preferred_element_type=F32)
            + jnp.dot(tri, lo, preferred_element_type=F32))


def _edge_rows(b, half):
    rows = b.shape[0]
    if half >= 8:
        parts = [jnp.broadcast_to(b[e:e + 1, :], (2 * half, b.shape[1]))
                 for e in range(half - 1, rows, 2 * half)]
        return parts[0] if len(parts) == 1 else jnp.concatenate(parts, axis=0)
    b3 = b.reshape(rows // 8, 8, b.shape[1])
    sub = lax.broadcasted_iota(jnp.int32, b3.shape, 1)
    pick = lambda r: jnp.broadcast_to(b3[:, r:r + 1, :], b3.shape)
    edge = pick(half - 1)
    for start in range(2 * half, 8, 2 * half):
        edge = jnp.where(sub >= start, pick(start + half - 1), edge)
    return edge.reshape(b.shape)


def _hgrn_kernel(q_ref, f_ref, v_ref, gate_ref, lbl_ref, gn_ref, o_ref, st_ref, *, n_heads, lb_row):
    c_rows = HGRN_CHUNK

    @pl.when(pl.program_id(1) == 0)
    def _():
        st_ref[...] = jnp.zeros_like(st_ref)

    logits = lbl_ref[...]
    ex = jnp.exp(logits - jnp.max(logits, axis=0, keepdims=True))
    lb_all = jnp.sum(ex[0:lb_row + 1, :], axis=0, keepdims=True) / jnp.sum(ex, axis=0, keepdims=True)

    r_i = lax.broadcasted_iota(jnp.int32, (c_rows, c_rows), 0)
    c_i = lax.broadcasted_iota(jnp.int32, (c_rows, c_rows), 1)
    tri = (r_i >= c_i).astype(BF16)
    levels = []
    half = c_rows // 2
    while half >= 1:
        shift = half.bit_length()
        upper = (lax.shift_right_logical(r_i, shift - 1) & 1) == 1
        same = lax.shift_right_logical(r_i, shift) == lax.shift_right_logical(c_i, shift)
        levels.append((half, upper, jnp.where(upper, 1.0, -1.0), same))
        half //= 2

    for hh in range(n_heads):
        cols = slice(hh * HGRN_HEAD, (hh + 1) * HGRN_HEAD)
        lb = lb_all[:, cols]
        q = q_ref[:, cols]
        v = v_ref[:, cols]
        f = lb + (1.0 - lb) * jax.nn.sigmoid(f_ref[:, cols])
        k = 1.0 - f
        b = _cumsum_rows(tri, jnp.log2(f))
        b_last = b[c_rows - 1:c_rows, :]

        st = st_ref[hh]
        inter = lax.dot_general((q * jnp.exp2(b)).astype(BF16), st.astype(BF16),
                                (((1,), (1,)), ((), ())), preferred_element_type=F32)
        kt = (k * jnp.exp2(b_last - b)).astype(BF16)
        st_ref[hh] = st * jnp.exp2(b_last) + lax.dot_general(
            v.astype(BF16), kt, (((0,), (0,)), ((), ())), preferred_element_type=F32)

        att = jnp.zeros((c_rows, c_rows), F32)
        for half, upper, sign, same in levels:
            decay = jnp.exp2((b - _edge_rows(b, half)) * sign)
            scaled = jnp.where(upper, q, k) * decay
            qt = jnp.where(upper, scaled, 0.0).astype(BF16)
            kl = jnp.where(upper, 0.0, scaled).astype(BF16)
            pair = lax.dot_general(qt, kl, (((1,), (1,)), ((), ())), preferred_element_type=F32)
            att = att + jnp.where(same, pair, 0.0)
        o = inter + jnp.dot(att.astype(BF16), v.astype(BF16), preferred_element_type=F32)
        o = o + jnp.sum(q * k, axis=-1, keepdims=True) * v

        gate = gate_ref[:, cols]
        o_ref[:, cols] = (_rms(o, gn_ref[...]) * (gate * jax.nn.sigmoid(gate))).astype(BF16)


def _hgrn(p, lb_logits, gn, lb_row):
    m = p.shape[0]
    d_a = lb_logits.shape[1]
    hp = 8
    width = hp * HGRN_HEAD
    groups = d_a // width
    tb = HGRN_CHUNK
    n_l = lb_logits.shape[0]
    col = lambda off: (lambda g, t: (t, off + g))
    kern = functools.partial(_hgrn_kernel, n_heads=hp, lb_row=lb_row)
    return pl.pallas_call(
        kern,
        grid=(groups, m // tb),
        in_specs=[pl.BlockSpec((tb, width), col(0)),
                  pl.BlockSpec((tb, width), col(groups)),
                  pl.BlockSpec((tb, width), col(2 * groups)),
                  pl.BlockSpec((tb, width), col(3 * groups)),
                  pl.BlockSpec((n_l, width), lambda g, t: (0, g)),
                  pl.BlockSpec((1, HGRN_HEAD), lambda g, t: (0, 0))],
        out_specs=pl.BlockSpec((tb, width), lambda g, t: (t, g)),
        out_shape=jax.ShapeDtypeStruct((m, d_a), BF16),
        scratch_shapes=[pltpu.VMEM((hp, HGRN_HEAD, HGRN_HEAD), F32)],
        compiler_params=_params(("parallel", "arbitrary")),
        name="hgrn2",
    )(p, p, p, p, lb_logits, gn.reshape(1, HGRN_HEAD))


def _rglru_kernel(rx_ref, ry_ref, rxp_ref, cw_ref, cb_ref, wa_ref, ba_ref, wi_ref, bi_ref, lam_ref,
                  o_ref, ext_ref, a_ref, x_ref, hs_ref, h_ref, *, taps):
    tm, width = rx_ref.shape
    i = pl.program_id(0)

    @pl.when(i == 0)
    def _():
        h_ref[...] = jnp.zeros_like(h_ref)

    ext_ref[0:8, :] = jnp.where(i > 0, rxp_ref[...], 0.0)
    ext_ref[8:8 + tm, :] = rx_ref[...]
    u = jnp.zeros((tm, width), F32) + cb_ref[...]
    for j in range(taps):
        s = 8 - (taps - 1) + j
        u = u + cw_ref[j:j + 1, :] * ext_ref[s:s + tm, :]

    u_b = u.astype(BF16)
    r_parts, i_parts = [], []
    for n in range(width // RG_BLOCK):
        blk = slice(n * RG_BLOCK, (n + 1) * RG_BLOCK)
        r_parts.append(jnp.dot(u_b[:, blk], wa_ref[n], preferred_element_type=F32))
        i_parts.append(jnp.dot(u_b[:, blk], wi_ref[n], preferred_element_type=F32))
    r = jax.nn.sigmoid(jnp.concatenate(r_parts, axis=1) + ba_ref[...])
    ig = jax.nn.sigmoid(jnp.concatenate(i_parts, axis=1) + bi_ref[...])

    neg_lam = -lam_ref[...]
    softplus = jnp.maximum(neg_lam, 0.0) + jnp.log1p(jnp.exp(-jnp.abs(neg_lam)))
    log_a = -RG_C * r * softplus
    row = i * tm + lax.broadcasted_iota(jnp.int32, (tm, 1), 0)
    a = jnp.exp(log_a)
    xin = jnp.sqrt(1.0 - a * a) * (ig * u)
    a_ref[...] = a
    x_ref[...] = jnp.where(row >= PAD_ROWS, xin, 0.0)

    sub = lax.broadcasted_iota(jnp.int32, (8, width), 0)

    def group(gidx, h):
        base = pl.multiple_of(gidx * 8, 8)
        a8 = a_ref[pl.ds(base, 8), :]
        x8 = x_ref[pl.ds(base, 8), :]
        for shift in (1, 2, 4):
            keep = sub >= shift
            a_up = jnp.where(keep, pltpu.roll(a8, shift, axis=0), 1.0)
            x_up = jnp.where(keep, pltpu.roll(x8, shift, axis=0), 0.0)
            x8 = a8 * x_up + x8
            a8 = a8 * a_up
        h8 = a8 * h + x8
        hs_ref[pl.ds(base, 8), :] = h8
        return h8[7:8, :]

    h_ref[...] = lax.fori_loop(0, tm // 8, group, h_ref[...])
    o_ref[...] = (hs_ref[...] * jax.nn.gelu(ry_ref[...])).astype(BF16)


def _rglru(p, cw, cb, wa, ba, wi, bi, lam, width):
    m = p.shape[0]
    taps = cw.shape[0]
    tm = _pick_tile(m, ROW_TILE, FRONT)
    nb = width // RG_BLOCK
    row = lambda v: v.reshape(1, width)
    full2 = lambda shape: pl.BlockSpec(shape, lambda i: (0, 0))
    full3 = lambda shape: pl.BlockSpec(shape, lambda i: (0, 0, 0))
    return pl.pallas_call(
        functools.partial(_rglru_kernel, taps=taps),
        grid=(m // tm,),
        in_specs=[pl.BlockSpec((tm, width), lambda i: (i, 0)),
                  pl.BlockSpec((tm, width), lambda i: (i, 1)),
                  pl.BlockSpec((8, width), lambda i: (jnp.maximum(i * (tm // 8) - 1, 0), 0)),
                  full2((taps, width)), full2((1, width)),
                  full3((nb, RG_BLOCK, RG_BLOCK)), full2((1, width)),
                  full3((nb, RG_BLOCK, RG_BLOCK)), full2((1, width)),
                  full2((1, width))],
        out_specs=pl.BlockSpec((tm, width), lambda i: (i, 0)),
        out_shape=jax.ShapeDtypeStruct((m, width), BF16),
        scratch_shapes=[pltpu.VMEM((tm + 8, width), F32),
                        pltpu.VMEM((tm, width), F32),
                        pltpu.VMEM((tm, width), F32),
                        pltpu.VMEM((tm, width), F32),
                        pltpu.VMEM((1, width), F32)],
        compiler_params=_params(("arbitrary",)),
        name="rglru",
    )(p, p, p, cw, row(cb), wa, row(ba), wi, row(bi), row(lam))


def _latent_kernel(c_ref, ct_ref, g_ref, gt_ref, o_ref, ot_ref):
    o_ref[...] = _rms(c_ref[...], g_ref[...]).astype(BF16)
    ct = ct_ref[...]
    ot_ref[0:KV_RANK, :] = (ct * lax.rsqrt(jnp.mean(ct * ct, axis=0, keepdims=True) + EPS)
                            * gt_ref[...]).astype(BF16)
    first = lax.broadcasted_iota(jnp.int32, (ONES_ROWS, ct.shape[1]), 0) == 0
    ot_ref[KV_RANK:KV_RANK + ONES_ROWS, :] = jnp.where(first, 1.0, 0.0).astype(BF16)


def _latent(p, pt, g, col0, row0):
    m = p.shape[0]
    r = g.shape[0]
    tm = _pick_tile(m, ROW_TILE, FRONT)
    return pl.pallas_call(
        _latent_kernel,
        grid=(m // tm,),
        in_specs=[pl.BlockSpec((tm, r), lambda i: (i, col0 // r)),
                  pl.BlockSpec((r, tm), lambda i: (row0 // r, i)),
                  pl.BlockSpec((1, r), lambda i: (0, 0)),
                  pl.BlockSpec((r, 1), lambda i: (0, 0))],
        out_specs=[pl.BlockSpec((tm, r), lambda i: (i, 0)),
                   pl.BlockSpec((r + ONES_ROWS, tm), lambda i: (0, i))],
        out_shape=[jax.ShapeDtypeStruct((m, r), BF16), jax.ShapeDtypeStruct((r + ONES_ROWS, m), BF16)],
        compiler_params=_params(("parallel",)),
        name="latent_norm",
    )(p, pt, g.reshape(1, r), g.reshape(r, 1))


def _dsa_kernel(qt_ref, iqt_ref, iwt_ref, c_ref, ct_ref, ik_ref, wuk_ref, wuvt_ref, o_ref,
                sc_ref, iqp_ref, qlt_ref, acc_ref, m_ref, sa_ref, sb_ref, *, k_sel):
    i = pl.program_id(0)
    tq = Q_BLOCK
    n_heads = ATT_HEADS
    sub_blocks = KEY_CHUNK // tq
    n_chunks = (i * tq + KEY_CHUNK - 1) // KEY_CHUNK

    scale = ATT_HEAD_DIM ** -0.5 * LOG2_E
    for h in range(n_heads):
        qh = qt_ref[h * ATT_HEAD_DIM:(h + 1) * ATT_HEAD_DIM, :].astype(BF16)
        ql = jnp.dot(wuk_ref[h], qh, preferred_element_type=F32) * scale
        qlt_ref[:, h * tq:(h + 1) * tq] = ql.astype(BF16)

    k_loc = lax.broadcasted_iota(jnp.int32, (tq, tq), 0)
    q_loc = lax.broadcasted_iota(jnp.int32, (tq, tq), 1)
    head_cols = [slice(h * tq, (h + 1) * tq) for h in range(n_heads)]

    def masked_scores(rows, bias, s_ref):
        bias2 = jnp.concatenate([bias, bias], axis=1)
        cmax = []
        for hp in range(n_heads // 2):
            cols = slice(2 * hp * tq, (2 * hp + 2) * tq)
            sm = jnp.dot(c_ref[rows, :], qlt_ref[:, cols], preferred_element_type=F32) + bias2
            s_ref[:, cols] = sm
            cmax.append(jnp.max(sm, axis=0, keepdims=True))
        return jnp.concatenate(cmax, axis=1)

    def accumulate(rows, s_ref, cmax):
        m_old = m_ref[...]
        m_new = jnp.maximum(m_old, cmax)
        m_ref[...] = m_new
        p = jnp.exp2(s_ref[...] - m_new).astype(BF16)
        acc_ref[...] = jnp.exp2(m_old - m_new) * acc_ref[...] + jnp.dot(
            ct_ref[:, rows], p, preferred_element_type=F32)

    m_ref[...] = jnp.full(m_ref.shape, NEG_BIG, F32)
    acc_ref[...] = jnp.zeros_like(acc_ref)

    q_row = i * tq + q_loc
    allowed0 = jnp.logical_and(jnp.logical_or(k_loc >= PAD_ROWS, k_loc == q_row), k_loc <= q_row)
    s0_ref = sa_ref.at[0:FRONT, :]
    cmax0 = masked_scores(slice(0, FRONT), jnp.where(allowed0, 0.0, NEG_BIG), s0_ref)
    accumulate(slice(0, FRONT), s0_ref, cmax0)

    iw = iwt_ref[...] * ((IDX_DIM ** -0.5) * (IDX_HEADS ** -0.5))

    for hp in range(IDX_HEADS // 2):
        pair = jnp.concatenate(
            [iqt_ref[(2 * hp) * IDX_DIM:(2 * hp + 1) * IDX_DIM, :],
             iqt_ref[(2 * hp + 1) * IDX_DIM:(2 * hp + 2) * IDX_DIM, :]], axis=1).astype(BF16)
        iqp_ref[hp, 0:IDX_DIM, :] = pair
        iqp_ref[hp, IDX_DIM:2 * IDX_DIM, :] = jnp.zeros_like(pair)

    def score_chunk(j, carry):
        mn, mx = carry
        for u in range(sub_blocks):
            kb = j * sub_blocks + 1 + u
            r0 = pl.multiple_of(kb * tq, tq)
            ikb = ik_ref[pl.ds(r0, tq), :]
            s = jnp.zeros((tq, tq), F32)
            for hp in range(IDX_HEADS // 2):
                x = jnp.dot(ikb, iqp_ref[hp], preferred_element_type=F32)
                s = s + jnp.maximum(x[:, 0:tq], 0.0) * iw[2 * hp:2 * hp + 1, :]
                s = s + jnp.maximum(x[:, tq:2 * tq], 0.0) * iw[2 * hp + 1:2 * hp + 2, :]
            visible = jnp.logical_or(kb < i, jnp.logical_and(kb == i, k_loc <= q_loc))
            sc_ref[pl.ds(r0, tq), :] = jnp.where(visible, s, -jnp.inf)
            mx = jnp.maximum(mx, jnp.max(jnp.where(visible, s, -jnp.inf).reshape(tq // 8, 8, tq), axis=0))
            mn = jnp.minimum(mn, jnp.min(jnp.where(visible, s, jnp.inf).reshape(tq // 8, 8, tq), axis=0))
        return mn, mx

    mn, mx = lax.fori_loop(0, n_chunks, score_chunk,
                           (jnp.full((8, tq), jnp.inf, F32), jnp.full((8, tq), -jnp.inf, F32)))
    row_min = jnp.min(mn, axis=0, keepdims=True)
    row_max = jnp.max(mx, axis=0, keepdims=True)

    groups = KEY_CHUNK // 64

    def chunk_scores(j):
        r0 = pl.multiple_of(FRONT + j * KEY_CHUNK, FRONT)
        return sc_ref[pl.ds(r0, KEY_CHUNK), :].reshape(groups, 8, 8, tq)

    def count_ge(t):
        def body(j, cnt):
            kk = chunk_scores(j)
            for g in range(groups):
                cnt = jnp.where(kk[g] >= t, cnt + 1, cnt)
            return cnt
        cnt = lax.fori_loop(0, n_chunks, body, jnp.zeros((8, 8, tq), jnp.int32))
        return jnp.sum(jnp.sum(cnt, axis=0), axis=0, keepdims=True)

    lane = lax.broadcasted_iota(jnp.int32, (1, tq), 1)
    n_visible = (i - 1) * tq + lane + 1
    k_row = jnp.minimum(k_sel, n_visible)

    def probe(mid, movable, lo, hi, cnt_lo):
        c = count_ge(mid)
        up = jnp.logical_and(c >= k_row, movable)
        down = jnp.logical_and(jnp.logical_not(up), movable)
        return jnp.where(up, mid, lo), jnp.where(down, mid, hi), jnp.where(up, c, cnt_lo)

    def midpoint(lo, hi, cnt_lo):
        mid = lo + 0.5 * (hi - lo)
        movable = jnp.logical_and(cnt_lo != k_row, jnp.logical_and(mid > lo, mid < hi))
        return mid, movable

    def lane_flags(lo, hi, cnt_lo):
        moving = midpoint(lo, hi, cnt_lo)[1].astype(jnp.int32)
        return jnp.sum(moving + 256 * (cnt_lo > k_row).astype(jnp.int32))

    lo, hi, cnt_lo = probe(row_max, n_visible > k_row, row_min, row_max, n_visible)

    def bisect_cond(state):
        return jnp.logical_and(state[0] % 256 > 0, state[1] < BISECT_CAP)

    def bisect_steps(_, bracket):
        lo, hi, cnt_lo = bracket
        for _ in range(BISECT_STEPS):
            mid, movable = midpoint(lo, hi, cnt_lo)
            lo, hi, cnt_lo = probe(mid, movable, lo, hi, cnt_lo)
        return lo, hi, cnt_lo

    def bisect_body(state):
        lo, hi, cnt_lo = bisect_steps(0, state[2:])
        return lane_flags(lo, hi, cnt_lo), state[1] + 1, lo, hi, cnt_lo

    lo, hi, cnt_lo = lax.fori_loop(0, BISECT_BLIND, bisect_steps, (lo, hi, cnt_lo))
    state = (lane_flags(lo, hi, cnt_lo), jnp.int32(0), lo, hi, cnt_lo)
    flags, _, thr, _, cnt_thr = lax.while_loop(bisect_cond, bisect_body, state)

    tied = cnt_thr > k_row

    @pl.when(flags >= 256)
    def _():
        key_idx = lax.broadcasted_iota(jnp.int32, (groups, 8, 8, tq), 0) * 64 + (
            lax.broadcasted_iota(jnp.int32, (groups, 8, 8, tq), 1) * 8
            + lax.broadcasted_iota(jnp.int32, (groups, 8, 8, tq), 2))

        def count_where(pred):
            def body(j, cnt):
                hit = pred(chunk_scores(j), j * KEY_CHUNK + key_idx)
                return cnt + jnp.sum(hit.astype(jnp.int32), axis=0)
            cnt = lax.fori_loop(0, n_chunks, body, jnp.zeros((8, 8, tq), jnp.int32))
            return jnp.sum(jnp.sum(cnt, axis=0), axis=0, keepdims=True)

        wanted = k_row - count_where(lambda kk, idx: kk > thr)

        def index_step(_, bounds):
            below, cap = bounds
            mid = below + lax.shift_right_logical(cap - below, 1)
            enough = count_where(lambda kk, idx: jnp.logical_and(kk == thr, idx <= mid)) >= wanted
            return jnp.where(enough, below, mid), jnp.where(enough, mid, cap)

        n_keys = n_chunks * KEY_CHUNK
        steps = max(1, (sc_ref.shape[0] - 1).bit_length())
        _, cap = lax.fori_loop(0, steps, index_step,
                               (jnp.full((1, tq), -1, jnp.int32), jnp.full((1, tq), 1, jnp.int32) * (n_keys - 1)))

        def drop_chunk(j, carry):
            r0 = pl.multiple_of(FRONT + j * KEY_CHUNK, FRONT)
            kk = chunk_scores(j)
            extra = jnp.logical_and(jnp.logical_and(kk == thr, j * KEY_CHUNK + key_idx > cap), tied)
            sc_ref[pl.ds(r0, KEY_CHUNK), :] = jnp.where(extra, -jnp.inf, kk).reshape(KEY_CHUNK, tq)
            return carry

        lax.fori_loop(0, n_chunks, drop_chunk, 0)

    def chunk_rows(j):
        return pl.ds(pl.multiple_of(FRONT + j * KEY_CHUNK, FRONT), KEY_CHUNK)

    def chunk_scores_masked(j, s_ref):
        rows = chunk_rows(j)
        return masked_scores(rows, jnp.where(sc_ref[rows, :] >= thr, 0.0, NEG_BIG), s_ref)

    @pl.when(i > 0)
    def _():
        def chunk_pair(t, cmax_a):
            j = 2 * t
            cmax_b = chunk_scores_masked(j + 1, sb_ref)
            accumulate(chunk_rows(j), sa_ref, cmax_a)
            cmax_a = chunk_scores_masked(j + 2, sa_ref)
            accumulate(chunk_rows(j + 1), sb_ref, cmax_b)
            return cmax_a

        n_pairs = (n_chunks - 1) // 2
        cmax_a = lax.fori_loop(0, n_pairs, chunk_pair, chunk_scores_masked(0, sa_ref))
        j = 2 * n_pairs

        @pl.when(n_chunks - j == 1)
        def _():
            accumulate(chunk_rows(j), sa_ref, cmax_a)

        @pl.when(n_chunks - j == 2)
        def _():
            cmax_b = chunk_scores_masked(j + 1, sb_ref)
            accumulate(chunk_rows(j), sa_ref, cmax_a)
            accumulate(chunk_rows(j + 1), sb_ref, cmax_b)

    o_lat = (acc_ref[0:KV_RANK, :] / acc_ref[KV_RANK:KV_RANK + 1, :]).astype(BF16)
    for h in range(n_heads):
        oh = jnp.dot(wuvt_ref[h], o_lat[:, head_cols[h]], preferred_element_type=F32)
        o_ref[:, h * ATT_HEAD_DIM:(h + 1) * ATT_HEAD_DIM] = oh.T.astype(BF16)


def _dsa(pt, c, ct, ik, wuk, wuvt, k_sel, q_row0, iq_row0, iw_row0):
    m = c.shape[0]
    tq = Q_BLOCK
    d_q = ATT_HEADS * ATT_HEAD_DIM
    d_iq = IDX_HEADS * IDX_DIM
    full2 = lambda shape: pl.BlockSpec(shape, lambda i: (0, 0))
    full3 = lambda shape: pl.BlockSpec(shape, lambda i: (0, 0, 0))
    return pl.pallas_call(
        functools.partial(_dsa_kernel, k_sel=k_sel),
        grid=(m // tq,),
        in_specs=[pl.BlockSpec((d_q, tq), lambda i: (q_row0 // d_q, i)),
                  pl.BlockSpec((d_iq, tq), lambda i: (iq_row0 // d_iq, i)),
                  pl.BlockSpec((IDX_HEADS, tq), lambda i: (iw_row0 // IDX_HEADS, i)),
                  full2(c.shape), full2(ct.shape), full2(ik.shape),
                  full3(wuk.shape), full3(wuvt.shape)],
        out_specs=pl.BlockSpec((tq, d_q), lambda i: (i, 0)),
        out_shape=jax.ShapeDtypeStruct((m, d_q), BF16),
        scratch_shapes=[pltpu.VMEM((m, tq), F32),
                        pltpu.VMEM((IDX_HEADS // 2, 2 * IDX_DIM, 2 * tq), BF16),
                        pltpu.VMEM((KV_RANK, ATT_HEADS * tq), BF16),
                        pltpu.VMEM((KV_RANK + ONES_ROWS, ATT_HEADS * tq), F32),
                        pltpu.VMEM((1, ATT_HEADS * tq), F32),
                        pltpu.VMEM((KEY_CHUNK, ATT_HEADS * tq), F32),
                        pltpu.VMEM((KEY_CHUNK, ATT_HEADS * tq), F32)],
        compiler_params=_params(("arbitrary",)),
        name="dsa",
    )(pt, pt, pt, c, ct, ik, wuk, wuvt)


def _pad_cols(w, n):
    return jnp.pad(w, ((0, 0), (0, n - w.shape[1])))


def kernel(x, meta_tokens, ln_mix_pre, ln_mix_post, ln_ffn_pre, ln_ffn_post, ffn_w1, ffn_w3, ffn_w2,
           ab_w_in, ab_w_out, hgrn_lb_logits, hgrn_out_norm, sconv_w,
           cd_w_in, cd_w_out, rg_conv_w, rg_conv_b, rg_w_a, rg_b_a, rg_w_i, rg_b_i, rg_lambda,
           mla_kv_norm, mla_w_uk, mla_w_uv):
    assert x.shape[0] == 1
    seq, d = x.shape[1], x.shape[2]
    assert seq % KEY_CHUNK == 0
    d_a = hgrn_lb_logits.shape[1]
    d_c = rg_lambda.shape[1]
    d_d = ATT_HEADS * ATT_HEAD_DIM
    d_iq = IDX_HEADS * IDX_DIM
    k_sel = min(TOPK_MAX, seq // 4)

    x2 = x[0]
    front = jnp.concatenate([jnp.zeros((PAD_ROWS, d), F32), meta_tokens.astype(F32)], axis=0)
    w1, w3, w2 = ffn_w1.astype(BF16), ffn_w3.astype(BF16), ffn_w2.astype(BF16)

    p0 = _norm_mm(x2, ln_mix_pre[0], ab_w_in[0].astype(BF16), tn=512, front=front)
    og = _hgrn(p0, hgrn_lb_logits, hgrn_out_norm[0], lb_row=0)
    h = _out_proj_conv(og, p0, sconv_w[0], 4 * d_a, ab_w_out[0].astype(BF16), x2, front, ln_mix_post[0])
    h = _ffn(h, ln_ffn_pre[0], w1, w3, w2, ln_ffn_post[0], layer=0)

    w_in = cd_w_in[0]
    o_rx, o_q, o_c = 0, 2 * d_c, 2 * d_c + d_d
    o_iq = o_c + KV_RANK
    o_ik = o_iq + d_iq
    o_iw = o_ik + IDX_DIM
    w_rows = jnp.concatenate([w_in[:, o_rx:o_q], w_in[:, o_c:o_iq], w_in[:, o_ik:o_iw]], axis=1)
    w_rows = _pad_cols(w_rows, -(-w_rows.shape[1] // 512) * 512).astype(BF16)
    w_cols = jnp.concatenate([w_in[:, o_q:o_c], w_in[:, o_iq:o_ik], w_in[:, o_c:o_iq], w_in[:, o_iw:]], axis=1)
    w_cols = _pad_cols(w_cols, -(-w_cols.shape[1] // 512) * 512).astype(BF16).T
    p1, p1t = _norm_mm_both(h, ln_mix_pre[1], w_rows, w_cols, tn=w_rows.shape[1] // 2)

    hc = _rglru(p1, rg_conv_w[0], rg_conv_b[0], rg_w_a[0].astype(BF16), rg_b_a[0],
                rg_w_i[0].astype(BF16), rg_b_i[0], rg_lambda[0], width=d_c)
    c, ct = _latent(p1, p1t, mla_kv_norm[0], col0=2 * d_c, row0=d_d + d_iq)
    ik = p1[:, 2 * d_c + KV_RANK:2 * d_c + KV_RANK + 2 * IDX_DIM].astype(BF16)
    wuk = jnp.transpose(mla_w_uk[0], (1, 0, 2)).astype(BF16)
    wuvt = jnp.transpose(mla_w_uv[0], (1, 2, 0)).astype(BF16)
    att = _dsa(p1t, c, ct, ik, wuk, wuvt, k_sel,
               q_row0=0, iq_row0=d_d, iw_row0=d_d + d_iq + KV_RANK)
    h = _out_proj(hc, att, cd_w_out[0].astype(BF16), h, ln_mix_post[1])
    return _ffn(h, ln_ffn_pre[1], w1, w3, w2, ln_ffn_post[1], layer=1, skip_front=True)[None]
```
